```python
import math
import jax, jax.numpy as jnp
from jax import lax
import numpy as np

D_MODEL = 2048
BATCH = 1
SEQ = 16384
DEPTH = 1
DEC_BATCH = 32
DEC_SEQ = 1
PAST_LEN = 16384
PAGE_SIZE = 128

NSA_HEADS = 16
NSA_GROUPS = 4
NSA_HPG = NSA_HEADS // NSA_GROUPS
HEAD_DIM = 128
CMP_BLOCK = 32
CMP_STRIDE = 16
CMP_RATIO = CMP_BLOCK // CMP_STRIDE
SEL_BLOCK = 64
N_SELECT = 16
WINDOW = 512
Q_BLOCK = 128
SEL_BONUS = 1.0e6
HG_HEADS = 8
HG_DK = 128
HG_DV = 128
HG_CHUNK = 64
D_FF = 5504
CONV_W = 3
EPS = 1e-6

NSA_Q = NSA_HEADS * HEAD_DIM
NSA_KV = NSA_GROUPS * HEAD_DIM
HG_QK = HG_HEADS * HG_DK
HG_V = HG_HEADS * HG_DV
SPLITS = [NSA_Q, 6 * NSA_KV, 3 * NSA_HEADS, HG_QK, HG_QK, HG_V, HG_V, D_MODEL, D_MODEL]
SPLIT_POINTS = [int(v) for v in np.cumsum(SPLITS)[:-1]]
IN_WIDTH = sum(SPLITS)

kernel_name = 'nsa_hgrn2_convffn_hybrid_step'


def rmsnorm(x, g):
    xf = x.astype(jnp.float32)
    y = xf * lax.rsqrt(jnp.mean(xf * xf, axis=-1, keepdims=True) + EPS)
    return (y * g.astype(jnp.float32)).astype(x.dtype)


def masked_softmax(s, mask):
    s = jnp.where(mask, s.astype(jnp.float32), -jnp.inf)
    m = jnp.max(s, axis=-1, keepdims=True)
    m = jnp.where(jnp.isfinite(m), m, 0.0)
    p = jnp.exp(s - m)
    return p / jnp.maximum(jnp.sum(p, axis=-1, keepdims=True), 1e-30)


def alibi_slopes():
    h = np.arange(1, NSA_HEADS + 1, dtype=np.float32)
    return jnp.asarray(2.0 ** (-8.0 * h / NSA_HEADS), dtype=jnp.float32).reshape(NSA_GROUPS, NSA_HPG)


def compress_blocks(k, pos, w1, w2):
    B, Tk, G, Dh = k.shape
    n_chunk = Tk // CMP_STRIDE
    n_cmp = n_chunk - CMP_RATIO + 1
    parts = k[:, :n_chunk * CMP_STRIDE].reshape(B, n_chunk, CMP_STRIDE, G, Dh)
    pos_r = pos.reshape(CMP_RATIO, CMP_STRIDE, Dh)
    w1_r = w1.reshape(CMP_RATIO, CMP_STRIDE, Dh, Dh)
    pre = jnp.einsum('bnpgd,pde->bnge', parts[:, :n_cmp] + pos_r[0][:, None, :], w1_r[0])
    for r in range(1, CMP_RATIO):
        pre = pre + jnp.einsum('bnpgd,pde->bnge', parts[:, r:r + n_cmp] + pos_r[r][:, None, :], w1_r[r])
    return jax.nn.gelu(pre) @ w2


def selection_aggregation(n_cmp, n_sel):
    i = jnp.arange(n_cmp)[:, None] * CMP_STRIDE
    j = jnp.arange(n_sel)[None, :] * SEL_BLOCK
    return ((i <= j + SEL_BLOCK - 1) & (i + CMP_BLOCK - 1 >= j)).astype(jnp.float32)


def nsa_attention(q, gate_logits, kv, win, pos_k, pos_v, w_ck1, w_ck2, w_cv1, w_cv2):
    B, Tq = q.shape[0], q.shape[1]
    Tk = kv.shape[1]
    Tw = win.shape[1]
    q0 = Tk - Tq
    w0 = Tk - Tw
    G, R, Dh = NSA_GROUPS, NSA_HPG, HEAD_DIM
    scale = Dh ** -0.5
    slopes = alibi_slopes()
    k_cmp = compress_blocks(kv[:, :, 0], pos_k, w_ck1, w_ck2)
    v_cmp = compress_blocks(kv[:, :, 1], pos_v, w_cv1, w_cv2)
    n_cmp = k_cmp.shape[1]
    cmp_end = jnp.arange(n_cmp) * CMP_STRIDE + CMP_BLOCK - 1
    n_sel = -(-Tk // SEL_BLOCK)
    pad = n_sel * SEL_BLOCK - Tk
    k_sel = jnp.pad(kv[:, :, 2], ((0, 0), (0, pad), (0, 0), (0, 0))).transpose(0, 2, 1, 3)
    v_sel = jnp.pad(kv[:, :, 3], ((0, 0), (0, pad), (0, 0), (0, 0))).transpose(0, 2, 1, 3)
    agg = selection_aggregation(n_cmp, n_sel)
    top = min(N_SELECT, n_sel)
    win_pad = jnp.pad(win, ((0, 0), (WINDOW, 0), (0, 0), (0, 0), (0, 0)))
    b_idx = jnp.arange(B)[:, None, None, None]
    g_idx = jnp.arange(G)[None, None, :, None]
    j_sel = jnp.arange(n_sel)
    blk_off = jnp.arange(SEL_BLOCK)
    qb = math.gcd(Tq, Q_BLOCK)

    def attend_block(i):
        t0 = i * qb
        qi = lax.dynamic_slice_in_dim(q, t0, qb, axis=1).reshape(B, qb, G, R, Dh)
        gates = jax.nn.sigmoid(lax.dynamic_slice_in_dim(gate_logits, t0, qb, axis=1).astype(jnp.float32)).reshape(B, qb, G, R, 3)
        t = q0 + t0 + jnp.arange(qb)
        d_c = (t[:, None] - cmp_end[None, :]).astype(jnp.float32)
        s_c = jnp.einsum('bqgrd,bngd->bqgrn', qi, k_cmp) * scale - slopes[:, :, None] * d_c[:, None, None, :]
        p_c = masked_softmax(s_c, (d_c >= 0)[:, None, None, :])
        o_c = jnp.einsum('bqgrn,bngd->bqgrd', p_c.astype(v_cmp.dtype), v_cmp)
        imp = jnp.einsum('bqgrn,nj->bqgj', p_c, agg)
        cur = t // SEL_BLOCK
        valid = j_sel[None, :] <= cur[:, None]
        forced = (j_sel[None, :] == 0) | (j_sel[None, :] == cur[:, None]) | (j_sel[None, :] == cur[:, None] - 1)
        imp = jnp.where(valid[None, :, None, :], imp + jnp.where(forced, SEL_BONUS, 0.0)[None, :, None, :], -SEL_BONUS)
        _, idx = lax.top_k(imp, top)
        pos = (idx[..., None] * SEL_BLOCK + blk_off).reshape(B, qb, G, top * SEL_BLOCK)
        ks = k_sel[b_idx, g_idx, pos]
        vs = v_sel[b_idx, g_idx, pos]
        d_s = (t[None, :, None, None] - pos).astype(jnp.float32)
        s_s = jnp.einsum('bqgrd,bqgsd->bqgrs', qi, ks) * scale - slopes[None, None, :, :, None] * d_s[:, :, :, None, :]
        p_s = masked_softmax(s_s, (d_s >= 0)[:, :, :, None, :])
        o_s = jnp.einsum('bqgrs,bqgsd->bqgrd', p_s.astype(vs.dtype), vs)
        wkv = lax.dynamic_slice_in_dim(win_pad, q0 + t0 - w0, qb + WINDOW, axis=1)
        s_pos = q0 + t0 - WINDOW + jnp.arange(qb + WINDOW)
        d_w = t[:, None] - s_pos[None, :]
        m_w = (d_w >= 0) & (d_w < WINDOW) & (s_pos >= w0)[None, :]
        s_w = jnp.einsum('bqgrd,bsgd->bqgrs', qi, wkv[:, :, 0]) * scale - slopes[:, :, None] * d_w.astype(jnp.float32)[:, None, None, :]
        p_w = masked_softmax(s_w, m_w[:, None, None, :])
        o_w = jnp.einsum('bqgrs,bsgd->bqgrd', p_w.astype(wkv.dtype), wkv[:, :, 1])
        o = gates[..., 0:1] * o_c + gates[..., 1:2] * o_s + gates[..., 2:3] * o_w
        return o.reshape(B, qb, G * R * Dh).astype(q.dtype)

    out = lax.map(attend_block, jnp.arange(Tq // qb))
    return out.transpose(1, 0, 2, 3).reshape(B, Tq, G * R * Dh)


def hgrn2_recurrence(q, k, v, log_f, s0):
    B, T, H, K = q.shape
    V = v.shape[-1]
    c = math.gcd(T, HG_CHUNK)
    nc = T // c

    def to_chunks(a):
        return a.reshape(B, nc, c, H, a.shape[-1]).transpose(1, 0, 3, 2, 4).astype(jnp.float32)

    tril = jnp.tril(jnp.ones((c, c), dtype=bool))

    def step(S, inp):
        qc, kc, vc, gc = inp
        Gc = jnp.cumsum(gc, axis=2)
        diff = Gc[:, :, :, None, :] - Gc[:, :, None, :, :]
        decay = jnp.exp(jnp.where(tril[None, None, :, :, None], diff, -jnp.inf))
        A = jnp.einsum('bhtk,bhsk,bhtsk->bhts', qc, kc, decay)
        o = jnp.einsum('bhts,bhsv->bhtv', A, vc) + jnp.einsum('bhtk,bhkv->bhtv', qc * jnp.exp(Gc), S)
        G_last = Gc[:, :, -1:, :]
        S = jnp.exp(G_last[:, :, 0, :, None]) * S + jnp.einsum('bhsk,bhsv->bhkv', kc * jnp.exp(G_last - Gc), vc)
        return S, o

    s_fin, o = lax.scan(step, s0.astype(jnp.float32), (to_chunks(q), to_chunks(k), to_chunks(v), to_chunks(log_f)))
    o = o.transpose(1, 0, 3, 2, 4).reshape(B, T, H, V)
    return o, s_fin


def conv_ffn(u, conv_buf, w_up, conv_w, conv_b, w_down):
    T = u.shape[1]
    a, v = jnp.split(u @ w_up, 2, axis=-1)
    a_ext = jnp.concatenate([conv_buf.astype(a.dtype), a], axis=1)
    conv = conv_b + a_ext[:, 0:T] * conv_w[0]
    for j in range(1, CONV_W):
        conv = conv + a_ext[:, j:j + T] * conv_w[j]
    hidden = jax.nn.silu(conv) * v
    return hidden @ w_down, a_ext[:, T:]


def layer_forward(h, c, kv_past, win_past, s0, conv_buf, lb, win_keep,
                  w_ada, b_ada, g_norm1, w_in, pos_k, pos_v, w_ck1, w_ck2, w_cv1, w_cv2,
                  hg_norm, w_br_a, w_br_b, w_out, g_norm2, w_up, conv_w, conv_b, w_down):
    B, T, _ = h.shape
    mod = (jax.nn.silu(c) @ w_ada + b_ada)[:, None, :]
    sh1, sc1, gt1, sh2, sc2, gt2 = jnp.split(mod, 6, axis=-1)
    u = rmsnorm(h, g_norm1) * (1.0 + sc1) + sh1
    proj = u @ w_in
    q_a, kv_a, gate_a, q_b, f_b, i_b, og_b, m_a, m_b = jnp.split(proj, SPLIT_POINTS, axis=-1)
    kv_rows = kv_a.reshape(B, T, 6, NSA_GROUPS, HEAD_DIM)
    kv_new = kv_rows[:, :, :4]
    win_rows = kv_rows[:, :, 4:]
    kv_all = jnp.concatenate([kv_past.astype(kv_new.dtype), kv_new], axis=1)
    win_all = jnp.concatenate([win_past.astype(win_rows.dtype), win_rows], axis=1)
    o_a = nsa_attention(q_a.reshape(B, T, NSA_HEADS, HEAD_DIM), gate_a.reshape(B, T, NSA_HEADS, 3),
                        kv_all, win_all, pos_k, pos_v, w_ck1, w_ck2, w_cv1, w_cv2)
    z = f_b.astype(jnp.float32).reshape(B, T, HG_HEADS, HG_DK)
    lbh = lb.reshape(HG_HEADS, HG_DK)
    log_f = jnp.log(lbh + (1.0 - lbh) * jax.nn.sigmoid(z))
    k_b = (1.0 - lbh) * jax.nn.sigmoid(-z)
    q_h = jax.nn.silu(q_b).reshape(B, T, HG_HEADS, HG_DK)
    v_h = i_b.reshape(B, T, HG_HEADS, HG_DV)
    o_h, s_new = hgrn2_recurrence(q_h, k_b, v_h, log_f, s0)
    o_h = (rmsnorm(o_h, hg_norm) * jax.nn.silu(og_b.reshape(B, T, HG_HEADS, HG_DV).astype(jnp.float32))).astype(h.dtype).reshape(B, T, HG_V)
    mix = jax.nn.sigmoid(m_a) * (o_a @ w_br_a) + jax.nn.sigmoid(m_b) * (o_h @ w_br_b)
    h = h + gt1 * (mix @ w_out)
    u2 = rmsnorm(h, g_norm2) * (1.0 + sc2) + sh2
    f_out, conv_new = conv_ffn(u2, conv_buf, w_up, conv_w, conv_b, w_down)
    h = h + gt2 * f_out
    win_new = win_all[:, win_all.shape[1] - win_keep:]
    return h, kv_new, win_new, s_new, conv_new


def setup_inputs(seed: int = 0) -> dict:
    key = jax.random.key(seed)
    ks = jax.random.split(key, 32)
    n_pages = PAST_LEN // PAGE_SIZE
    n_used = DEC_BATCH * n_pages
    n_phys = n_used + max(1, n_used // 4)
    wbuf = min(WINDOW, PAST_LEN)

    def nrm(k, shape, scale):
        return jax.random.normal(k, shape, jnp.float32) * scale

    def gain(k, shape):
        return 1.0 + nrm(k, shape, 0.02)

    page_table = jax.random.permutation(ks[5], n_phys)[:n_used].reshape(DEC_BATCH, n_pages).astype(jnp.int32)
    return {
        'x_prompt': nrm(ks[0], (BATCH, SEQ, D_MODEL), 1.0),
        'x_sample': nrm(ks[1], (DEC_BATCH, DEC_SEQ, D_MODEL), 1.0),
        'cache_kv': nrm(ks[2], (DEPTH, n_phys, PAGE_SIZE, 4, NSA_GROUPS, HEAD_DIM), 1.0),
        'cache_win': nrm(ks[3], (DEPTH, DEC_BATCH, wbuf, 2, NSA_GROUPS, HEAD_DIM), 1.0),
        'state_hgrn': nrm(ks[4], (DEPTH, DEC_BATCH, HG_HEADS, HG_DK, HG_DV), 0.5),
        'state_conv': nrm(ks[6], (DEPTH, DEC_BATCH, CONV_W - 1, D_FF), 1.0),
        'page_table': page_table,
        'c_prompt': nrm(ks[7], (BATCH, D_MODEL), 1.0),
        'c_sample': nrm(ks[8], (DEC_BATCH, D_MODEL), 1.0),
        'w_ada': nrm(ks[9], (DEPTH, D_MODEL, 6 * D_MODEL), 0.3 * D_MODEL ** -0.5),
        'b_ada': nrm(ks[10], (DEPTH, 6 * D_MODEL), 0.02),
        'g_norm1': gain(ks[11], (DEPTH, D_MODEL)),
        'w_in': nrm(ks[12], (DEPTH, D_MODEL, IN_WIDTH), D_MODEL ** -0.5),
        'nsa_pos_k': nrm(ks[13], (DEPTH, CMP_BLOCK, HEAD_DIM), 0.1),
        'nsa_pos_v': nrm(ks[14], (DEPTH, CMP_BLOCK, HEAD_DIM), 0.1),
        'w_ck1': nrm(ks[15], (DEPTH, CMP_BLOCK * HEAD_DIM, HEAD_DIM), (CMP_BLOCK * HEAD_DIM) ** -0.5),
        'w_ck2': nrm(ks[16], (DEPTH, HEAD_DIM, HEAD_DIM), HEAD_DIM ** -0.5),
        'w_cv1': nrm(ks[17], (DEPTH, CMP_BLOCK * HEAD_DIM, HEAD_DIM), (CMP_BLOCK * HEAD_DIM) ** -0.5),
        'w_cv2': nrm(ks[18], (DEPTH, HEAD_DIM, HEAD_DIM), HEAD_DIM ** -0.5),
        'hg_lb_logits': nrm(ks[19], (DEPTH + 1, HG_QK), 1.0),
        'hg_norm': gain(ks[20], (DEPTH, HG_DV)),
        'w_br_a': nrm(ks[21], (DEPTH, NSA_Q, D_MODEL), NSA_Q ** -0.5),
        'w_br_b': nrm(ks[22], (DEPTH, HG_V, D_MODEL), HG_V ** -0.5),
        'w_out': nrm(ks[23], (DEPTH, D_MODEL, D_MODEL), D_MODEL ** -0.5),
        'g_norm2': gain(ks[24], (DEPTH, D_MODEL)),
        'w_up': nrm(ks[25], (DEPTH, D_MODEL, 2 * D_FF), D_MODEL ** -0.5),
        'conv_w': nrm(ks[26], (DEPTH, CONV_W, D_FF), CONV_W ** -0.5),
        'conv_b': nrm(ks[27], (DEPTH, D_FF), 0.02),
        'w_down': nrm(ks[28], (DEPTH, D_FF, D_MODEL), D_FF ** -0.5),
        'g_final': gain(ks[29], (D_MODEL,)),
    }


def reference(x_prompt, x_sample, cache_kv, cache_win, state_hgrn, state_conv, page_table, c_prompt, c_sample,
              w_ada, b_ada, g_norm1, w_in, nsa_pos_k, nsa_pos_v, w_ck1, w_ck2, w_cv1, w_cv2,
              hg_lb_logits, hg_norm, w_br_a, w_br_b, w_out, g_norm2, w_up, conv_w, conv_b, w_down, g_final):
    B, T = x_prompt.shape[0], x_prompt.shape[1]
    DB = x_sample.shape[0]
    n_pages = page_table.shape[1]
    past = n_pages * PAGE_SIZE
    win_keep_p = min(WINDOW, T)
    win_keep_s = cache_win.shape[2]
    lower_bounds = jnp.cumsum(jax.nn.softmax(hg_lb_logits.astype(jnp.float32), axis=0), axis=0)
    hp, hs = x_prompt, x_sample
    kv_p, win_p, hg_p, cv_p = [], [], [], []
    kv_s, win_s, hg_s, cv_s = [], [], [], []
    for layer in range(DEPTH):
        prm = (w_ada[layer], b_ada[layer], g_norm1[layer], w_in[layer], nsa_pos_k[layer], nsa_pos_v[layer],
               w_ck1[layer], w_ck2[layer], w_cv1[layer], w_cv2[layer], hg_norm[layer], w_br_a[layer], w_br_b[layer],
               w_out[layer], g_norm2[layer], w_up[layer], conv_w[layer], conv_b[layer], w_down[layer])
        empty_kv = jnp.zeros((B, 0, 4, NSA_GROUPS, HEAD_DIM), hp.dtype)
        empty_win = jnp.zeros((B, 0, 2, NSA_GROUPS, HEAD_DIM), hp.dtype)
        s0 = jnp.zeros((B, HG_HEADS, HG_DK, HG_DV), jnp.float32)
        conv0 = jnp.zeros((B, CONV_W - 1, D_FF), hp.dtype)
        hp, kvn, winn, sn, cvn = layer_forward(hp, c_prompt, empty_kv, empty_win, s0, conv0, lower_bounds[layer], win_keep_p, *prm)
        kv_p.append(kvn); win_p.append(winn); hg_p.append(sn); cv_p.append(cvn)
        kv_past = cache_kv[layer][page_table].reshape(DB, past, 4, NSA_GROUPS, HEAD_DIM)
        hs, kvn, winn, sn, cvn = layer_forward(hs, c_sample, kv_past, cache_win[layer], state_hgrn[layer], state_conv[layer],
                                               lower_bounds[layer], win_keep_s, *prm)
        kv_s.append(kvn); win_s.append(winn); hg_s.append(sn); cv_s.append(cvn)
    y_prompt = rmsnorm(hp, g_final)
    y_sample = rmsnorm(hs, g_final)
    return (y_prompt, y_sample,
            jnp.stack(kv_p), jnp.stack(win_p), jnp.stack(hg_p), jnp.stack(cv_p),
            jnp.stack(kv_s), jnp.stack(win_s), jnp.stack(hg_s), jnp.stack(cv_s))
```

```python
import functools
import math

import numpy as np
import jax
import jax.numpy as jnp
from jax import lax
from jax.experimental import pallas as pl
from jax.experimental.pallas import tpu as pltpu

F32 = jnp.float32
BF16 = jnp.bfloat16

NSA_HEADS = 16
NSA_GROUPS = 4
HPG = NSA_HEADS // NSA_GROUPS
DH = 128
CMP_BLOCK = 32
CMP_STRIDE = 16
SEL_BLOCK = 64
N_SELECT = 16
WINDOW = 512
SEL_BONUS = 1.0e6
HG_HEADS = 8
HG_DK = 128
HG_DV = 128
CONV_W = 3
EPS = 1e-6
PAGE = 128

NEG = -1.0e30
V7X_VMEM_LIMIT = 56 * 1024 * 1024
CHUNKS_PER_PAGE = PAGE // CMP_STRIDE
HG_SUB = 16


def _params(sem):
    return pltpu.CompilerParams(dimension_semantics=sem, vmem_limit_bytes=V7X_VMEM_LIMIT)


def _pick(dim, target, mult):
    if dim <= target:
        return dim
    t = (target // mult) * mult
    while t >= mult:
        if dim % t == 0:
            return t
        t -= mult
    return dim


def _dot(a, b):
    return jnp.dot(a, b, preferred_element_type=F32)


def _dot_nt(a, b):
    return lax.dot_general(a, b, (((1,), (1,)), ((), ())), preferred_element_type=F32)


def _sigmoid(x):
    return 1.0 / (1.0 + jnp.exp(-x))


def _silu(x):
    return x * _sigmoid(x)


def _split3(x):
    a = x.astype(BF16)
    r = x - a.astype(F32)
    b = r.astype(BF16)
    c = (r - b.astype(F32)).astype(BF16)
    return a, b, c


def _mm_body(*refs, n_extra, act, epilogue):
    x_ref, w_ref = refs[0], refs[1]
    extras = refs[2:2 + n_extra]
    outs = refs[2 + n_extra:]
    x = x_ref[...]
    if act is not None:
        x = act(x.astype(F32))
    acc = _dot(x.astype(BF16), w_ref[...].astype(BF16))
    res = epilogue(acc, *[e[...] for e in extras])
    if not isinstance(res, tuple):
        res = (res,)
    for o, r in zip(outs, res):
        o[...] = r.astype(o.dtype)


def mm(x, w, extras=(), epilogue=lambda a: a, out_dtypes=(F32,), act=None, tm=1024, tn=512, name="mm"):
    M, K = x.shape
    N = w.shape[1]
    tm = _pick(M, tm, 8)
    tn = _pick(N, tn, 128)
    in_specs = [pl.BlockSpec((tm, K), lambda i, j: (i, 0)), pl.BlockSpec((K, tn), lambda i, j: (0, j))]
    args = [x, w]
    for arr, kind in extras:
        if kind == "tile":
            in_specs.append(pl.BlockSpec((tm, tn), lambda i, j: (i, j)))
        else:
            in_specs.append(pl.BlockSpec((1, tn), lambda i, j: (0, j)))
        args.append(arr)
    outs = pl.pallas_call(
        functools.partial(_mm_body, n_extra=len(extras), act=act, epilogue=epilogue),
        grid=(M // tm, N // tn),
        in_specs=in_specs,
        out_specs=[pl.BlockSpec((tm, tn), lambda i, j: (i, j)) for _ in out_dtypes],
        out_shape=[jax.ShapeDtypeStruct((M, N), d) for d in out_dtypes],
        compiler_params=_params(("parallel", "parallel")),
        name=name,
    )(*args)
    return outs if len(outs) > 1 else outs[0]


def _norm_body(x_ref, g_ref, sc_ref, sh_ref, o_ref):
    x = x_ref[...].astype(F32)
    y = x * lax.rsqrt(jnp.mean(x * x, axis=-1, keepdims=True) + EPS) * g_ref[...]
    o_ref[...] = (y * (1.0 + sc_ref[...]) + sh_ref[...]).astype(o_ref.dtype)


def norm_mod(x, g, sc, sh, out_dtype=BF16):
    M, D = x.shape
    tm = _pick(M, 256, 8)
    per_row = sc.shape[0] == M and M > 1
    mod_spec = pl.BlockSpec((tm, D), lambda i: (i, 0)) if per_row else pl.BlockSpec((1, D), lambda i: (0, 0))
    return pl.pallas_call(
        _norm_body,
        grid=(M // tm,),
        in_specs=[pl.BlockSpec((tm, D), lambda i: (i, 0)), pl.BlockSpec((1, D), lambda i: (0, 0)), mod_spec, mod_spec],
        out_specs=pl.BlockSpec((tm, D), lambda i: (i, 0)),
        out_shape=jax.ShapeDtypeStruct((M, D), out_dtype),
        compiler_params=_params(("parallel",)),
        name="norm_mod",
    )(x, g, sc, sh)


def _cmpz_body(pt_ref, *refs, P):
    pages = refs[:P]
    wk_ref, wv_ref, zk_ref, zv_ref = refs[P:P + 4]
    ri = lax.broadcasted_iota(jnp.int32, (PAGE, PAGE), 0)
    ci = lax.broadcasted_iota(jnp.int32, (PAGE, PAGE), 1)
    perm = (ci == (ri % CHUNKS_PER_PAGE) * CMP_STRIDE + ri // CHUNKS_PER_PAGE).astype(BF16)
    xp = [_dot(perm, pages[k][...].astype(BF16)) for k in range(P)]
    for kind, (w_ref, z_ref) in enumerate(((wk_ref, zk_ref), (wv_ref, zv_ref))):
        rows = []
        for k in range(P):
            for g in range(NSA_GROUPS):
                col = kind * NSA_GROUPS * DH + g * DH
                pieces = [xp[k][p * CHUNKS_PER_PAGE:(p + 1) * CHUNKS_PER_PAGE, col:col + DH]
                          for p in range(CMP_STRIDE)]
                rows.append(jnp.concatenate(pieces, axis=1))
        y = jnp.concatenate(rows, axis=0).astype(BF16)
        z = _dot(y, w_ref[...])
        z_ref[...] = z.reshape(1, P, NSA_GROUPS, CHUNKS_PER_PAGE, 2 * DH)


def cmpz(rows3, page_tab, wk_ab, wv_ab):
    B, n_pages = page_tab.shape
    P = _pick(n_pages, 8, 1)
    half = 2 * NSA_GROUPS * DH

    def page_map(k):
        return lambda b, j, pt: (pt[b * n_pages + j * P + k], 0, 0)

    grid_spec = pltpu.PrefetchScalarGridSpec(
        num_scalar_prefetch=1,
        grid=(B, n_pages // P),
        in_specs=[pl.BlockSpec((None, PAGE, half), page_map(k)) for k in range(P)]
        + [pl.BlockSpec((CMP_STRIDE * DH, 2 * DH), lambda b, j, pt: (0, 0))] * 2,
        out_specs=[pl.BlockSpec((1, P, NSA_GROUPS, CHUNKS_PER_PAGE, 2 * DH), lambda b, j, pt: (b, j, 0, 0, 0))] * 2,
    )
    zshape = jax.ShapeDtypeStruct((B, n_pages, NSA_GROUPS, CHUNKS_PER_PAGE, 2 * DH), F32)
    return pl.pallas_call(
        functools.partial(_cmpz_body, P=P),
        grid_spec=grid_spec,
        out_shape=[zshape, zshape],
        compiler_params=_params(("parallel", "parallel")),
        name="nsa_cmpz",
    )(page_tab.reshape(-1), *([rows3] * P), wk_ab, wv_ab)


def _cmpfin_body(z_ref, c_ref, w2_ref, o_ref, *, n_chunk):
    z = z_ref[0, :, 0].reshape(n_chunk, 2 * DH)
    nxt = pltpu.roll(z[:, DH:], n_chunk - 1, axis=0)
    pre = z[:, :DH] + nxt + c_ref[0:1, :]
    h = jax.nn.gelu(pre)
    o_ref[0, 0] = _dot(h.astype(BF16), w2_ref[...]).astype(o_ref.dtype)


def cmp_finish(z, c_row8, w2):
    B, n_pages = z.shape[0], z.shape[1]
    n_chunk = n_pages * CHUNKS_PER_PAGE
    return pl.pallas_call(
        functools.partial(_cmpfin_body, n_chunk=n_chunk),
        grid=(B, NSA_GROUPS),
        in_specs=[pl.BlockSpec((1, n_pages, 1, CHUNKS_PER_PAGE, 2 * DH), lambda b, g: (b, 0, g, 0, 0)),
                  pl.BlockSpec((8, DH), lambda b, g: (0, 0)),
                  pl.BlockSpec((DH, DH), lambda b, g: (0, 0))],
        out_specs=pl.BlockSpec((1, 1, n_chunk, DH), lambda b, g: (b, g, 0, 0)),
        out_shape=jax.ShapeDtypeStruct((B, NSA_GROUPS, n_chunk, DH), BF16),
        compiler_params=_params(("parallel", "parallel")),
        name="nsa_cmpfin",
    )(z, c_row8, w2)


def _nsacmp_body(sl_ref, q_ref, kc_ref, vc_ref, agg_ref, oc_ref, mask_ref, idx_ref, *, TQ, NC, NS, q0, top):
    g = pl.program_id(1)
    i = pl.program_id(2)
    t = q0 + i * TQ + lax.broadcasted_iota(jnp.int32, (TQ, 1), 0)
    ce = lax.broadcasted_iota(jnp.int32, (1, NC), 1) * CMP_STRIDE + (CMP_BLOCK - 1)
    d = (t - ce).astype(F32)
    valid = d >= 0.0
    kc = kc_ref[0, 0]
    vc = vc_ref[0, 0]
    psum = jnp.zeros((TQ, NC), F32)
    for r in range(HPG):
        s = _dot_nt(q_ref[0, :, r * DH:(r + 1) * DH], kc) - sl_ref[g * HPG + r] * d
        m = jnp.max(jnp.where(valid, s, NEG), axis=1, keepdims=True)
        p = jnp.where(valid, jnp.exp(s - m), 0.0)
        p = p / jnp.maximum(jnp.sum(p, axis=1, keepdims=True), 1e-30)
        oc_ref[0, :, r * DH:(r + 1) * DH] = _dot(p.astype(BF16), vc)
        psum = psum + p
    ph = psum.astype(BF16)
    plo = (psum - ph.astype(F32)).astype(BF16)
    imp = _dot(ph, agg_ref[...]) + _dot(plo, agg_ref[...])
    cur = t // SEL_BLOCK
    j = lax.broadcasted_iota(jnp.int32, (1, NS), 1)
    forced = (j == 0) | (j == cur) | (j == cur - 1)
    imp = jnp.where(j <= cur, imp + jnp.where(forced, SEL_BONUS, 0.0), -SEL_BONUS)
    impT = imp.T
    io = lax.broadcasted_iota(jnp.int32, (NS, TQ), 0).astype(F32)
    selT = jnp.zeros((NS, TQ), F32)
    picks = []
    for _ in range(top):
        mx = jnp.max(impT, axis=0, keepdims=True)
        am = jnp.min(jnp.where(impT == mx, io, float(NS)), axis=0, keepdims=True)
        hit = io == am
        selT = jnp.where(hit, 1.0, selT)
        impT = jnp.where(hit, -3.0e38, impT)
        picks.append(am)
    mask_ref[0, 0] = selT.T.astype(mask_ref.dtype)
    idx_ref[0, 0] = jnp.concatenate(picks, axis=0).astype(jnp.int32)


def nsa_cmp(q, kc, vc, agg, slopes, q0, TQ=128):
    B, Tq, _ = q.shape
    NC = kc.shape[2]
    NS = agg.shape[1]
    top = N_SELECT
    grid_spec = pltpu.PrefetchScalarGridSpec(
        num_scalar_prefetch=1,
        grid=(B, NSA_GROUPS, Tq // TQ),
        in_specs=[pl.BlockSpec((1, TQ, HPG * DH), lambda b, g, i, sl: (b, i, g)),
                  pl.BlockSpec((1, 1, NC, DH), lambda b, g, i, sl: (b, g, 0, 0)),
                  pl.BlockSpec((1, 1, NC, DH), lambda b, g, i, sl: (b, g, 0, 0)),
                  pl.BlockSpec((NC, NS), lambda b, g, i, sl: (0, 0))],
        out_specs=[pl.BlockSpec((1, TQ, HPG * DH), lambda b, g, i, sl: (b, i, g)),
                   pl.BlockSpec((1, 1, TQ, NS), lambda b, g, i, sl: (b, g, i, 0)),
                   pl.BlockSpec((1, 1, top, TQ), lambda b, g, i, sl: (b, g, 0, i))],
    )
    return pl.pallas_call(
        functools.partial(_nsacmp_body, TQ=TQ, NC=NC, NS=NS, q0=q0, top=top),
        grid_spec=grid_spec,
        out_shape=[jax.ShapeDtypeStruct((B, Tq, NSA_HEADS * DH), F32),
                   jax.ShapeDtypeStruct((B, NSA_GROUPS, Tq, NS), BF16),
                   jax.ShapeDtypeStruct((B, NSA_GROUPS, top, Tq), jnp.int32)],
        compiler_params=_params(("parallel", "parallel", "parallel")),
        name="nsa_cmp",
    )(slopes, q, kc, vc, agg)


def _flash_step(s, allowed, v, m_ref, l_ref, acc_ref, r):
    s = jnp.where(allowed, s, NEG)
    m_old = m_ref[r]
    m_new = jnp.maximum(m_old, jnp.max(s, axis=1, keepdims=True))
    alpha = jnp.exp(m_old - m_new)
    p = jnp.exp(s - m_new)
    l_ref[r] = alpha * l_ref[r] + jnp.sum(p, axis=1, keepdims=True)
    acc_ref[r] = alpha * acc_ref[r] + _dot(p.astype(BF16), v)
    m_ref[r] = m_new


def _nsasel_body(qi_t, ki_t, kw_t, fl_t, sl_ref, q_ref, k_ref, v_ref, kw_ref, vw_ref, mask_ref, e_ref, gate_ref,
                 oc_ref, o_ref, ms, ls, accs, mw, lw, accw, *, TQ, TK):
    g = pl.program_id(0)
    st = pl.program_id(1)
    qi = qi_t[st]
    ki = ki_t[st]
    fl = fl_t[st]
    t0 = qi * TQ
    k0 = ki * TK

    @pl.when((fl & 1) != 0)
    def _init():
        ms[...] = jnp.full(ms.shape, NEG, F32)
        mw[...] = jnp.full(mw.shape, NEG, F32)
        ls[...] = jnp.zeros(ls.shape, F32)
        lw[...] = jnp.zeros(lw.shape, F32)
        accs[...] = jnp.zeros(accs.shape, F32)
        accw[...] = jnp.zeros(accw.shape, F32)

    tq = t0 + lax.broadcasted_iota(jnp.int32, (TQ, 1), 0)
    kp = k0 + lax.broadcasted_iota(jnp.int32, (1, TK), 1)
    dist = tq - kp
    kprel = (kp - t0).astype(F32)
    sel = _dot(mask_ref[0], e_ref[...])
    allowed = (sel > 0.5) & (dist >= 0)
    for r in range(HPG):
        s = _dot_nt(q_ref[:, r * DH:(r + 1) * DH], k_ref[...]) + sl_ref[g * HPG + r] * kprel
        _flash_step(s, allowed, v_ref[...], ms, ls, accs, r)

    @pl.when((fl & 4) != 0)
    def _window():
        allowed_w = (dist >= 0) & (dist < WINDOW)
        for r in range(HPG):
            s = _dot_nt(q_ref[:, r * DH:(r + 1) * DH], kw_ref[...]) + sl_ref[g * HPG + r] * kprel
            _flash_step(s, allowed_w, vw_ref[...], mw, lw, accw, r)

    @pl.when((fl & 2) != 0)
    def _finish():
        gt = gate_ref[...]
        for r in range(HPG):
            o = (gt[:, 3 * r:3 * r + 1] * oc_ref[:, r * DH:(r + 1) * DH]
                 + gt[:, 3 * r + 1:3 * r + 2] * (accs[r] / ls[r])
                 + gt[:, 3 * r + 2:3 * r + 3] * (accw[r] / lw[r]))
            o_ref[:, r * DH:(r + 1) * DH] = o.astype(o_ref.dtype)


def nsa_sel(q, kvb, winb, mask, expand, gates, oc, slopes, TQ=256, TK=512):
    T = q.shape[0]
    NS = mask.shape[2]
    assert TK % TQ == 0 and WINDOW <= TK and T % TK == 0
    qi_l, ki_l, kw_l, fl_l = [], [], [], []
    for qi in range(T // TQ):
        a = (qi * TQ + TQ - 1) // TK
        for ki in range(a + 1):
            win = ki >= a - 1
            qi_l.append(qi)
            ki_l.append(ki)
            kw_l.append(max(ki, a - 1, 0))
            fl_l.append((1 if ki == 0 else 0) | (2 if ki == a else 0) | (4 if win else 0))
    tabs = [jnp.asarray(np.asarray(x, np.int32)) for x in (qi_l, ki_l, kw_l, fl_l)]
    n_steps = len(qi_l)
    G4 = NSA_GROUPS
    grid_spec = pltpu.PrefetchScalarGridSpec(
        num_scalar_prefetch=5,
        grid=(NSA_GROUPS, n_steps),
        in_specs=[
            pl.BlockSpec((TQ, HPG * DH), lambda g, s, qi, ki, kw, fl, sl: (qi[s], g)),
            pl.BlockSpec((TK, DH), lambda g, s, qi, ki, kw, fl, sl: (ki[s], 2 * G4 + g)),
            pl.BlockSpec((TK, DH), lambda g, s, qi, ki, kw, fl, sl: (ki[s], 3 * G4 + g)),
            pl.BlockSpec((TK, DH), lambda g, s, qi, ki, kw, fl, sl: (kw[s], g)),
            pl.BlockSpec((TK, DH), lambda g, s, qi, ki, kw, fl, sl: (kw[s], G4 + g)),
            pl.BlockSpec((1, TQ, NS), lambda g, s, qi, ki, kw, fl, sl: (g, qi[s], 0)),
            pl.BlockSpec((NS, TK), lambda g, s, qi, ki, kw, fl, sl: (0, ki[s])),
            pl.BlockSpec((TQ, 128), lambda g, s, qi, ki, kw, fl, sl: (qi[s], g)),
            pl.BlockSpec((TQ, HPG * DH), lambda g, s, qi, ki, kw, fl, sl: (qi[s], g)),
        ],
        out_specs=pl.BlockSpec((TQ, HPG * DH), lambda g, s, qi, ki, kw, fl, sl: (qi[s], g)),
        scratch_shapes=[pltpu.VMEM((HPG, TQ, 1), F32), pltpu.VMEM((HPG, TQ, 1), F32), pltpu.VMEM((HPG, TQ, DH), F32),
                        pltpu.VMEM((HPG, TQ, 1), F32), pltpu.VMEM((HPG, TQ, 1), F32), pltpu.VMEM((HPG, TQ, DH), F32)],
    )
    return pl.pallas_call(
        functools.partial(_nsasel_body, TQ=TQ, TK=TK),
        grid_spec=grid_spec,
        out_shape=jax.ShapeDtypeStruct((T, NSA_HEADS * DH), BF16),
        compiler_params=_params(("parallel", "arbitrary")),
        name="nsa_sel",
    )(*tabs, slopes, q, kvb, kvb, winb, winb, mask, expand, gates, oc)


def _rowsel(rowg, vals):
    out = vals[0]
    for g in range(1, NSA_GROUPS):
        out = jnp.where(rowg == g, vals[g], out)
    return out


def _nsasels_body(kpos_t, val_t, blk_t, q_ref, *refs, n_slot, Wc, past):
    k_refs = refs[0:4]
    v_refs = refs[4:8]
    (win_ref, knew_ref, vnew_ref, kwnew_ref, vwnew_ref, sl_ref, gc_ref, gs_ref, gw_ref, oc_ref,
     o_ref, m_s, l_s, acc_s) = refs[8:]
    b = pl.program_id(0)
    s = pl.program_id(1)
    q = q_ref[0]
    rowg = lax.broadcasted_iota(jnp.int32, (NSA_HEADS, 1), 0) // HPG
    slope = sl_ref[...][:, 0:1]

    @pl.when(s == 0)
    def _init():
        m_s[...] = jnp.full(m_s.shape, NEG, F32)
        l_s[...] = jnp.zeros(l_s.shape, F32)
        acc_s[...] = jnp.zeros(acc_s.shape, F32)

    base = (b * NSA_GROUPS) * n_slot + s
    sc = _rowsel(rowg, [_dot_nt(q, k_refs[g][0].astype(BF16)) for g in range(NSA_GROUPS)])
    kp0 = _rowsel(rowg, [kpos_t[base + g * n_slot] for g in range(NSA_GROUPS)])
    ok = _rowsel(rowg, [val_t[base + g * n_slot] for g in range(NSA_GROUPS)]) > 0
    dist = (past - kp0 - lax.broadcasted_iota(jnp.int32, (1, SEL_BLOCK), 1)).astype(F32)
    sc = jnp.where(ok, sc - slope * dist, NEG)
    m_old = m_s[...]
    m_new = jnp.maximum(m_old, jnp.max(sc, axis=1, keepdims=True))
    alpha = jnp.exp(m_old - m_new)
    p = jnp.where(ok, jnp.exp(sc - m_new), 0.0)
    pb = p.astype(BF16)
    pv = _rowsel(rowg, [_dot(pb, v_refs[g][0].astype(BF16)) for g in range(NSA_GROUPS)])
    l_s[...] = alpha * l_s[...] + jnp.sum(p, axis=1, keepdims=True)
    acc_s[...] = alpha * acc_s[...] + pv
    m_s[...] = m_new

    @pl.when(s == n_slot - 1)
    def _finish():
        qf = q.astype(F32)
        sn = jnp.sum(qf * knew_ref[0], axis=1, keepdims=True)
        m1 = m_s[...]
        m2 = jnp.maximum(m1, sn)
        a2 = jnp.exp(m1 - m2)
        pn = jnp.exp(sn - m2)
        o_s = (a2 * acc_s[...] + pn * vnew_ref[0]) / (a2 * l_s[...] + pn)
        w = win_ref[0]
        kw = NSA_GROUPS * DH
        sw = _rowsel(rowg, [_dot_nt(q, w[:, g * DH:(g + 1) * DH].astype(BF16)) for g in range(NSA_GROUPS)])
        dw = (Wc - lax.broadcasted_iota(jnp.int32, (1, Wc), 1)).astype(F32)
        okw = dw < float(WINDOW)
        sw = jnp.where(okw, sw - slope * dw, NEG)
        swn = jnp.sum(qf * kwnew_ref[0], axis=1, keepdims=True)
        mwin = jnp.maximum(jnp.max(sw, axis=1, keepdims=True), swn)
        pw = jnp.where(okw, jnp.exp(sw - mwin), 0.0)
        pwn = jnp.exp(swn - mwin)
        pwb = pw.astype(BF16)
        ow = _rowsel(rowg, [_dot(pwb, w[:, kw + g * DH:kw + (g + 1) * DH].astype(BF16)) for g in range(NSA_GROUPS)])
        o_w = (ow + pwn * vwnew_ref[0]) / (jnp.sum(pw, axis=1, keepdims=True) + pwn)
        o = gc_ref[0] * oc_ref[0] + gs_ref[0] * o_s + gw_ref[0] * o_w
        o_ref[0] = o.astype(o_ref.dtype)


def nsa_sel_sample(q16, half_pages, blk, val, kpos, win_rows, knew, vnew, kwnew, vwnew, slope16, gc, gs, gw, oc16, past):
    DB = q16.shape[0]
    n_slot = blk.shape[0] // (DB * NSA_GROUPS)
    Wc = win_rows.shape[1]
    G4 = NSA_GROUPS

    def kmap(g, kind):
        return lambda b, s, kp, vl, bk: (bk[(b * G4 + g) * n_slot + s], 0, kind * G4 + g)

    head = lambda b, s, kp, vl, bk: (b, 0, 0)
    hspec = pl.BlockSpec((1, NSA_HEADS, DH), head)
    grid_spec = pltpu.PrefetchScalarGridSpec(
        num_scalar_prefetch=3,
        grid=(DB, n_slot),
        in_specs=[hspec]
        + [pl.BlockSpec((1, SEL_BLOCK, DH), kmap(g, 2)) for g in range(G4)]
        + [pl.BlockSpec((1, SEL_BLOCK, DH), kmap(g, 3)) for g in range(G4)]
        + [pl.BlockSpec((1, Wc, 2 * G4 * DH), head)]
        + [hspec] * 4
        + [pl.BlockSpec((NSA_HEADS, DH), lambda b, s, kp, vl, bk: (0, 0))]
        + [hspec] * 4,
        out_specs=hspec,
        scratch_shapes=[pltpu.VMEM((NSA_HEADS, 1), F32), pltpu.VMEM((NSA_HEADS, 1), F32), pltpu.VMEM((NSA_HEADS, DH), F32)],
    )
    return pl.pallas_call(
        functools.partial(_nsasels_body, n_slot=n_slot, Wc=Wc, past=past),
        grid_spec=grid_spec,
        out_shape=jax.ShapeDtypeStruct((DB, NSA_HEADS, DH), BF16),
        compiler_params=_params(("parallel", "arbitrary")),
        name="nsa_sel_sample",
    )(kpos, val, blk, q16, *([half_pages] * 8), win_rows, knew, vnew, kwnew, vwnew, slope16, gc, gs, gw, oc16)


def _hgrn_body(q_ref, f_ref, i_ref, og_ref, lb_ref, gn_ref, s0_ref, o_ref, sout_ref, st_ref, *, TC, nT):
    i = pl.program_id(1)

    @pl.when(i == 0)
    def _init():
        st_ref[...] = s0_ref[0].T

    lb = lb_ref[...]
    gn = gn_ref[...]
    C = HG_SUB
    row = lax.broadcasted_iota(jnp.int32, (C, 1), 0)
    tril = (lax.broadcasted_iota(jnp.int32, (C, C), 0) >= lax.broadcasted_iota(jnp.int32, (C, C), 1)).astype(BF16)
    ones = jnp.ones((HG_DK, 128), BF16)

    def sub(c, carry):
        r0 = pl.multiple_of(c * C, C)
        z = f_ref[pl.ds(r0, C), :]
        qr = q_ref[pl.ds(r0, C), :]
        v = i_ref[pl.ds(r0, C), :]
        og = og_ref[pl.ds(r0, C), :]
        f = lb + (1.0 - lb) * _sigmoid(z)
        logf = jnp.log(f)
        kk = (1.0 - lb) * _sigmoid(-z)
        q = _silu(qr)
        a, b2, c2 = _split3(logf)
        G = _dot(tril, a) + _dot(tril, b2) + _dot(tril, c2)
        st = st_ref[...]
        o = _dot_nt((q * jnp.exp(G)).astype(BF16), st.astype(BF16))
        pieces = []
        for s in range(C):
            e = jnp.exp(jnp.where(row >= s, G - G[s:s + 1, :], NEG))
            pieces.append(q * e * kk[s:s + 1, :])
        rs = _dot(jnp.concatenate(pieces, axis=0).astype(BF16), ones)
        for s in range(C):
            o = o + rs[s * C:(s + 1) * C, :] * v[s:s + 1, :]
        gl = G[C - 1:C, :]
        kd = kk * jnp.exp(gl - G)
        upd = lax.dot_general(v.astype(BF16), kd.astype(BF16), (((0,), (0,)), ((), ())), preferred_element_type=F32)
        st_ref[...] = st * jnp.exp(gl) + upd
        y = o * lax.rsqrt(jnp.mean(o * o, axis=-1, keepdims=True) + EPS) * gn * _silu(og)
        o_ref[pl.ds(r0, C), :] = y.astype(o_ref.dtype)
        return carry

    lax.fori_loop(0, TC // C, sub, 0)

    @pl.when(i == nT - 1)
    def _fin():
        sout_ref[0] = st_ref[...].T


def hgrn_prompt(hg, lb, gn, s0, TC=128):
    T = hg.shape[0]
    H = HG_HEADS
    nT = T // TC
    return pl.pallas_call(
        functools.partial(_hgrn_body, TC=TC, nT=nT),
        grid=(H, nT),
        in_specs=[pl.BlockSpec((TC, HG_DK), lambda h, i: (i, h)),
                  pl.BlockSpec((TC, HG_DK), lambda h, i: (i, H + h)),
                  pl.BlockSpec((TC, HG_DV), lambda h, i: (i, 2 * H + h)),
                  pl.BlockSpec((TC, HG_DV), lambda h, i: (i, 3 * H + h)),
                  pl.BlockSpec((1, HG_DK), lambda h, i: (0, h)),
                  pl.BlockSpec((1, HG_DV), lambda h, i: (0, 0)),
                  pl.BlockSpec((1, HG_DK, HG_DV), lambda h, i: (h, 0, 0))],
        out_specs=[pl.BlockSpec((TC, HG_DV), lambda h, i: (i, h)),
                   pl.BlockSpec((1, HG_DK, HG_DV), lambda h, i: (h, 0, 0))],
        out_shape=[jax.ShapeDtypeStruct((T, H * HG_DV), BF16), jax.ShapeDtypeStruct((H, HG_DK, HG_DV), F32)],
        scratch_shapes=[pltpu.VMEM((HG_DV, HG_DK), F32)],
        compiler_params=_params(("parallel", "arbitrary")),
        name="hgrn_prompt",
    )(hg, hg, hg, hg, lb, gn, s0)


def _hgrns_body(qc_ref, zc_ref, v_ref, og_ref, lbc_ref, gn_ref, s_ref, o_ref, sout_ref):
    outs = []
    for h in range(HG_HEADS):
        z = zc_ref[0, h]
        lb = lbc_ref[h]
        f = lb + (1.0 - lb) * _sigmoid(z)
        kk = (1.0 - lb) * _sigmoid(-z)
        q = _silu(qc_ref[0, h])
        v = v_ref[0, h:h + 1, :]
        s_new = f * s_ref[0, h] + kk * v
        sout_ref[0, h] = s_new
        o = jnp.sum(q * s_new, axis=0, keepdims=True)
        og = og_ref[0, h:h + 1, :]
        outs.append(o * lax.rsqrt(jnp.mean(o * o, axis=-1, keepdims=True) + EPS) * gn_ref[...] * _silu(og))
    o_ref[0] = jnp.concatenate(outs, axis=0).astype(o_ref.dtype)


def hgrn_sample(qcol, zcol, v, og, lbcol, gn, s0):
    DB = v.shape[0]
    H = HG_HEADS
    col = pl.BlockSpec((1, H, HG_DK, 1), lambda b: (b, 0, 0, 0))
    rowb = pl.BlockSpec((1, H, HG_DV), lambda b: (b, 0, 0))
    st = pl.BlockSpec((1, H, HG_DK, HG_DV), lambda b: (b, 0, 0, 0))
    return pl.pallas_call(
        _hgrns_body,
        grid=(DB,),
        in_specs=[col, col, rowb, rowb, pl.BlockSpec((H, HG_DK, 1), lambda b: (0, 0, 0)),
                  pl.BlockSpec((1, HG_DV), lambda b: (0, 0)), st],
        out_specs=[rowb, st],
        out_shape=[jax.ShapeDtypeStruct((DB, H, HG_DV), BF16), jax.ShapeDtypeStruct((DB, H, HG_DK, HG_DV), F32)],
        compiler_params=_params(("parallel",)),
        name="hgrn_sample",
    )(qcol, zcol, v, og, lbcol, gn, s0)


def _ffn_body(u_ref, uh_ref, wa_ref, wv_ref, cp_ref, wd_ref, h_ref, gt_ref, gf_ref, y_ref, acc_ref, *, nf, tm):
    i = pl.program_id(0)
    f = pl.program_id(1)
    u = u_ref[...]
    a = _dot(u, wa_ref[...])
    v = _dot(u, wv_ref[...])
    ah = _dot(uh_ref[...], wa_ref[...]) * (i > 0).astype(F32)
    cp = cp_ref[...]
    rows = lax.broadcasted_iota(jnp.int32, (tm, 1), 0)
    a1 = jnp.where(rows == 0, ah[7:8, :], pltpu.roll(a, 1, axis=0))
    a2 = jnp.where(rows == 0, ah[6:7, :], jnp.where(rows == 1, ah[7:8, :], pltpu.roll(a, 2, axis=0)))
    conv = cp[3:4, :] + a2 * cp[0:1, :] + a1 * cp[1:2, :] + a * cp[2:3, :]
    hid = (_silu(conv) * v).astype(BF16)
    contrib = _dot(hid, wd_ref[...])

    @pl.when(f == 0)
    def _first():
        acc_ref[...] = contrib

    @pl.when(f > 0)
    def _rest():
        acc_ref[...] += contrib

    @pl.when(f == nf - 1)
    def _fin():
        h2 = h_ref[...] + gt_ref[...] * acc_ref[...]
        y_ref[...] = h2 * lax.rsqrt(jnp.mean(h2 * h2, axis=-1, keepdims=True) + EPS) * gf_ref[...]


def ffn_prompt(u2, h1, wa, wv, cp, wd, gt2, gf, tm=512, tf=512):
    M, D = u2.shape
    Fp = wa.shape[1]
    tm = _pick(M, tm, 8)
    tf = _pick(Fp, tf, 128)
    nf = Fp // tf
    hb = tm // 8
    return pl.pallas_call(
        functools.partial(_ffn_body, nf=nf, tm=tm),
        grid=(M // tm, nf),
        in_specs=[pl.BlockSpec((tm, D), lambda i, f: (i, 0)),
                  pl.BlockSpec((8, D), lambda i, f: (jnp.maximum(i * hb - 1, 0), 0)),
                  pl.BlockSpec((D, tf), lambda i, f: (0, f)),
                  pl.BlockSpec((D, tf), lambda i, f: (0, f)),
                  pl.BlockSpec((8, tf), lambda i, f: (0, f)),
                  pl.BlockSpec((tf, D), lambda i, f: (f, 0)),
                  pl.BlockSpec((tm, D), lambda i, f: (i, 0)),
                  pl.BlockSpec((1, D), lambda i, f: (0, 0)),
                  pl.BlockSpec((1, D), lambda i, f: (0, 0))],
        out_specs=pl.BlockSpec((tm, D), lambda i, f: (i, 0)),
        out_shape=jax.ShapeDtypeStruct((M, D), F32),
        scratch_shapes=[pltpu.VMEM((tm, D), F32)],
        compiler_params=_params(("parallel", "arbitrary")),
        name="ffn_prompt",
    )(u2, u2, wa, wv, cp, wd, h1, gt2, gf)


def _alibi_slopes():
    h = np.arange(1, NSA_HEADS + 1, dtype=np.float32)
    return np.asarray(2.0 ** (-8.0 * h / NSA_HEADS), dtype=np.float32)


def _agg_matrix(n_chunk, n_cmp, n_sel, ns_pad):
    i = np.arange(n_chunk)[:, None] * CMP_STRIDE
    j = np.arange(ns_pad)[None, :] * SEL_BLOCK
    m = (i <= j + SEL_BLOCK - 1) & (i + CMP_BLOCK - 1 >= j)
    m &= (np.arange(n_chunk)[:, None] < n_cmp) & (np.arange(ns_pad)[None, :] < n_sel)
    return jnp.asarray(m.astype(np.float32), BF16)


def _layer_weights(w_in, w_ck1, w_cv1, w_br_a, w_br_b, w_out, w_up, conv_w, conv_b, w_down, w_ada):
    D = w_in.shape[0]
    nq, nkv = NSA_HEADS * DH, NSA_GROUPS * DH
    o = 0
    w = {}
    w["q"] = w_in[:, o:o + nq]; o += nq
    w["kv4"] = w_in[:, o:o + 4 * nkv]; o += 4 * nkv
    w["win"] = w_in[:, o:o + 2 * nkv]; o += 2 * nkv
    gate = w_in[:, o:o + 3 * NSA_HEADS]; o += 3 * NSA_HEADS
    gate = jnp.pad(gate.reshape(D, NSA_GROUPS, 3 * HPG), ((0, 0), (0, 0), (0, 128 - 3 * HPG)))
    w["gate"] = gate.reshape(D, NSA_GROUPS * 128)
    nh = 4 * HG_HEADS * HG_DK
    w["hg"] = w_in[:, o:o + nh]; o += nh
    w["ma"] = w_in[:, o:o + D]; o += D
    w["mb"] = w_in[:, o:o + D]; o += D
    half = CMP_STRIDE * DH
    w["ck_ab"] = jnp.concatenate([w_ck1[:half], w_ck1[half:]], axis=1)
    w["cv_ab"] = jnp.concatenate([w_cv1[:half], w_cv1[half:]], axis=1)
    w["br_a"], w["br_b"], w["out"], w["ada"] = w_br_a, w_br_b, w_out, w_ada
    F = w_down.shape[0]
    Fp = -(-F // 512) * 512
    w["up_a"] = jnp.pad(w_up[:, :F], ((0, 0), (0, Fp - F)))
    w["up_v"] = jnp.pad(w_up[:, F:], ((0, 0), (0, Fp - F)))
    w["down"] = jnp.pad(w_down, ((0, Fp - F), (0, 0)))
    w = {k: v.astype(BF16) for k, v in w.items()}
    cp = jnp.concatenate([conv_w, conv_b[None, :], jnp.zeros((8 - CONV_W - 1, F), F32)], axis=0)
    w["conv"] = jnp.pad(cp, ((0, 0), (0, Fp - F)))
    return w


def _project(u, w):
    scale = DH ** -0.5
    p = {}
    p["q"] = mm(u, w["q"], epilogue=lambda a: a * scale, out_dtypes=(BF16,), name="proj_q")
    p["kv4"], p["kv4b"] = mm(u, w["kv4"], epilogue=lambda a: (a, a), out_dtypes=(F32, BF16), name="proj_kv")
    p["win"], p["winb"] = mm(u, w["win"], epilogue=lambda a: (a, a), out_dtypes=(F32, BF16), name="proj_win")
    p["gate"] = mm(u, w["gate"], epilogue=_sigmoid, name="proj_gate")
    p["hg"] = mm(u, w["hg"], name="proj_hg")
    p["ma"] = mm(u, w["ma"], epilogue=_sigmoid, out_dtypes=(BF16,), name="proj_ma")
    p["mb"] = mm(u, w["mb"], epilogue=_sigmoid, out_dtypes=(BF16,), name="proj_mb")
    return p


def _compressed_kv(rows3, page_tab, w, c_k, c_v, w_ck2, w_cv2):
    zk, zv = cmpz(rows3, page_tab, w["ck_ab"], w["cv_ab"])
    return cmp_finish(zk, c_k, w_ck2), cmp_finish(zv, c_v, w_cv2)


def _merge_out(oa, oh, p, w, x, gt1):
    gkind = "row" if gt1.shape[0] == 1 else "tile"
    a1 = mm(oa, w["br_a"], extras=[(p["ma"], "tile")], epilogue=lambda a, m: a * m.astype(F32),
            out_dtypes=(BF16,), name="branch_a")
    mix = mm(oh, w["br_b"], extras=[(p["mb"], "tile"), (a1, "tile")],
             epilogue=lambda a, m, prev: a * m.astype(F32) + prev.astype(F32), out_dtypes=(BF16,), name="branch_b")
    return mm(mix, w["out"], extras=[(x, "tile"), (gt1, gkind)], epilogue=lambda a, xr, g: xr + g * a, name="out_proj")


def kernel(x_prompt, x_sample, cache_kv, cache_win, state_hgrn, state_conv, page_table, c_prompt, c_sample, w_ada, b_ada, g_norm1, w_in, nsa_pos_k, nsa_pos_v, w_ck1, w_ck2, w_cv1, w_cv2, hg_lb_logits, hg_norm, w_br_a, w_br_b, w_out, g_norm2, w_up, conv_w, conv_b, w_down, g_final):
    B, T, D = x_prompt.shape
    DB = x_sample.shape[0]
    depth = w_in.shape[0]
    assert B == 1 and x_sample.shape[1] == 1 and depth == 1 and T % 512 == 0 and T >= N_SELECT * SEL_BLOCK
    n_pages = page_table.shape[1]
    past = n_pages * PAGE
    Wc = cache_win.shape[2]
    F = w_down.shape[1]
    slopes = jnp.asarray(_alibi_slopes())
    lower_bounds = jnp.cumsum(jax.nn.softmax(hg_lb_logits.astype(F32), axis=0), axis=0)
    layer = 0
    w = _layer_weights(w_in[layer], w_ck1[layer], w_cv1[layer], w_br_a[layer], w_br_b[layer], w_out[layer],
                       w_up[layer], conv_w[layer], conv_b[layer], w_down[layer], w_ada[layer])
    w_ck2b, w_cv2b = w_ck2[layer].astype(BF16), w_cv2[layer].astype(BF16)
    lb = lower_bounds[layer][None, :]
    gn = hg_norm[layer][None, :]
    g1, g2, gf = g_norm1[layer][None, :], g_norm2[layer][None, :], g_final[None, :]

    n_c = -(-(B + DB) // 8) * 8
    c_all = jnp.pad(jnp.concatenate([c_prompt, c_sample], axis=0), ((0, n_c - B - DB), (0, 0)))
    mod = mm(c_all, w["ada"], extras=[(b_ada[layer][None, :], "row")], epilogue=lambda a, b: a + b, act=_silu, name="adaln")
    sh1, sc1, gt1, sh2, sc2, gt2 = [mod[:, k * D:(k + 1) * D] for k in range(6)]
    ps, ss = slice(0, 1), slice(B, B + DB)

    def pos_term(pos, w1):
        return mm(jnp.pad(pos.reshape(1, -1), ((0, 7), (0, 0))), w1.astype(BF16), name="cmp_pos")
    c_k, c_v = pos_term(nsa_pos_k[layer], w_ck1[layer]), pos_term(nsa_pos_v[layer], w_cv1[layer])

    xp = x_prompt[0]
    u = norm_mod(xp, g1, sc1[ps], sh1[ps])
    p = _project(u, w)
    ident = jnp.arange(T // PAGE, dtype=jnp.int32)[None, :]
    kc, vc = _compressed_kv(p["kv4"].reshape(T // PAGE, PAGE, 4 * NSA_GROUPS * DH), ident, w, c_k, c_v, w_ck2b, w_cv2b)
    n_chunk = T // CMP_STRIDE
    n_sel = T // SEL_BLOCK
    agg = _agg_matrix(n_chunk, n_chunk - 1, n_sel, n_sel)
    oc, mask, _ = nsa_cmp(p["q"][None], kc, vc, agg, slopes, q0=0)
    expand = jnp.asarray(np.kron(np.eye(n_sel, dtype=np.float32), np.ones((1, SEL_BLOCK), np.float32)), BF16)
    oa = nsa_sel(p["q"], p["kv4b"], p["winb"], mask[0], expand, p["gate"], oc[0], slopes)
    s0p = jnp.zeros((HG_HEADS, HG_DK, HG_DV), F32)
    oh, s_new_p = hgrn_prompt(p["hg"], lb, gn, s0p)
    h1 = _merge_out(oa, oh, p, w, xp, gt1[ps])
    u2 = norm_mod(h1, g2, sc2[ps], sh2[ps])
    y_p = ffn_prompt(u2, h1, w["up_a"], w["up_v"], w["conv"], w["down"], gt2[ps], gf)
    a_tail = mm(u2[T - 8:], w["up_a"], name="ffn_tail")
    kv_new_p = p["kv4"].reshape(1, 1, T, 4, NSA_GROUPS, DH)
    wk = min(WINDOW, T)
    win_new_p = p["win"][T - wk:].reshape(1, 1, wk, 2, NSA_GROUPS, DH)
    conv_new_p = a_tail[8 - (CONV_W - 1):, :F].reshape(1, 1, CONV_W - 1, F)

    xs = x_sample[:, 0]
    us = norm_mod(xs, g1, sc1[ss], sh1[ss])
    q = _project(us, w)
    cache3 = cache_kv[layer].reshape(-1, PAGE, 4 * NSA_GROUPS * DH)
    kcs, vcs = _compressed_kv(cache3, page_table, w, c_k, c_v, w_ck2b, w_cv2b)
    n_chunk_s = past // CMP_STRIDE
    n_sel_s = past // SEL_BLOCK + 1
    ns_pad = -(-n_sel_s // 128) * 128
    agg_s = _agg_matrix(n_chunk_s, n_chunk_s - 1, n_sel_s, ns_pad)
    TQS = 128
    q_pad = jnp.pad(q["q"][:, None, :], ((0, 0), (0, TQS - 1), (0, 0)))
    ocs, _, idx = nsa_cmp(q_pad, kcs, vcs, agg_s, slopes, q0=past)
    idx = idx[:, :, :, 0]
    n_past_blk = past // SEL_BLOCK
    jc = jnp.minimum(idx, n_past_blk - 1)
    per_page = PAGE // SEL_BLOCK
    pg = jnp.take_along_axis(page_table[:, None, :], jc // per_page, axis=2)
    blk = (pg * per_page + jc % per_page).astype(jnp.int32).reshape(-1)
    val = (idx < n_past_blk).astype(jnp.int32).reshape(-1)
    kpos = (jc * SEL_BLOCK).astype(jnp.int32).reshape(-1)

    def per_head(rows):
        return jnp.repeat(rows.reshape(DB, NSA_GROUPS, DH), HPG, axis=1)
    nk = NSA_GROUPS * DH
    gate3 = q["gate"].reshape(DB, NSA_GROUPS, 128)[:, :, :3 * HPG].reshape(DB, NSA_HEADS, 3)
    gb = [jnp.broadcast_to(gate3[:, :, k:k + 1], (DB, NSA_HEADS, 128)) for k in range(3)]
    slope16 = jnp.broadcast_to(slopes[:, None], (NSA_HEADS, DH))
    oas = nsa_sel_sample(
        q["q"].reshape(DB, NSA_HEADS, DH), cache_kv[layer].reshape(-1, SEL_BLOCK, 4 * NSA_GROUPS * DH), blk, val, kpos,
        cache_win[layer].reshape(DB, Wc, 2 * nk),
        per_head(q["kv4"][:, 2 * nk:3 * nk]), per_head(q["kv4"][:, 3 * nk:4 * nk]),
        per_head(q["win"][:, :nk]), per_head(q["win"][:, nk:]),
        slope16, gb[0], gb[1], gb[2], ocs[:, 0].reshape(DB, NSA_HEADS, DH), past)
    hq, hz, hv, hog = [q["hg"][:, k * HG_HEADS * HG_DK:(k + 1) * HG_HEADS * HG_DK] for k in range(4)]
    ohs, s_new_s = hgrn_sample(hq.reshape(DB, HG_HEADS, HG_DK, 1), hz.reshape(DB, HG_HEADS, HG_DK, 1),
                               hv.reshape(DB, HG_HEADS, HG_DV), hog.reshape(DB, HG_HEADS, HG_DV),
                               lb.reshape(HG_HEADS, HG_DK, 1), gn, state_hgrn[layer])
    h1s = _merge_out(oas.reshape(DB, NSA_HEADS * DH), ohs.reshape(DB, HG_HEADS * HG_DV), q, w, xs, gt1[ss])
    u2s = norm_mod(h1s, g2, sc2[ss], sh2[ss])
    Fp = w["up_a"].shape[1]
    a_s = mm(u2s, w["up_a"], name="ffn_s_a")
    buf = jnp.pad(state_conv[layer], ((0, 0), (0, 0), (0, Fp - F)))

    def conv_gate(v, a, b0, b1, cw0, cw1, cw2, cb):
        conv = cb + b0 * cw0 + b1 * cw1 + a * cw2
        return _silu(conv) * v
    hid = mm(u2s, w["up_v"], extras=[(a_s, "tile"), (buf[:, 0], "tile"), (buf[:, 1], "tile")]
             + [(w["conv"][k:k + 1], "row") for k in range(4)], epilogue=conv_gate, out_dtypes=(BF16,), name="ffn_s_v")

    def resid_norm(a, h, g, gfin):
        h2 = h + g * a
        return h2 * lax.rsqrt(jnp.mean(h2 * h2, axis=-1, keepdims=True) + EPS) * gfin
    y_s = mm(hid, w["down"], extras=[(h1s, "tile"), (gt2[ss], "tile"), (gf, "row")], epilogue=resid_norm, tn=D, name="ffn_s_down")

    kv_new_s = q["kv4"].reshape(1, DB, 1, 4, NSA_GROUPS, DH)
    win_new_s = jnp.concatenate([cache_win[layer], q["win"].reshape(DB, 1, 2, NSA_GROUPS, DH)], axis=1)[None, :, 1:]
    conv_new_s = jnp.stack([state_conv[layer][:, 1], a_s[:, :F]], axis=1)[None]
    return (y_p[None], y_s[:, None, :], kv_new_p, win_new_p, s_new_p[None, None], conv_new_p,
            kv_new_s, win_new_s, s_new_s[None], conv_new_s)
```

```python
import functools
import math

import numpy as np
import jax
import jax.numpy as jnp
from jax import lax
from jax.experimental import pallas as pl
from jax.experimental.pallas import tpu as pltpu

F32 = jnp.float32
BF16 = jnp.bfloat16

NSA_HEADS = 16
NSA_GROUPS = 4
HPG = NSA_HEADS // NSA_GROUPS
DH = 128
CMP_BLOCK = 32
CMP_STRIDE = 16
SEL_BLOCK = 64
N_SELECT = 16
WINDOW = 512
SEL_BONUS = 1.0e6
HG_HEADS = 8
HG_DK = 128
HG_DV = 128
CONV_W = 3
EPS = 1e-6
PAGE = 128

NEG = -1.0e30
V7X_VMEM_LIMIT = 56 * 1024 * 1024
CHUNKS_PER_PAGE = PAGE // CMP_STRIDE
HG_SUB = 16


def _params(sem):
    return pltpu.CompilerParams(dimension_semantics=sem, vmem_limit_bytes=V7X_VMEM_LIMIT)


def _pick(dim, target, mult):
    if dim <= target:
        return dim
    t = (target // mult) * mult
    while t >= mult:
        if dim % t == 0:
            return t
        t -= mult
    return dim


def _dot(a, b):
    return jnp.dot(a, b, preferred_element_type=F32)


def _dot_nt(a, b):
    return lax.dot_general(a, b, (((1,), (1,)), ((), ())), preferred_element_type=F32)


def _sigmoid(x):
    return 1.0 / (1.0 + jnp.exp(-x))


def _silu(x):
    return x * _sigmoid(x)


def _split3(x):
    a = x.astype(BF16)
    r = x - a.astype(F32)
    b = r.astype(BF16)
    c = (r - b.astype(F32)).astype(BF16)
    return a, b, c


def _mm_body(*refs, n_extra, act, epilogue):
    x_ref, w_ref = refs[0], refs[1]
    extras = refs[2:2 + n_extra]
    outs = refs[2 + n_extra:]
    x = x_ref[...]
    if act is not None:
        x = act(x.astype(F32))
    acc = _dot(x.astype(BF16), w_ref[...].astype(BF16))
    res = epilogue(acc, *[e[...] for e in extras])
    if not isinstance(res, tuple):
        res = (res,)
    for o, r in zip(outs, res):
        o[...] = r.astype(o.dtype)


def mm(x, w, extras=(), epilogue=lambda a: a, out_dtypes=(F32,), act=None, tm=1024, tn=512, name="mm"):
    M, K = x.shape
    N = w.shape[1]
    tm = _pick(M, tm, 8)
    tn = _pick(N, tn, 128)
    in_specs = [pl.BlockSpec((tm, K), lambda i, j: (i, 0)), pl.BlockSpec((K, tn), lambda i, j: (0, j))]
    args = [x, w]
    for arr, kind in extras:
        if kind == "tile":
            in_specs.append(pl.BlockSpec((tm, tn), lambda i, j: (i, j)))
        else:
            in_specs.append(pl.BlockSpec((1, tn), lambda i, j: (0, j)))
        args.append(arr)
    outs = pl.pallas_call(
        functools.partial(_mm_body, n_extra=len(extras), act=act, epilogue=epilogue),
        grid=(M // tm, N // tn),
        in_specs=in_specs,
        out_specs=[pl.BlockSpec((tm, tn), lambda i, j: (i, j)) for _ in out_dtypes],
        out_shape=[jax.ShapeDtypeStruct((M, N), d) for d in out_dtypes],
        compiler_params=_params(("parallel", "parallel")),
        name=name,
    )(*args)
    return outs if len(outs) > 1 else outs[0]


def _norm_body(x_ref, g_ref, sc_ref, sh_ref, o_ref):
    x = x_ref[...].astype(F32)
    y = x * lax.rsqrt(jnp.mean(x * x, axis=-1, keepdims=True) + EPS) * g_ref[...]
    o_ref[...] = (y * (1.0 + sc_ref[...]) + sh_ref[...]).astype(o_ref.dtype)


def norm_mod(x, g, sc, sh, out_dtype=BF16):
    M, D = x.shape
    tm = _pick(M, 256, 8)
    per_row = sc.shape[0] == M and M > 1
    mod_spec = pl.BlockSpec((tm, D), lambda i: (i, 0)) if per_row else pl.BlockSpec((1, D), lambda i: (0, 0))
    return pl.pallas_call(
        _norm_body,
        grid=(M // tm,),
        in_specs=[pl.BlockSpec((tm, D), lambda i: (i, 0)), pl.BlockSpec((1, D), lambda i: (0, 0)), mod_spec, mod_spec],
        out_specs=pl.BlockSpec((tm, D), lambda i: (i, 0)),
        out_shape=jax.ShapeDtypeStruct((M, D), out_dtype),
        compiler_params=_params(("parallel",)),
        name="norm_mod",
    )(x, g, sc, sh)


def _cmpz_body(pt_ref, *refs, P, flat):
    pages = refs[:P]
    wk_ref, wv_ref, zk_ref, zv_ref = refs[P:P + 4]
    nrow = 2 * NSA_GROUPS
    if flat:
        fl = [pages[k].reshape(PAGE * nrow, DH) for k in range(P)]

        def piece(k, p, kind, g):
            return fl[k][pl.ds(p * nrow + kind * NSA_GROUPS + g, CHUNKS_PER_PAGE, stride=CMP_STRIDE * nrow), :]
    else:
        ri = lax.broadcasted_iota(jnp.int32, (PAGE, PAGE), 0)
        ci = lax.broadcasted_iota(jnp.int32, (PAGE, PAGE), 1)
        perm = (ci == (ri % CHUNKS_PER_PAGE) * CMP_STRIDE + ri // CHUNKS_PER_PAGE).astype(BF16)
        xp = [_dot(perm, pages[k][...].astype(BF16)) for k in range(P)]

        def piece(k, p, kind, g):
            col = (kind * NSA_GROUPS + g) * DH
            return xp[k][p * CHUNKS_PER_PAGE:(p + 1) * CHUNKS_PER_PAGE, col:col + DH]
    for kind, (w_ref, z_ref) in enumerate(((wk_ref, zk_ref), (wv_ref, zv_ref))):
        rows = []
        for k in range(P):
            for g in range(NSA_GROUPS):
                rows.append(jnp.concatenate([piece(k, p, kind, g) for p in range(CMP_STRIDE)], axis=1))
        y = jnp.concatenate(rows, axis=0).astype(BF16)
        z = _dot(y, w_ref[...])
        z_ref[...] = z.reshape(1, P, NSA_GROUPS, CHUNKS_PER_PAGE, 2 * DH)


def cmpz(rows, page_tab, wk_ab, wv_ab):
    B, n_pages = page_tab.shape
    P = _pick(n_pages, 8, 1)
    flat = rows.ndim == 4
    if flat:
        def page_spec(k):
            return pl.BlockSpec((None, PAGE, 2 * NSA_GROUPS, DH),
                                lambda b, j, pt: (pt[b * n_pages + j * P + k], 0, 0, 0))
    else:
        def page_spec(k):
            return pl.BlockSpec((None, PAGE, 2 * NSA_GROUPS * DH), lambda b, j, pt: (pt[b * n_pages + j * P + k], 0, 0))

    grid_spec = pltpu.PrefetchScalarGridSpec(
        num_scalar_prefetch=1,
        grid=(B, n_pages // P),
        in_specs=[page_spec(k) for k in range(P)]
        + [pl.BlockSpec((CMP_STRIDE * DH, 2 * DH), lambda b, j, pt: (0, 0))] * 2,
        out_specs=[pl.BlockSpec((1, P, NSA_GROUPS, CHUNKS_PER_PAGE, 2 * DH), lambda b, j, pt: (b, j, 0, 0, 0))] * 2,
    )
    zshape = jax.ShapeDtypeStruct((B, n_pages, NSA_GROUPS, CHUNKS_PER_PAGE, 2 * DH), F32)
    return pl.pallas_call(
        functools.partial(_cmpz_body, P=P, flat=flat),
        grid_spec=grid_spec,
        out_shape=[zshape, zshape],
        compiler_params=_params(("parallel", "parallel")),
        name="nsa_cmpz",
    )(page_tab.reshape(-1), *([rows] * P), wk_ab, wv_ab)


def _cmpfin_body(z_ref, c_ref, w2_ref, o_ref, *, n_chunk):
    z = z_ref[0, :, 0].reshape(n_chunk, 2 * DH)
    nxt = pltpu.roll(z[:, DH:], n_chunk - 1, axis=0)
    pre = z[:, :DH] + nxt + c_ref[0:1, :]
    h = jax.nn.gelu(pre)
    o_ref[0, 0] = _dot(h.astype(BF16), w2_ref[...]).astype(o_ref.dtype)


def cmp_finish(z, c_row8, w2):
    B, n_pages = z.shape[0], z.shape[1]
    n_chunk = n_pages * CHUNKS_PER_PAGE
    return pl.pallas_call(
        functools.partial(_cmpfin_body, n_chunk=n_chunk),
        grid=(B, NSA_GROUPS),
        in_specs=[pl.BlockSpec((1, n_pages, 1, CHUNKS_PER_PAGE, 2 * DH), lambda b, g: (b, 0, g, 0, 0)),
                  pl.BlockSpec((8, DH), lambda b, g: (0, 0)),
                  pl.BlockSpec((DH, DH), lambda b, g: (0, 0))],
        out_specs=pl.BlockSpec((1, 1, n_chunk, DH), lambda b, g: (b, g, 0, 0)),
        out_shape=jax.ShapeDtypeStruct((B, NSA_GROUPS, n_chunk, DH), BF16),
        compiler_params=_params(("parallel", "parallel")),
        name="nsa_cmpfin",
    )(z, c_row8, w2)


def _nsacmp_body(sl_ref, q_ref, kc_ref, vc_ref, agg_ref, oc_ref, mask_ref, idx_ref, any_ref, *, TQ, NC, NS, q0, top):
    g = pl.program_id(1)
    i = pl.program_id(2)
    t = q0 + i * TQ + lax.broadcasted_iota(jnp.int32, (TQ, 1), 0)
    ce = lax.broadcasted_iota(jnp.int32, (1, NC), 1) * CMP_STRIDE + (CMP_BLOCK - 1)
    d = (t - ce).astype(F32)
    valid = d >= 0.0
    kc = kc_ref[0, 0]
    vc = vc_ref[0, 0]
    psum = jnp.zeros((TQ, NC), F32)
    for r in range(HPG):
        s = _dot_nt(q_ref[0, :, r * DH:(r + 1) * DH], kc) - sl_ref[g * HPG + r] * d
        m = jnp.max(jnp.where(valid, s, NEG), axis=1, keepdims=True)
        p = jnp.where(valid, jnp.exp(s - m), 0.0)
        p = p / jnp.maximum(jnp.sum(p, axis=1, keepdims=True), 1e-30)
        oc_ref[0, :, r * DH:(r + 1) * DH] = _dot(p.astype(BF16), vc)
        psum = psum + p
    ph = psum.astype(BF16)
    plo = (psum - ph.astype(F32)).astype(BF16)
    imp = _dot(ph, agg_ref[...]) + _dot(plo, agg_ref[...])
    cur = t // SEL_BLOCK
    j = lax.broadcasted_iota(jnp.int32, (1, NS), 1)
    forced = (j == 0) | (j == cur) | (j == cur - 1)
    imp = jnp.where(j <= cur, imp + jnp.where(forced, SEL_BONUS, 0.0), -SEL_BONUS)
    impT = imp.T
    io = lax.broadcasted_iota(jnp.int32, (NS, TQ), 0).astype(F32)
    selT = jnp.zeros((NS, TQ), F32)
    picks = []
    for _ in range(top):
        mx = jnp.max(impT, axis=0, keepdims=True)
        am = jnp.min(jnp.where(impT == mx, io, float(NS)), axis=0, keepdims=True)
        hit = io == am
        selT = jnp.where(hit, 1.0, selT)
        impT = jnp.where(hit, -3.0e38, impT)
        picks.append(am)
    sel = selT.T
    mask_ref[0, 0] = sel.astype(mask_ref.dtype)
    idx_ref[0, 0] = jnp.concatenate(picks, axis=0).astype(jnp.int32)
    any_ref[0, 0, 0] = jnp.broadcast_to(jnp.max(sel, axis=0, keepdims=True), (8, NS))


def nsa_cmp(q, kc, vc, agg, slopes, q0, TQ=128):
    B, Tq, _ = q.shape
    NC = kc.shape[2]
    NS = agg.shape[1]
    top = N_SELECT
    grid_spec = pltpu.PrefetchScalarGridSpec(
        num_scalar_prefetch=1,
        grid=(B, NSA_GROUPS, Tq // TQ),
        in_specs=[pl.BlockSpec((1, TQ, HPG * DH), lambda b, g, i, sl: (b, i, g)),
                  pl.BlockSpec((1, 1, NC, DH), lambda b, g, i, sl: (b, g, 0, 0)),
                  pl.BlockSpec((1, 1, NC, DH), lambda b, g, i, sl: (b, g, 0, 0)),
                  pl.BlockSpec((NC, NS), lambda b, g, i, sl: (0, 0))],
        out_specs=[pl.BlockSpec((1, TQ, HPG * DH), lambda b, g, i, sl: (b, i, g)),
                   pl.BlockSpec((1, 1, TQ, NS), lambda b, g, i, sl: (b, g, i, 0)),
                   pl.BlockSpec((1, 1, top, TQ), lambda b, g, i, sl: (b, g, 0, i)),
                   pl.BlockSpec((1, 1, 1, 8, NS), lambda b, g, i, sl: (b, g, i, 0, 0))],
    )
    return pl.pallas_call(
        functools.partial(_nsacmp_body, TQ=TQ, NC=NC, NS=NS, q0=q0, top=top),
        grid_spec=grid_spec,
        out_shape=[jax.ShapeDtypeStruct((B, Tq, NSA_HEADS * DH), F32),
                   jax.ShapeDtypeStruct((B, NSA_GROUPS, Tq, NS), BF16),
                   jax.ShapeDtypeStruct((B, NSA_GROUPS, top, Tq), jnp.int32),
                   jax.ShapeDtypeStruct((B, NSA_GROUPS, Tq // TQ, 8, NS), F32)],
        compiler_params=_params(("parallel", "parallel", "parallel")),
        name="nsa_cmp",
    )(slopes, q, kc, vc, agg)


def _nsawin_body(sl_ref, q_ref, k0_ref, k1_ref, k2_ref, v0_ref, v1_ref, v2_ref, o_ref, *, TQ):
    g = pl.program_id(0)
    qi = pl.program_id(1)
    t0 = qi * TQ
    tq = t0 + lax.broadcasted_iota(jnp.int32, (TQ, 1), 0)
    kp = t0 - 2 * TQ + lax.broadcasted_iota(jnp.int32, (1, 3 * TQ), 1)
    dist = tq - kp
    bias0 = jnp.where((dist >= 0) & (dist < WINDOW) & (kp >= 0), 0.0, NEG)
    kprel = (kp - t0).astype(F32)
    k = jnp.concatenate([k0_ref[...], k1_ref[...], k2_ref[...]], axis=0)
    v = jnp.concatenate([v0_ref[...], v1_ref[...], v2_ref[...]], axis=0)
    for r in range(HPG):
        s = _dot_nt(q_ref[:, r * DH:(r + 1) * DH], k) + (sl_ref[g * HPG + r] * kprel + bias0)
        p = jnp.exp(s - jnp.max(s, axis=1, keepdims=True))
        o = _dot(p.astype(BF16), v) / jnp.sum(p, axis=1, keepdims=True)
        o_ref[:, r * DH:(r + 1) * DH] = o.astype(o_ref.dtype)


def nsa_win(q, winb, slopes, TQ=256):
    T = q.shape[0]
    assert WINDOW <= 2 * TQ and T % TQ == 0
    G4 = NSA_GROUPS

    def kmap(off, kind):
        return lambda g, i, sl: (jnp.maximum(i - off, 0), kind * G4 + g)

    grid_spec = pltpu.PrefetchScalarGridSpec(
        num_scalar_prefetch=1,
        grid=(NSA_GROUPS, T // TQ),
        in_specs=[pl.BlockSpec((TQ, HPG * DH), lambda g, i, sl: (i, g))]
        + [pl.BlockSpec((TQ, DH), kmap(off, 0)) for off in (2, 1, 0)]
        + [pl.BlockSpec((TQ, DH), kmap(off, 1)) for off in (2, 1, 0)],
        out_specs=pl.BlockSpec((TQ, HPG * DH), lambda g, i, sl: (i, g)),
    )
    return pl.pallas_call(
        functools.partial(_nsawin_body, TQ=TQ),
        grid_spec=grid_spec,
        out_shape=jax.ShapeDtypeStruct((T, NSA_HEADS * DH), F32),
        compiler_params=_params(("parallel", "parallel")),
        name="nsa_win",
    )(slopes, q, *([winb] * 6))


def _nsasel_body(qi_t, ki_t, fl_t, sl_ref, q_ref, k_ref, v_ref, mask_ref, e_ref, gate_ref, oc_ref, ow_ref, o_ref,
                 ms, ls, accs, *, TQ, TK, n_steps):
    g = pl.program_id(0)
    st = g * n_steps + pl.program_id(1)
    qi = qi_t[st]
    ki = ki_t[st]
    fl = fl_t[st]
    t0 = qi * TQ
    k0 = ki * TK

    @pl.when((fl & 4) == 0)
    def _step():
        @pl.when((fl & 1) != 0)
        def _init():
            ms[...] = jnp.full(ms.shape, NEG, F32)
            ls[...] = jnp.zeros(ls.shape, F32)
            accs[...] = jnp.zeros(accs.shape, F32)

        tq = t0 + lax.broadcasted_iota(jnp.int32, (TQ, 1), 0)
        kp = k0 + lax.broadcasted_iota(jnp.int32, (1, TK), 1)
        kprel = (kp - t0).astype(F32)
        sel = _dot(mask_ref[0], e_ref[...])
        bias0 = jnp.where((sel > 0.5) & (tq >= kp), 0.0, NEG)
        k = k_ref[...]
        v = v_ref[...]
        reps = TK // 128
        for r in range(HPG):
            s = _dot_nt(q_ref[:, r * DH:(r + 1) * DH], k) + (sl_ref[g * HPG + r] * kprel + bias0)
            m_old = ms[r]
            m_new = jnp.maximum(m_old, jnp.max(s, axis=1, keepdims=True))
            alpha = jnp.exp(m_old - m_new)
            p = jnp.exp(s - jnp.tile(m_new, (1, reps)))
            ls[r] = alpha * ls[r] + jnp.sum(p, axis=1, keepdims=True)
            accs[r] = alpha * accs[r] + _dot(p.astype(BF16), v)
            ms[r] = m_new

        @pl.when((fl & 2) != 0)
        def _finish():
            gt = gate_ref[...]
            for r in range(HPG):
                cs = slice(r * DH, (r + 1) * DH)
                o = (gt[:, 3 * r:3 * r + 1] * oc_ref[:, cs] + gt[:, 3 * r + 1:3 * r + 2] * (accs[r] / ls[r])
                     + gt[:, 3 * r + 2:3 * r + 3] * ow_ref[:, cs])
                o_ref[:, cs] = o.astype(o_ref.dtype)


def nsa_sel(q, kvb, mask, anyq, expand, gates, oc, ow, slopes, TQ=256, TK=512):
    T = q.shape[0]
    NS = mask.shape[2]
    G4 = NSA_GROUPS
    nq, nk = T // TQ, T // TK
    assert T % TK == 0 and T % TQ == 0 and NS == nk * (TK // SEL_BLOCK)
    last = (np.arange(nq) * TQ + TQ - 1) // TK
    causal = np.arange(nk)[None, :] <= last[:, None]
    n_steps = int(causal.sum())
    act = (anyq.reshape(G4, nq, -1, nk, TK // SEL_BLOCK) > 0.5).any(axis=(2, 4)) & jnp.asarray(causal)[None]
    flat = act.reshape(G4, nq * nk)
    cnt = flat.sum(axis=-1).astype(jnp.int32)
    order = jnp.argsort(jnp.logical_not(flat), axis=-1, stable=True)[:, :n_steps].astype(jnp.int32)
    skip = jnp.arange(n_steps, dtype=jnp.int32)[None, :] >= cnt[:, None]
    idx = jnp.where(skip, jnp.take_along_axis(order, (cnt - 1)[:, None], axis=1), order)
    qi_t, ki_t = idx // nk, idx % nk
    fl_t = ((ki_t == 0).astype(jnp.int32) | ((ki_t == jnp.asarray(last, jnp.int32)[qi_t]).astype(jnp.int32) << 1)
            | (skip.astype(jnp.int32) << 2))
    tabs = [x.reshape(-1).astype(jnp.int32) for x in (qi_t, ki_t, fl_t)]
    qblk = pl.BlockSpec((TQ, HPG * DH), lambda g, s, qi, ki, fl, sl: (qi[g * n_steps + s], g))
    grid_spec = pltpu.PrefetchScalarGridSpec(
        num_scalar_prefetch=4,
        grid=(NSA_GROUPS, n_steps),
        in_specs=[
            qblk,
            pl.BlockSpec((TK, DH), lambda g, s, qi, ki, fl, sl: (ki[g * n_steps + s], 2 * G4 + g)),
            pl.BlockSpec((TK, DH), lambda g, s, qi, ki, fl, sl: (ki[g * n_steps + s], 3 * G4 + g)),
            pl.BlockSpec((1, TQ, NS), lambda g, s, qi, ki, fl, sl: (g, qi[g * n_steps + s], 0)),
            pl.BlockSpec((NS, TK), lambda g, s, qi, ki, fl, sl: (0, ki[g * n_steps + s])),
            pl.BlockSpec((TQ, 128), lambda g, s, qi, ki, fl, sl: (qi[g * n_steps + s], g)),
            qblk,
            qblk,
        ],
        out_specs=qblk,
        scratch_shapes=[pltpu.VMEM((HPG, TQ, 128), F32), pltpu.VMEM((HPG, TQ, 128), F32),
                        pltpu.VMEM((HPG, TQ, DH), F32)],
    )
    return pl.pallas_call(
        functools.partial(_nsasel_body, TQ=TQ, TK=TK, n_steps=n_steps),
        grid_spec=grid_spec,
        out_shape=jax.ShapeDtypeStruct((T, NSA_HEADS * DH), BF16),
        compiler_params=_params(("parallel", "arbitrary")),
        name="nsa_sel",
    )(*tabs, slopes, q, kvb, kvb, mask, expand, gates, oc, ow)


def _rowsel(rowg, vals):
    out = vals[0]
    for g in range(1, NSA_GROUPS):
        out = jnp.where(rowg == g, vals[g], out)
    return out


def _nsasels_body(kpos_t, val_t, blk_t, q_ref, *refs, n_slot, Wc, past):
    kv_refs = refs[0:4]
    (win_ref, knew_ref, vnew_ref, kwnew_ref, vwnew_ref, sl_ref, gc_ref, gs_ref, gw_ref, oc_ref,
     o_ref, m_s, l_s, acc_s) = refs[4:]
    G4 = NSA_GROUPS
    b = pl.program_id(0)
    s = pl.program_id(1)
    q = q_ref[0]
    rowg = lax.broadcasted_iota(jnp.int32, (NSA_HEADS, 1), 0) // HPG
    slope = sl_ref[...][:, 0:1]

    @pl.when(s == 0)
    def _init():
        m_s[...] = jnp.full(m_s.shape, NEG, F32)
        l_s[...] = jnp.zeros(l_s.shape, F32)
        acc_s[...] = jnp.zeros(acc_s.shape, F32)

    base = (b * NSA_GROUPS) * n_slot + s
    sc = _rowsel(rowg, [_dot_nt(q, kv_refs[g][:, g, :].astype(BF16)) for g in range(G4)])
    kp0 = _rowsel(rowg, [kpos_t[base + g * n_slot] for g in range(NSA_GROUPS)])
    ok = _rowsel(rowg, [val_t[base + g * n_slot] for g in range(NSA_GROUPS)]) > 0
    dist = (past - kp0 - lax.broadcasted_iota(jnp.int32, (1, SEL_BLOCK), 1)).astype(F32)
    sc = jnp.where(ok, sc - slope * dist, NEG)
    m_old = m_s[...]
    m_new = jnp.maximum(m_old, jnp.max(sc, axis=1, keepdims=True))
    alpha = jnp.exp(m_old - m_new)
    p = jnp.where(ok, jnp.exp(sc - m_new), 0.0)
    pb = p.astype(BF16)
    pv = _rowsel(rowg, [_dot(pb, kv_refs[g][:, G4 + g, :].astype(BF16)) for g in range(G4)])
    l_s[...] = alpha * l_s[...] + jnp.sum(p, axis=1, keepdims=True)
    acc_s[...] = alpha * acc_s[...] + pv
    m_s[...] = m_new

    @pl.when(s == n_slot - 1)
    def _finish():
        qf = q.astype(F32)
        sn = jnp.sum(qf * knew_ref[0], axis=1, keepdims=True)
        m1 = m_s[...]
        m2 = jnp.maximum(m1, sn)
        a2 = jnp.exp(m1 - m2)
        pn = jnp.exp(sn - m2)
        o_s = (a2 * acc_s[...] + pn * vnew_ref[0]) / (a2 * l_s[...] + pn)
        sw = _rowsel(rowg, [_dot_nt(q, win_ref[:, g, :].astype(BF16)) for g in range(G4)])
        dw = (Wc - lax.broadcasted_iota(jnp.int32, (1, Wc), 1)).astype(F32)
        okw = dw < float(WINDOW)
        sw = jnp.where(okw, sw - slope * dw, NEG)
        swn = jnp.sum(qf * kwnew_ref[0], axis=1, keepdims=True)
        mwin = jnp.maximum(jnp.max(sw, axis=1, keepdims=True), swn)
        pw = jnp.where(okw, jnp.exp(sw - mwin), 0.0)
        pwn = jnp.exp(swn - mwin)
        pwb = pw.astype(BF16)
        ow = _rowsel(rowg, [_dot(pwb, win_ref[:, G4 + g, :].astype(BF16)) for g in range(G4)])
        o_w = (ow + pwn * vwnew_ref[0]) / (jnp.sum(pw, axis=1, keepdims=True) + pwn)
        o = gc_ref[0] * oc_ref[0] + gs_ref[0] * o_s + gw_ref[0] * o_w
        o_ref[0] = o.astype(o_ref.dtype)


def nsa_sel_sample(q16, half_pages, blk, val, kpos, win_rows, knew, vnew, kwnew, vwnew, slope16, gc, gs, gw, oc16, past):
    DB = q16.shape[0]
    n_slot = blk.shape[0] // (DB * NSA_GROUPS)
    Wc = win_rows.shape[1]
    G4 = NSA_GROUPS

    def kvmap(g):
        return lambda b, s, kp, vl, bk: (bk[(b * G4 + g) * n_slot + s], 0, 1, 0)

    head = lambda b, s, kp, vl, bk: (b, 0, 0)
    hspec = pl.BlockSpec((1, NSA_HEADS, DH), head)
    grid_spec = pltpu.PrefetchScalarGridSpec(
        num_scalar_prefetch=3,
        grid=(DB, n_slot),
        in_specs=[hspec]
        + [pl.BlockSpec((None, SEL_BLOCK, 2 * G4, DH), kvmap(g)) for g in range(G4)]
        + [pl.BlockSpec((None, Wc, 2 * G4, DH), lambda b, s, kp, vl, bk: (b, 0, 0, 0))]
        + [hspec] * 4
        + [pl.BlockSpec((NSA_HEADS, DH), lambda b, s, kp, vl, bk: (0, 0))]
        + [hspec] * 4,
        out_specs=hspec,
        scratch_shapes=[pltpu.VMEM((NSA_HEADS, 1), F32), pltpu.VMEM((NSA_HEADS, 1), F32), pltpu.VMEM((NSA_HEADS, DH), F32)],
    )
    return pl.pallas_call(
        functools.partial(_nsasels_body, n_slot=n_slot, Wc=Wc, past=past),
        grid_spec=grid_spec,
        out_shape=jax.ShapeDtypeStruct((DB, NSA_HEADS, DH), BF16),
        compiler_params=_params(("parallel", "arbitrary")),
        name="nsa_sel_sample",
    )(kpos, val, blk, q16, *([half_pages] * 4), win_rows, knew, vnew, kwnew, vwnew, slope16, gc, gs, gw, oc16)


def _hgrn_tile(z, qraw, v, og, lb, gn, st):
    TC = z.shape[0]
    C = HG_SUB
    nsub = TC // C
    q = _silu(qraw)
    logf = jnp.log(lb + (1.0 - lb) * _sigmoid(z))
    kk = (1.0 - lb) * _sigmoid(-z)
    tril = (lax.broadcasted_iota(jnp.int32, (TC, TC), 0) >= lax.broadcasted_iota(jnp.int32, (TC, TC), 1)).astype(BF16)
    a, b2, c2 = _split3(logf)
    G = _dot(tril, a) + _dot(tril, b2) + _dot(tril, c2)
    gl = G[TC - 1:TC, :]
    vb = v.astype(BF16)
    tok = lax.broadcasted_iota(jnp.int32, (TC, 1), 0)

    o = _dot_nt((q * jnp.exp(G)).astype(BF16), st.astype(BF16))
    kd = kk * jnp.exp(gl - G)
    upd = lax.dot_general(vb, kd.astype(BF16), (((0,), (0,)), ((), ())), preferred_element_type=F32)
    st_new = st * jnp.exp(gl) + upd

    offs = [jnp.zeros((C, HG_DV), F32)]
    for b in range(1, nsub):
        ref = G[b * C - 1:b * C, :]
        qb = q[b * C:(b + 1) * C, :] * jnp.exp(G[b * C:(b + 1) * C, :] - ref)
        kb = kk * jnp.exp(jnp.where(tok < b * C, ref - G, NEG))
        att = _dot_nt(qb.astype(BF16), kb.astype(BF16))
        offs.append(_dot(att.astype(BF16), vb))
    o = o + jnp.concatenate(offs, axis=0)

    def pick(x, s):
        return jnp.concatenate([jnp.broadcast_to(x[b * C + s:b * C + s + 1, :], (C, x.shape[1])) for b in range(nsub)], axis=0)

    sub = tok % C
    pieces = []
    for s in range(C):
        e = jnp.exp(jnp.where(sub >= s, G - pick(G, s), NEG))
        pieces.append((q * e * pick(kk, s)).astype(BF16))
    rs = _dot(jnp.concatenate(pieces, axis=0), jnp.ones((HG_DK, 128), BF16))
    for s in range(C):
        o = o + rs[s * TC:(s + 1) * TC, :] * pick(v, s)

    y = o * lax.rsqrt(jnp.mean(o * o, axis=-1, keepdims=True) + EPS) * gn * _silu(og)
    return y, st_new


def _hgrn_body(q_ref, f_ref, i_ref, og_ref, lb_ref, gn_ref, s0_ref, o_ref, sout_ref, st_ref, *, HB, nT):
    i = pl.program_id(1)

    @pl.when(i == 0)
    def _init():
        for h in range(HB):
            st_ref[h] = s0_ref[h].T

    for h in range(HB):
        cs = slice(h * HG_DK, (h + 1) * HG_DK)
        y, st_new = _hgrn_tile(f_ref[:, cs], q_ref[:, cs], i_ref[:, cs], og_ref[:, cs], lb_ref[:, cs], gn_ref[...],
                               st_ref[h])
        st_ref[h] = st_new
        o_ref[:, cs] = y.astype(o_ref.dtype)

    @pl.when(i == nT - 1)
    def _fin():
        for h in range(HB):
            sout_ref[h] = st_ref[h].T


def hgrn_prompt(hg, lb, gn, s0, TC=128, HB=2):
    T = hg.shape[0]
    H = HG_HEADS
    nT = T // TC
    nb = H // HB
    W = HB * HG_DK
    return pl.pallas_call(
        functools.partial(_hgrn_body, HB=HB, nT=nT),
        grid=(nb, nT),
        in_specs=[pl.BlockSpec((TC, W), lambda h, i: (i, h)),
                  pl.BlockSpec((TC, W), lambda h, i: (i, nb + h)),
                  pl.BlockSpec((TC, W), lambda h, i: (i, 2 * nb + h)),
                  pl.BlockSpec((TC, W), lambda h, i: (i, 3 * nb + h)),
                  pl.BlockSpec((1, W), lambda h, i: (0, h)),
                  pl.BlockSpec((1, HG_DV), lambda h, i: (0, 0)),
                  pl.BlockSpec((HB, HG_DK, HG_DV), lambda h, i: (h, 0, 0))],
        out_specs=[pl.BlockSpec((TC, W), lambda h, i: (i, h)),
                   pl.BlockSpec((HB, HG_DK, HG_DV), lambda h, i: (h, 0, 0))],
        out_shape=[jax.ShapeDtypeStruct((T, H * HG_DV), BF16), jax.ShapeDtypeStruct((H, HG_DK, HG_DV), F32)],
        scratch_shapes=[pltpu.VMEM((HB, HG_DV, HG_DK), F32)],
        compiler_params=_params(("parallel", "arbitrary")),
        name="hgrn_prompt",
    )(hg, hg, hg, hg, lb, gn, s0)


def _hgrns_body(qc_ref, zc_ref, v_ref, og_ref, lbc_ref, gn_ref, s_ref, o_ref, sout_ref):
    outs = []
    for h in range(HG_HEADS):
        z = zc_ref[0, h]
        lb = lbc_ref[h]
        f = lb + (1.0 - lb) * _sigmoid(z)
        kk = (1.0 - lb) * _sigmoid(-z)
        q = _silu(qc_ref[0, h])
        v = v_ref[0, h:h + 1, :]
        s_new = f * s_ref[0, h] + kk * v
        sout_ref[0, h] = s_new
        o = jnp.sum(q * s_new, axis=0, keepdims=True)
        og = og_ref[0, h:h + 1, :]
        outs.append(o * lax.rsqrt(jnp.mean(o * o, axis=-1, keepdims=True) + EPS) * gn_ref[...] * _silu(og))
    o_ref[0] = jnp.concatenate(outs, axis=0).astype(o_ref.dtype)


def hgrn_sample(qcol, zcol, v, og, lbcol, gn, s0):
    DB = v.shape[0]
    H = HG_HEADS
    col = pl.BlockSpec((1, H, HG_DK, 1), lambda b: (b, 0, 0, 0))
    rowb = pl.BlockSpec((1, H, HG_DV), lambda b: (b, 0, 0))
    st = pl.BlockSpec((1, H, HG_DK, HG_DV), lambda b: (b, 0, 0, 0))
    return pl.pallas_call(
        _hgrns_body,
        grid=(DB,),
        in_specs=[col, col, rowb, rowb, pl.BlockSpec((H, HG_DK, 1), lambda b: (0, 0, 0)),
                  pl.BlockSpec((1, HG_DV), lambda b: (0, 0)), st],
        out_specs=[rowb, st],
        out_shape=[jax.ShapeDtypeStruct((DB, H, HG_DV), BF16), jax.ShapeDtypeStruct((DB, H, HG_DK, HG_DV), F32)],
        compiler_params=_params(("parallel",)),
        name="hgrn_sample",
    )(qcol, zcol, v, og, lbcol, gn, s0)


def _ffn_body(u_ref, uh_ref, wa_ref, wv_ref, cp_ref, wd_ref, h_ref, gt_ref, gf_ref, y_ref, acc_ref, *, nf, tm):
    i = pl.program_id(0)
    f = pl.program_id(1)
    u = u_ref[...]
    a = _dot(u, wa_ref[...])
    v = _dot(u, wv_ref[...])
    ah = _dot(uh_ref[...], wa_ref[...]) * (i > 0).astype(F32)
    cp = cp_ref[...]
    rows = lax.broadcasted_iota(jnp.int32, (tm, 1), 0)
    a1 = jnp.where(rows == 0, ah[7:8, :], pltpu.roll(a, 1, axis=0))
    a2 = jnp.where(rows == 0, ah[6:7, :], jnp.where(rows == 1, ah[7:8, :], pltpu.roll(a, 2, axis=0)))
    conv = cp[3:4, :] + a2 * cp[0:1, :] + a1 * cp[1:2, :] + a * cp[2:3, :]
    hid = (_silu(conv) * v).astype(BF16)
    contrib = _dot(hid, wd_ref[...])

    @pl.when(f == 0)
    def _first():
        acc_ref[...] = contrib

    @pl.when(f > 0)
    def _rest():
        acc_ref[...] += contrib

    @pl.when(f == nf - 1)
    def _fin():
        h2 = h_ref[...] + gt_ref[...] * acc_ref[...]
        y_ref[...] = h2 * lax.rsqrt(jnp.mean(h2 * h2, axis=-1, keepdims=True) + EPS) * gf_ref[...]


def ffn_prompt(u2, h1, wa, wv, cp, wd, gt2, gf, tm=512, tf=512):
    M, D = u2.shape
    Fp = wa.shape[1]
    tm = _pick(M, tm, 8)
    tf = _pick(Fp, tf, 128)
    nf = Fp // tf
    hb = tm // 8
    return pl.pallas_call(
        functools.partial(_ffn_body, nf=nf, tm=tm),
        grid=(M // tm, nf),
        in_specs=[pl.BlockSpec((tm, D), lambda i, f: (i, 0)),
                  pl.BlockSpec((8, D), lambda i, f: (jnp.maximum(i * hb - 1, 0), 0)),
                  pl.BlockSpec((D, tf), lambda i, f: (0, f)),
                  pl.BlockSpec((D, tf), lambda i, f: (0, f)),
                  pl.BlockSpec((8, tf), lambda i, f: (0, f)),
                  pl.BlockSpec((tf, D), lambda i, f: (f, 0)),
                  pl.BlockSpec((tm, D), lambda i, f: (i, 0)),
                  pl.BlockSpec((1, D), lambda i, f: (0, 0)),
                  pl.BlockSpec((1, D), lambda i, f: (0, 0))],
        out_specs=pl.BlockSpec((tm, D), lambda i, f: (i, 0)),
        out_shape=jax.ShapeDtypeStruct((M, D), F32),
        scratch_shapes=[pltpu.VMEM((tm, D), F32)],
        compiler_params=_params(("parallel", "arbitrary")),
        name="ffn_prompt",
    )(u2, u2, wa, wv, cp, wd, h1, gt2, gf)


def _alibi_slopes():
    h = np.arange(1, NSA_HEADS + 1, dtype=np.float32)
    return np.asarray(2.0 ** (-8.0 * h / NSA_HEADS), dtype=np.float32)


def _agg_matrix(n_chunk, n_cmp, n_sel, ns_pad):
    i = np.arange(n_chunk)[:, None] * CMP_STRIDE
    j = np.arange(ns_pad)[None, :] * SEL_BLOCK
    m = (i <= j + SEL_BLOCK - 1) & (i + CMP_BLOCK - 1 >= j)
    m &= (np.arange(n_chunk)[:, None] < n_cmp) & (np.arange(ns_pad)[None, :] < n_sel)
    return jnp.asarray(m.astype(np.float32), BF16)


def _layer_weights(w_in, w_ck1, w_cv1, w_br_a, w_br_b, w_out, w_up, conv_w, conv_b, w_down, w_ada):
    D = w_in.shape[0]
    nq, nkv = NSA_HEADS * DH, NSA_GROUPS * DH
    o = 0
    w = {}
    w["q"] = w_in[:, o:o + nq]; o += nq
    w["kv4"] = w_in[:, o:o + 4 * nkv]; o += 4 * nkv
    w["win"] = w_in[:, o:o + 2 * nkv]; o += 2 * nkv
    gate = w_in[:, o:o + 3 * NSA_HEADS]; o += 3 * NSA_HEADS
    gate = jnp.pad(gate.reshape(D, NSA_GROUPS, 3 * HPG), ((0, 0), (0, 0), (0, 128 - 3 * HPG)))
    w["gate"] = gate.reshape(D, NSA_GROUPS * 128)
    nh = 4 * HG_HEADS * HG_DK
    w["hg"] = w_in[:, o:o + nh]; o += nh
    w["ma"] = w_in[:, o:o + D]; o += D
    w["mb"] = w_in[:, o:o + D]; o += D
    half = CMP_STRIDE * DH
    w["ck_ab"] = jnp.concatenate([w_ck1[:half], w_ck1[half:]], axis=1)
    w["cv_ab"] = jnp.concatenate([w_cv1[:half], w_cv1[half:]], axis=1)
    w["br_a"], w["br_b"], w["out"], w["ada"] = w_br_a, w_br_b, w_out, w_ada
    F = w_down.shape[0]
    Fp = -(-F // 512) * 512
    w["up_a"] = jnp.pad(w_up[:, :F], ((0, 0), (0, Fp - F)))
    w["up_v"] = jnp.pad(w_up[:, F:], ((0, 0), (0, Fp - F)))
    w["down"] = jnp.pad(w_down, ((0, Fp - F), (0, 0)))
    w = {k: v.astype(BF16) for k, v in w.items()}
    cp = jnp.concatenate([conv_w, conv_b[None, :], jnp.zeros((8 - CONV_W - 1, F), F32)], axis=0)
    w["conv"] = jnp.pad(cp, ((0, 0), (0, Fp - F)))
    return w


def _project(u, w):
    scale = DH ** -0.5
    p = {}
    p["q"] = mm(u, w["q"], epilogue=lambda a: a * scale, out_dtypes=(BF16,), name="proj_q")
    p["kv4"], p["kv4b"] = mm(u, w["kv4"], epilogue=lambda a: (a, a), out_dtypes=(F32, BF16), name="proj_kv")
    p["win"], p["winb"] = mm(u, w["win"], epilogue=lambda a: (a, a), out_dtypes=(F32, BF16), name="proj_win")
    p["gate"] = mm(u, w["gate"], epilogue=_sigmoid, name="proj_gate")
    p["hg"] = mm(u, w["hg"], name="proj_hg")
    p["ma"] = mm(u, w["ma"], epilogue=_sigmoid, out_dtypes=(BF16,), name="proj_ma")
    p["mb"] = mm(u, w["mb"], epilogue=_sigmoid, out_dtypes=(BF16,), name="proj_mb")
    return p


def _compressed_kv(rows3, page_tab, w, c_k, c_v, w_ck2, w_cv2):
    zk, zv = cmpz(rows3, page_tab, w["ck_ab"], w["cv_ab"])
    return cmp_finish(zk, c_k, w_ck2), cmp_finish(zv, c_v, w_cv2)


def _merge_out(oa, oh, p, w, x, gt1):
    gkind = "row" if gt1.shape[0] == 1 else "tile"
    a1 = mm(oa, w["br_a"], extras=[(p["ma"], "tile")], epilogue=lambda a, m: a * m.astype(F32),
            out_dtypes=(BF16,), name="branch_a")
    mix = mm(oh, w["br_b"], extras=[(p["mb"], "tile"), (a1, "tile")],
             epilogue=lambda a, m, prev: a * m.astype(F32) + prev.astype(F32), out_dtypes=(BF16,), name="branch_b")
    return mm(mix, w["out"], extras=[(x, "tile"), (gt1, gkind)], epilogue=lambda a, xr, g: xr + g * a, name="out_proj")


def kernel(x_prompt, x_sample, cache_kv, cache_win, state_hgrn, state_conv, page_table, c_prompt, c_sample, w_ada, b_ada, g_norm1, w_in, nsa_pos_k, nsa_pos_v, w_ck1, w_ck2, w_cv1, w_cv2, hg_lb_logits, hg_norm, w_br_a, w_br_b, w_out, g_norm2, w_up, conv_w, conv_b, w_down, g_final):
    B, T, D = x_prompt.shape
    DB = x_sample.shape[0]
    depth = w_in.shape[0]
    assert B == 1 and x_sample.shape[1] == 1 and depth == 1 and T % 512 == 0 and T >= N_SELECT * SEL_BLOCK
    n_pages = page_table.shape[1]
    past = n_pages * PAGE
    Wc = cache_win.shape[2]
    F = w_down.shape[1]
    slopes = jnp.asarray(_alibi_slopes())
    lower_bounds = jnp.cumsum(jax.nn.softmax(hg_lb_logits.astype(F32), axis=0), axis=0)
    layer = 0
    w = _layer_weights(w_in[layer], w_ck1[layer], w_cv1[layer], w_br_a[layer], w_br_b[layer], w_out[layer],
                       w_up[layer], conv_w[layer], conv_b[layer], w_down[layer], w_ada[layer])
    w_ck2b, w_cv2b = w_ck2[layer].astype(BF16), w_cv2[layer].astype(BF16)
    lb = lower_bounds[layer][None, :]
    gn = hg_norm[layer][None, :]
    g1, g2, gf = g_norm1[layer][None, :], g_norm2[layer][None, :], g_final[None, :]

    n_c = -(-(B + DB) // 8) * 8
    c_all = jnp.pad(jnp.concatenate([c_prompt, c_sample], axis=0), ((0, n_c - B - DB), (0, 0)))
    mod = mm(c_all, w["ada"], extras=[(b_ada[layer][None, :], "row")], epilogue=lambda a, b: a + b, act=_silu, name="adaln")
    sh1, sc1, gt1, sh2, sc2, gt2 = [mod[:, k * D:(k + 1) * D] for k in range(6)]
    ps, ss = slice(0, 1), slice(B, B + DB)

    def pos_term(pos, w1):
        return mm(jnp.pad(pos.reshape(1, -1), ((0, 7), (0, 0))), w1.astype(BF16), name="cmp_pos")
    c_k, c_v = pos_term(nsa_pos_k[layer], w_ck1[layer]), pos_term(nsa_pos_v[layer], w_cv1[layer])

    xp = x_prompt[0]
    u = norm_mod(xp, g1, sc1[ps], sh1[ps])
    p = _project(u, w)
    ident = jnp.arange(T // PAGE, dtype=jnp.int32)[None, :]
    kc, vc = _compressed_kv(p["kv4"].reshape(T // PAGE, PAGE, 4 * NSA_GROUPS * DH), ident, w, c_k, c_v, w_ck2b, w_cv2b)
    n_chunk = T // CMP_STRIDE
    n_sel = T // SEL_BLOCK
    agg = _agg_matrix(n_chunk, n_chunk - 1, n_sel, n_sel)
    oc, mask, _, anyq = nsa_cmp(p["q"][None], kc, vc, agg, slopes, q0=0)
    expand = jnp.asarray(np.kron(np.eye(n_sel, dtype=np.float32), np.ones((1, SEL_BLOCK), np.float32)), BF16)
    ow = nsa_win(p["q"], p["winb"], slopes)
    oa = nsa_sel(p["q"], p["kv4b"], mask[0], anyq[0, :, :, 0, :], expand, p["gate"], oc[0], ow, slopes)
    s0p = jnp.zeros((HG_HEADS, HG_DK, HG_DV), F32)
    oh, s_new_p = hgrn_prompt(p["hg"], lb, gn, s0p)
    h1 = _merge_out(oa, oh, p, w, xp, gt1[ps])
    u2 = norm_mod(h1, g2, sc2[ps], sh2[ps])
    y_p = ffn_prompt(u2, h1, w["up_a"], w["up_v"], w["conv"], w["down"], gt2[ps], gf)
    a_tail = mm(u2[T - 8:], w["up_a"], name="ffn_tail")
    kv_new_p = p["kv4"].reshape(1, 1, T, 4, NSA_GROUPS, DH)
    wk = min(WINDOW, T)
    win_new_p = p["win"][T - wk:].reshape(1, 1, wk, 2, NSA_GROUPS, DH)
    conv_new_p = a_tail[8 - (CONV_W - 1):, :F].reshape(1, 1, CONV_W - 1, F)

    xs = x_sample[:, 0]
    us = norm_mod(xs, g1, sc1[ss], sh1[ss])
    q = _project(us, w)
    n_row = 4 * NSA_GROUPS
    kcs, vcs = _compressed_kv(cache_kv.reshape(-1, PAGE, n_row, DH), page_table, w, c_k, c_v, w_ck2b, w_cv2b)
    n_chunk_s = past // CMP_STRIDE
    n_sel_s = past // SEL_BLOCK + 1
    ns_pad = -(-n_sel_s // 128) * 128
    agg_s = _agg_matrix(n_chunk_s, n_chunk_s - 1, n_sel_s, ns_pad)
    TQS = 128
    q_pad = jnp.pad(q["q"][:, None, :], ((0, 0), (0, TQS - 1), (0, 0)))
    ocs, _, idx, _ = nsa_cmp(q_pad, kcs, vcs, agg_s, slopes, q0=past)
    idx = idx[:, :, :, 0]
    n_past_blk = past // SEL_BLOCK
    jc = jnp.minimum(idx, n_past_blk - 1)
    per_page = PAGE // SEL_BLOCK
    pg = jnp.take_along_axis(page_table[:, None, :], jc // per_page, axis=2)
    blk = (pg * per_page + jc % per_page).astype(jnp.int32).reshape(-1)
    val = (idx < n_past_blk).astype(jnp.int32).reshape(-1)
    kpos = (jc * SEL_BLOCK).astype(jnp.int32).reshape(-1)

    def per_head(rows):
        return jnp.repeat(rows.reshape(DB, NSA_GROUPS, DH), HPG, axis=1)
    nk = NSA_GROUPS * DH
    gate3 = q["gate"].reshape(DB, NSA_GROUPS, 128)[:, :, :3 * HPG].reshape(DB, NSA_HEADS, 3)
    gb = [jnp.broadcast_to(gate3[:, :, k:k + 1], (DB, NSA_HEADS, 128)) for k in range(3)]
    slope16 = jnp.broadcast_to(slopes[:, None], (NSA_HEADS, DH))
    oas = nsa_sel_sample(
        q["q"].reshape(DB, NSA_HEADS, DH), cache_kv.reshape(-1, SEL_BLOCK, n_row, DH), blk, val, kpos,
        cache_win.reshape(DB, Wc, 2 * NSA_GROUPS, DH),
        per_head(q["kv4"][:, 2 * nk:3 * nk]), per_head(q["kv4"][:, 3 * nk:4 * nk]),
        per_head(q["win"][:, :nk]), per_head(q["win"][:, nk:]),
        slope16, gb[0], gb[1], gb[2], ocs[:, 0].reshape(DB, NSA_HEADS, DH), past)
    hq, hz, hv, hog = [q["hg"][:, k * HG_HEADS * HG_DK:(k + 1) * HG_HEADS * HG_DK] for k in range(4)]
    ohs, s_new_s = hgrn_sample(hq.reshape(DB, HG_HEADS, HG_DK, 1), hz.reshape(DB, HG_HEADS, HG_DK, 1),
                               hv.reshape(DB, HG_HEADS, HG_DV), hog.reshape(DB, HG_HEADS, HG_DV),
                               lb.reshape(HG_HEADS, HG_DK, 1), gn, state_hgrn.reshape(state_hgrn.shape[1:]))
    h1s = _merge_out(oas.reshape(DB, NSA_HEADS * DH), ohs.reshape(DB, HG_HEADS * HG_DV), q, w, xs, gt1[ss])
    u2s = norm_mod(h1s, g2, sc2[ss], sh2[ss])
    Fp = w["up_a"].shape[1]
    a_s = mm(u2s, w["up_a"], name="ffn_s_a")
    conv_buf = state_conv.reshape(state_conv.shape[1:])
    buf = jnp.pad(conv_buf, ((0, 0), (0, 0), (0, Fp - F)))

    def conv_gate(v, a, b0, b1, cw0, cw1, cw2, cb):
        conv = cb + b0 * cw0 + b1 * cw1 + a * cw2
        return _silu(conv) * v
    hid = mm(u2s, w["up_v"], extras=[(a_s, "tile"), (buf[:, 0], "tile"), (buf[:, 1], "tile")]
             + [(w["conv"][k:k + 1], "row") for k in range(4)], epilogue=conv_gate, out_dtypes=(BF16,), name="ffn_s_v")

    def resid_norm(a, h, g, gfin):
        h2 = h + g * a
        return h2 * lax.rsqrt(jnp.mean(h2 * h2, axis=-1, keepdims=True) + EPS) * gfin
    y_s = mm(hid, w["down"], extras=[(h1s, "tile"), (gt2[ss], "tile"), (gf, "row")], epilogue=resid_norm, tn=D, name="ffn_s_down")

    kv_new_s = q["kv4"].reshape(1, DB, 1, 4, NSA_GROUPS, DH)
    win_new_s = jnp.concatenate([cache_win[:, :, 1:], q["win"].reshape(1, DB, 1, 2, NSA_GROUPS, DH)], axis=2)
    conv_new_s = jnp.stack([conv_buf[:, 1], a_s[:, :F]], axis=1)[None]
    return (y_p[None], y_s[:, None, :], kv_new_p, win_new_p, s_new_p[None, None], conv_new_p,
            kv_new_s, win_new_s, s_new_s[None], conv_new_s)
```

```python
import functools
import math

import numpy as np
import jax
import jax.numpy as jnp
from jax import lax
from jax.experimental import pallas as pl
from jax.experimental.pallas import tpu as pltpu

F32 = jnp.float32
BF16 = jnp.bfloat16

NSA_HEADS = 16
NSA_GROUPS = 4
HPG = NSA_HEADS // NSA_GROUPS
DH = 128
CMP_BLOCK = 32
CMP_STRIDE = 16
SEL_BLOCK = 64
N_SELECT = 16
WINDOW = 512
SEL_BONUS = 1.0e6
HG_HEADS = 8
HG_DK = 128
HG_DV = 128
CONV_W = 3
EPS = 1e-6
PAGE = 128

NEG = -1.0e30
LOG2E = math.log2(math.e)
V7X_VMEM_LIMIT = 56 * 1024 * 1024
CHUNKS_PER_PAGE = PAGE // CMP_STRIDE
HG_SUB = 16


def _params(sem):
    return pltpu.CompilerParams(dimension_semantics=sem, vmem_limit_bytes=V7X_VMEM_LIMIT)


def _pick(dim, target, mult):
    if dim <= target:
        return dim
    t = (target // mult) * mult
    while t >= mult:
        if dim % t == 0:
            return t
        t -= mult
    return dim


def _dot(a, b):
    return jnp.dot(a, b, preferred_element_type=F32)


def _dot_nt(a, b):
    return lax.dot_general(a, b, (((1,), (1,)), ((), ())), preferred_element_type=F32)


def _sigmoid(x):
    return 1.0 / (1.0 + jnp.exp(-x))


def _silu(x):
    return x * _sigmoid(x)


def _split3(x):
    a = x.astype(BF16)
    r = x - a.astype(F32)
    b = r.astype(BF16)
    c = (r - b.astype(F32)).astype(BF16)
    return a, b, c


def _mm_body(*refs, n_extra, act, epilogue):
    x_ref, w_ref = refs[0], refs[1]
    extras = refs[2:2 + n_extra]
    outs = refs[2 + n_extra:]
    x = x_ref[...]
    if act is not None:
        x = act(x.astype(F32))
    acc = _dot(x.astype(BF16), w_ref[...].astype(BF16))
    res = epilogue(acc, *[e[...] for e in extras])
    if not isinstance(res, tuple):
        res = (res,)
    for o, r in zip(outs, res):
        o[...] = r.astype(o.dtype)


def mm(x, w, extras=(), epilogue=lambda a: a, out_dtypes=(F32,), act=None, tm=1024, tn=512, name="mm"):
    M, K = x.shape
    N = w.shape[1]
    tm = _pick(M, tm, 8)
    tn = _pick(N, tn, 128)
    in_specs = [pl.BlockSpec((tm, K), lambda i, j: (i, 0)), pl.BlockSpec((K, tn), lambda i, j: (0, j))]
    args = [x, w]
    for arr, kind in extras:
        if kind == "tile":
            in_specs.append(pl.BlockSpec((tm, tn), lambda i, j: (i, j)))
        else:
            in_specs.append(pl.BlockSpec((1, tn), lambda i, j: (0, j)))
        args.append(arr)
    outs = pl.pallas_call(
        functools.partial(_mm_body, n_extra=len(extras), act=act, epilogue=epilogue),
        grid=(M // tm, N // tn),
        in_specs=in_specs,
        out_specs=[pl.BlockSpec((tm, tn), lambda i, j: (i, j)) for _ in out_dtypes],
        out_shape=[jax.ShapeDtypeStruct((M, N), d) for d in out_dtypes],
        compiler_params=_params(("parallel", "parallel")),
        name=name,
    )(*args)
    return outs if len(outs) > 1 else outs[0]


def _norm_body(x_ref, g_ref, sc_ref, sh_ref, o_ref):
    x = x_ref[...].astype(F32)
    y = x * lax.rsqrt(jnp.mean(x * x, axis=-1, keepdims=True) + EPS) * g_ref[...]
    o_ref[...] = (y * (1.0 + sc_ref[...]) + sh_ref[...]).astype(o_ref.dtype)


def norm_mod(x, g, sc, sh, out_dtype=BF16):
    M, D = x.shape
    tm = _pick(M, 256, 8)
    per_row = sc.shape[0] == M and M > 1
    mod_spec = pl.BlockSpec((tm, D), lambda i: (i, 0)) if per_row else pl.BlockSpec((1, D), lambda i: (0, 0))
    return pl.pallas_call(
        _norm_body,
        grid=(M // tm,),
        in_specs=[pl.BlockSpec((tm, D), lambda i: (i, 0)), pl.BlockSpec((1, D), lambda i: (0, 0)), mod_spec, mod_spec],
        out_specs=pl.BlockSpec((tm, D), lambda i: (i, 0)),
        out_shape=jax.ShapeDtypeStruct((M, D), out_dtype),
        compiler_params=_params(("parallel",)),
        name="norm_mod",
    )(x, g, sc, sh)


def _cmpz_body(pt_ref, *refs, P, flat):
    pages = refs[:P]
    wk_ref, wv_ref, zk_ref, zv_ref = refs[P:P + 4]
    nrow = 2 * NSA_GROUPS
    if flat:
        nc = CHUNKS_PER_PAGE
        ri = lax.broadcasted_iota(jnp.int32, (nc * nrow, nc * nrow), 0)
        ci = lax.broadcasted_iota(jnp.int32, (nc * nrow, nc * nrow), 1)
        swap = (ci == (ri % nc) * nrow + ri // nc).astype(BF16)
        rp = []
        for p in range(CMP_STRIDE):
            a = jnp.concatenate([jnp.concatenate([pages[k][CMP_STRIDE * c + p] for c in range(nc)], axis=0)
                                 for k in range(P)], axis=1)
            rp.append(_dot(swap, a.astype(BF16)))

        def piece(k, p, kind, g):
            j = kind * NSA_GROUPS + g
            return rp[p][j * nc:(j + 1) * nc, k * DH:(k + 1) * DH]
    else:
        ri = lax.broadcasted_iota(jnp.int32, (PAGE, PAGE), 0)
        ci = lax.broadcasted_iota(jnp.int32, (PAGE, PAGE), 1)
        perm = (ci == (ri % CHUNKS_PER_PAGE) * CMP_STRIDE + ri // CHUNKS_PER_PAGE).astype(BF16)
        xp = [_dot(perm, pages[k][...].astype(BF16)) for k in range(P)]

        def piece(k, p, kind, g):
            col = (kind * NSA_GROUPS + g) * DH
            return xp[k][p * CHUNKS_PER_PAGE:(p + 1) * CHUNKS_PER_PAGE, col:col + DH]
    for kind, (w_ref, z_ref) in enumerate(((wk_ref, zk_ref), (wv_ref, zv_ref))):
        rows = []
        for k in range(P):
            for g in range(NSA_GROUPS):
                rows.append(jnp.concatenate([piece(k, p, kind, g) for p in range(CMP_STRIDE)], axis=1))
        y = jnp.concatenate(rows, axis=0).astype(BF16)
        z = _dot(y, w_ref[...])
        z_ref[...] = z.reshape(1, P, NSA_GROUPS, CHUNKS_PER_PAGE, 2 * DH)


def cmpz(rows, page_tab, wk_ab, wv_ab):
    B, n_pages = page_tab.shape
    P = _pick(n_pages, 8, 1)
    flat = rows.ndim == 4
    if flat:
        def page_spec(k):
            return pl.BlockSpec((None, PAGE, 2 * NSA_GROUPS, DH),
                                lambda b, j, pt: (pt[b * n_pages + j * P + k], 0, 0, 0))
    else:
        def page_spec(k):
            return pl.BlockSpec((None, PAGE, 2 * NSA_GROUPS * DH), lambda b, j, pt: (pt[b * n_pages + j * P + k], 0, 0))

    grid_spec = pltpu.PrefetchScalarGridSpec(
        num_scalar_prefetch=1,
        grid=(B, n_pages // P),
        in_specs=[page_spec(k) for k in range(P)]
        + [pl.BlockSpec((CMP_STRIDE * DH, 2 * DH), lambda b, j, pt: (0, 0))] * 2,
        out_specs=[pl.BlockSpec((1, P, NSA_GROUPS, CHUNKS_PER_PAGE, 2 * DH), lambda b, j, pt: (b, j, 0, 0, 0))] * 2,
    )
    zshape = jax.ShapeDtypeStruct((B, n_pages, NSA_GROUPS, CHUNKS_PER_PAGE, 2 * DH), F32)
    return pl.pallas_call(
        functools.partial(_cmpz_body, P=P, flat=flat),
        grid_spec=grid_spec,
        out_shape=[zshape, zshape],
        compiler_params=_params(("parallel", "parallel")),
        name="nsa_cmpz",
    )(page_tab.reshape(-1), *([rows] * P), wk_ab, wv_ab)


def _cmpfin_body(z_ref, c_ref, w2_ref, o_ref, *, n_chunk):
    z = z_ref[0, :, 0].reshape(n_chunk, 2 * DH)
    nxt = pltpu.roll(z[:, DH:], n_chunk - 1, axis=0)
    pre = z[:, :DH] + nxt + c_ref[0:1, :]
    h = jax.nn.gelu(pre)
    o_ref[0, 0] = _dot(h.astype(BF16), w2_ref[...]).astype(o_ref.dtype)


def cmp_finish(z, c_row8, w2):
    B, n_pages = z.shape[0], z.shape[1]
    n_chunk = n_pages * CHUNKS_PER_PAGE
    return pl.pallas_call(
        functools.partial(_cmpfin_body, n_chunk=n_chunk),
        grid=(B, NSA_GROUPS),
        in_specs=[pl.BlockSpec((1, n_pages, 1, CHUNKS_PER_PAGE, 2 * DH), lambda b, g: (b, 0, g, 0, 0)),
                  pl.BlockSpec((8, DH), lambda b, g: (0, 0)),
                  pl.BlockSpec((DH, DH), lambda b, g: (0, 0))],
        out_specs=pl.BlockSpec((1, 1, n_chunk, DH), lambda b, g: (b, g, 0, 0)),
        out_shape=jax.ShapeDtypeStruct((B, NSA_GROUPS, n_chunk, DH), BF16),
        compiler_params=_params(("parallel", "parallel")),
        name="nsa_cmpfin",
    )(z, c_row8, w2)


def _nsacmp_body(sl_ref, q_ref, kc_ref, vc_ref, agg_ref, oc_ref, mask_ref, idx_ref, any_ref, imp_ref,
                 *, TQ, NC, NS, q0, top):
    g = pl.program_id(1)
    i = pl.program_id(2)
    t0 = q0 + i * TQ
    t = t0 + lax.broadcasted_iota(jnp.int32, (TQ, 1), 0)

    def scores(ncv):
        ce = lax.broadcasted_iota(jnp.int32, (1, ncv), 1) * CMP_STRIDE + (CMP_BLOCK - 1)
        d = (t - ce).astype(F32)
        valid = d >= 0.0
        kc = kc_ref[0, 0, :ncv, :]
        vc = vc_ref[0, 0, :ncv, :]
        psum = jnp.zeros((TQ, ncv), F32)
        for r in range(HPG):
            s = _dot_nt(q_ref[0, :, r * DH:(r + 1) * DH], kc) - sl_ref[g * HPG + r] * d
            m = jnp.max(jnp.where(valid, s, NEG), axis=1, keepdims=True)
            p = jnp.where(valid, jnp.exp2(s - m), 0.0)
            p = p / jnp.maximum(jnp.sum(p, axis=1, keepdims=True), 1e-30)
            oc_ref[0, :, r * DH:(r + 1) * DH] = _dot(p.astype(BF16), vc)
            psum = psum + p
        ph = psum.astype(BF16)
        plo = (psum - ph.astype(F32)).astype(BF16)
        imp_ref[...] = _dot(ph, agg_ref[:ncv, :]) + _dot(plo, agg_ref[:ncv, :])

    nbk = min(4, NC // 128) if NC % 128 == 0 else 1
    share = NC // nbk
    n_vis = jnp.maximum(t0 + TQ - CMP_BLOCK, 0) // CMP_STRIDE + 1
    bucket = jnp.minimum((n_vis + share - 1) // share, nbk)
    for bk in range(1, nbk + 1):
        pl.when(bucket == bk)(functools.partial(scores, bk * share))
    imp = imp_ref[...]
    cur = t // SEL_BLOCK
    j = lax.broadcasted_iota(jnp.int32, (1, NS), 1)
    forced = (j == 0) | (j == cur) | (j == cur - 1)
    imp = jnp.where(j <= cur, imp + jnp.where(forced, SEL_BONUS, 0.0), -SEL_BONUS)
    impT = imp.T
    io = lax.broadcasted_iota(jnp.int32, (NS, TQ), 0).astype(F32)
    selT = jnp.zeros((NS, TQ), F32)
    picks = []
    for _ in range(top):
        mx = jnp.max(impT, axis=0, keepdims=True)
        am = jnp.min(jnp.where(impT == mx, io, float(NS)), axis=0, keepdims=True)
        hit = io == am
        selT = jnp.where(hit, 1.0, selT)
        impT = jnp.where(hit, -3.0e38, impT)
        picks.append(am)
    sel = selT.T
    mask_ref[0, 0] = sel.astype(mask_ref.dtype)
    idx_ref[0, 0] = jnp.concatenate(picks, axis=0).astype(jnp.int32)
    any_ref[0, 0, 0] = jnp.broadcast_to(jnp.max(sel, axis=0, keepdims=True), (8, NS))


def nsa_cmp(q, kc, vc, agg, slopes, q0, TQ=128):
    B, Tq, _ = q.shape
    NC = kc.shape[2]
    NS = agg.shape[1]
    top = N_SELECT
    grid_spec = pltpu.PrefetchScalarGridSpec(
        num_scalar_prefetch=1,
        grid=(B, NSA_GROUPS, Tq // TQ),
        in_specs=[pl.BlockSpec((1, TQ, HPG * DH), lambda b, g, i, sl: (b, i, g)),
                  pl.BlockSpec((1, 1, NC, DH), lambda b, g, i, sl: (b, g, 0, 0)),
                  pl.BlockSpec((1, 1, NC, DH), lambda b, g, i, sl: (b, g, 0, 0)),
                  pl.BlockSpec((NC, NS), lambda b, g, i, sl: (0, 0))],
        out_specs=[pl.BlockSpec((1, TQ, HPG * DH), lambda b, g, i, sl: (b, i, g)),
                   pl.BlockSpec((1, 1, TQ, NS), lambda b, g, i, sl: (b, g, i, 0)),
                   pl.BlockSpec((1, 1, top, TQ), lambda b, g, i, sl: (b, g, 0, i)),
                   pl.BlockSpec((1, 1, 1, 8, NS), lambda b, g, i, sl: (b, g, i, 0, 0))],
        scratch_shapes=[pltpu.VMEM((TQ, NS), F32)],
    )
    return pl.pallas_call(
        functools.partial(_nsacmp_body, TQ=TQ, NC=NC, NS=NS, q0=q0, top=top),
        grid_spec=grid_spec,
        out_shape=[jax.ShapeDtypeStruct((B, Tq, NSA_HEADS * DH), F32),
                   jax.ShapeDtypeStruct((B, NSA_GROUPS, Tq, NS), BF16),
                   jax.ShapeDtypeStruct((B, NSA_GROUPS, top, Tq), jnp.int32),
                   jax.ShapeDtypeStruct((B, NSA_GROUPS, Tq // TQ, 8, NS), F32)],
        compiler_params=_params(("parallel", "parallel", "parallel")),
        name="nsa_cmp",
    )(slopes, q, kc, vc, agg)


def _nsawin_body(sl_ref, q_ref, k0_ref, k1_ref, k2_ref, v0_ref, v1_ref, v2_ref, o_ref, *, TQ):
    g = pl.program_id(0)
    qi = pl.program_id(1)
    t0 = qi * TQ
    tq = t0 + lax.broadcasted_iota(jnp.int32, (TQ, 1), 0)
    kp = t0 - 2 * TQ + lax.broadcasted_iota(jnp.int32, (1, 3 * TQ), 1)
    dist = tq - kp
    bias0 = jnp.where((dist >= 0) & (dist < WINDOW) & (kp >= 0), 0.0, NEG)
    kprel = (kp - t0).astype(F32)
    k = jnp.concatenate([k0_ref[...], k1_ref[...], k2_ref[...]], axis=0)
    v = jnp.concatenate([v0_ref[...], v1_ref[...], v2_ref[...]], axis=0)
    for r in range(HPG):
        s = _dot_nt(q_ref[:, r * DH:(r + 1) * DH], k) + (sl_ref[g * HPG + r] * kprel + bias0)
        p = jnp.exp2(s - jnp.max(s, axis=1, keepdims=True))
        o = _dot(p.astype(BF16), v) / jnp.sum(p, axis=1, keepdims=True)
        o_ref[:, r * DH:(r + 1) * DH] = o.astype(o_ref.dtype)


def nsa_win(q, winb, slopes, TQ=256):
    T = q.shape[0]
    assert WINDOW <= 2 * TQ and T % TQ == 0
    G4 = NSA_GROUPS

    def kmap(off, kind):
        return lambda g, i, sl: (jnp.maximum(i - off, 0), kind * G4 + g)

    grid_spec = pltpu.PrefetchScalarGridSpec(
        num_scalar_prefetch=1,
        grid=(NSA_GROUPS, T // TQ),
        in_specs=[pl.BlockSpec((TQ, HPG * DH), lambda g, i, sl: (i, g))]
        + [pl.BlockSpec((TQ, DH), kmap(off, 0)) for off in (2, 1, 0)]
        + [pl.BlockSpec((TQ, DH), kmap(off, 1)) for off in (2, 1, 0)],
        out_specs=pl.BlockSpec((TQ, HPG * DH), lambda g, i, sl: (i, g)),
    )
    return pl.pallas_call(
        functools.partial(_nsawin_body, TQ=TQ),
        grid_spec=grid_spec,
        out_shape=jax.ShapeDtypeStruct((T, NSA_HEADS * DH), F32),
        compiler_params=_params(("parallel", "parallel")),
        name="nsa_win",
    )(slopes, q, *([winb] * 6))


def _nsasel_body(qi_t, ki_t, fl_t, sl_ref, q_ref, k_ref, v_ref, mask_ref, e_ref, gate_ref, oc_ref, ow_ref, o_ref,
                 ms, ls, accs, *, TQ, TK, n_steps):
    g = pl.program_id(0)
    st = g * n_steps + pl.program_id(1)
    qi = qi_t[st]
    ki = ki_t[st]
    fl = fl_t[st]
    t0 = qi * TQ
    k0 = ki * TK

    @pl.when((fl & 4) == 0)
    def _step():
        @pl.when((fl & 1) != 0)
        def _init():
            ms[...] = jnp.full(ms.shape, NEG, F32)
            ls[...] = jnp.zeros(ls.shape, F32)
            accs[...] = jnp.zeros(accs.shape, F32)

        tq = t0 + lax.broadcasted_iota(jnp.int32, (TQ, 1), 0)
        kp = k0 + lax.broadcasted_iota(jnp.int32, (1, TK), 1)
        kprel = (kp - t0).astype(F32)
        sel = _dot(mask_ref[0], e_ref[...])
        bias0 = jnp.where((sel > 0.5) & (tq >= kp), 0.0, NEG)
        k = k_ref[...]
        v = v_ref[...]
        reps = TK // 128
        for r in range(HPG):
            s = _dot_nt(q_ref[:, r * DH:(r + 1) * DH], k) + (sl_ref[g * HPG + r] * kprel + bias0)
            m_old = ms[r]
            m_new = jnp.maximum(m_old, jnp.max(s, axis=1, keepdims=True))
            alpha = jnp.exp2(m_old - m_new)
            p = jnp.exp2(s - jnp.tile(m_new, (1, reps)))
            ls[r] = alpha * ls[r] + jnp.sum(p, axis=1, keepdims=True)
            accs[r] = alpha * accs[r] + _dot(p.astype(BF16), v)
            ms[r] = m_new

        @pl.when((fl & 2) != 0)
        def _finish():
            gt = gate_ref[...]
            for r in range(HPG):
                cs = slice(r * DH, (r + 1) * DH)
                o = (gt[:, 3 * r:3 * r + 1] * oc_ref[:, cs] + gt[:, 3 * r + 1:3 * r + 2] * (accs[r] / ls[r])
                     + gt[:, 3 * r + 2:3 * r + 3] * ow_ref[:, cs])
                o_ref[:, cs] = o.astype(o_ref.dtype)


def nsa_sel(q, kvb, mask, anyq, expand, gates, oc, ow, slopes, TQ=256, TK=512):
    T = q.shape[0]
    NS = mask.shape[2]
    G4 = NSA_GROUPS
    nq, nk = T // TQ, T // TK
    assert T % TK == 0 and T % TQ == 0 and NS == nk * (TK // SEL_BLOCK)
    last = (np.arange(nq) * TQ + TQ - 1) // TK
    causal = np.arange(nk)[None, :] <= last[:, None]
    n_steps = int(causal.sum())
    act = (anyq.reshape(G4, nq, -1, nk, TK // SEL_BLOCK) > 0.5).any(axis=(2, 4)) & jnp.asarray(causal)[None]
    flat = act.reshape(G4, nq * nk)
    cnt = flat.sum(axis=-1).astype(jnp.int32)
    order = jnp.argsort(jnp.logical_not(flat), axis=-1, stable=True)[:, :n_steps].astype(jnp.int32)
    skip = jnp.arange(n_steps, dtype=jnp.int32)[None, :] >= cnt[:, None]
    idx = jnp.where(skip, jnp.take_along_axis(order, (cnt - 1)[:, None], axis=1), order)
    qi_t, ki_t = idx // nk, idx % nk
    fl_t = ((ki_t == 0).astype(jnp.int32) | ((ki_t == jnp.asarray(last, jnp.int32)[qi_t]).astype(jnp.int32) << 1)
            | (skip.astype(jnp.int32) << 2))
    tabs = [x.reshape(-1).astype(jnp.int32) for x in (qi_t, ki_t, fl_t)]
    qblk = pl.BlockSpec((TQ, HPG * DH), lambda g, s, qi, ki, fl, sl: (qi[g * n_steps + s], g))
    grid_spec = pltpu.PrefetchScalarGridSpec(
        num_scalar_prefetch=4,
        grid=(NSA_GROUPS, n_steps),
        in_specs=[
            qblk,
            pl.BlockSpec((TK, DH), lambda g, s, qi, ki, fl, sl: (ki[g * n_steps + s], 2 * G4 + g)),
            pl.BlockSpec((TK, DH), lambda g, s, qi, ki, fl, sl: (ki[g * n_steps + s], 3 * G4 + g)),
            pl.BlockSpec((1, TQ, NS), lambda g, s, qi, ki, fl, sl: (g, qi[g * n_steps + s], 0)),
            pl.BlockSpec((NS, TK), lambda g, s, qi, ki, fl, sl: (0, ki[g * n_steps + s])),
            pl.BlockSpec((TQ, 128), lambda g, s, qi, ki, fl, sl: (qi[g * n_steps + s], g)),
            qblk,
            qblk,
        ],
        out_specs=qblk,
        scratch_shapes=[pltpu.VMEM((HPG, TQ, 128), F32), pltpu.VMEM((HPG, TQ, 128), F32),
                        pltpu.VMEM((HPG, TQ, DH), F32)],
    )
    return pl.pallas_call(
        functools.partial(_nsasel_body, TQ=TQ, TK=TK, n_steps=n_steps),
        grid_spec=grid_spec,
        out_shape=jax.ShapeDtypeStruct((T, NSA_HEADS * DH), BF16),
        compiler_params=_params(("parallel", "arbitrary")),
        name="nsa_sel",
    )(*tabs, slopes, q, kvb, kvb, mask, expand, gates, oc, ow)


def _rowsel(rowg, vals):
    out = vals[0]
    for g in range(1, NSA_GROUPS):
        out = jnp.where(rowg == g, vals[g], out)
    return out


def _nsasels_body(kpos_t, val_t, blk_t, q_ref, *refs, n_slot, sps, Wc, past):
    G4 = NSA_GROUPS
    kv_refs = refs[0:sps * G4]
    (win_ref, knew_ref, vnew_ref, kwnew_ref, vwnew_ref, sl_ref, gc_ref, gs_ref, gw_ref, oc_ref,
     o_ref, m_s, l_s, acc_s) = refs[sps * G4:]
    b = pl.program_id(0)
    s = pl.program_id(1)
    q = q_ref[0]
    rowg = lax.broadcasted_iota(jnp.int32, (NSA_HEADS, 1), 0) // HPG
    slope = sl_ref[...][:, 0:1]

    @pl.when(s == 0)
    def _init():
        m_s[...] = jnp.full(m_s.shape, NEG, F32)
        l_s[...] = jnp.zeros(l_s.shape, F32)
        acc_s[...] = jnp.zeros(acc_s.shape, F32)

    scs, oks = [], []
    m_old = m_s[...]
    m_new = m_old
    for u in range(sps):
        base = (b * NSA_GROUPS) * n_slot + s * sps + u
        sc = _rowsel(rowg, [_dot_nt(q, kv_refs[u * G4 + g][:, g, :].astype(BF16)) for g in range(G4)])
        kp0 = _rowsel(rowg, [kpos_t[base + g * n_slot] for g in range(NSA_GROUPS)])
        ok = _rowsel(rowg, [val_t[base + g * n_slot] for g in range(NSA_GROUPS)]) > 0
        dist = (past - kp0 - lax.broadcasted_iota(jnp.int32, (1, SEL_BLOCK), 1)).astype(F32)
        sc = jnp.where(ok, sc - slope * dist, NEG)
        m_new = jnp.maximum(m_new, jnp.max(sc, axis=1, keepdims=True))
        scs.append(sc)
        oks.append(ok)
    alpha = jnp.exp2(m_old - m_new)
    l_new = alpha * l_s[...]
    acc_new = alpha * acc_s[...]
    for u in range(sps):
        p = jnp.where(oks[u], jnp.exp2(scs[u] - m_new), 0.0)
        pb = p.astype(BF16)
        l_new = l_new + jnp.sum(p, axis=1, keepdims=True)
        acc_new = acc_new + _rowsel(rowg, [_dot(pb, kv_refs[u * G4 + g][:, G4 + g, :].astype(BF16)) for g in range(G4)])
    l_s[...] = l_new
    acc_s[...] = acc_new
    m_s[...] = m_new

    @pl.when(s == n_slot // sps - 1)
    def _finish():
        qf = q.astype(F32)
        sn = jnp.sum(qf * knew_ref[0], axis=1, keepdims=True)
        m1 = m_s[...]
        m2 = jnp.maximum(m1, sn)
        a2 = jnp.exp2(m1 - m2)
        pn = jnp.exp2(sn - m2)
        o_s = (a2 * acc_s[...] + pn * vnew_ref[0]) / (a2 * l_s[...] + pn)
        sw = _rowsel(rowg, [_dot_nt(q, win_ref[:, g, :].astype(BF16)) for g in range(G4)])
        dw = (Wc - lax.broadcasted_iota(jnp.int32, (1, Wc), 1)).astype(F32)
        okw = dw < float(WINDOW)
        sw = jnp.where(okw, sw - slope * dw, NEG)
        swn = jnp.sum(qf * kwnew_ref[0], axis=1, keepdims=True)
        mwin = jnp.maximum(jnp.max(sw, axis=1, keepdims=True), swn)
        pw = jnp.where(okw, jnp.exp2(sw - mwin), 0.0)
        pwn = jnp.exp2(swn - mwin)
        pwb = pw.astype(BF16)
        ow = _rowsel(rowg, [_dot(pwb, win_ref[:, G4 + g, :].astype(BF16)) for g in range(G4)])
        o_w = (ow + pwn * vwnew_ref[0]) / (jnp.sum(pw, axis=1, keepdims=True) + pwn)
        o = gc_ref[0] * oc_ref[0] + gs_ref[0] * o_s + gw_ref[0] * o_w
        o_ref[0] = o.astype(o_ref.dtype)


def nsa_sel_sample(q16, half_pages, blk, val, kpos, win_rows, knew, vnew, kwnew, vwnew, slope16, gc, gs, gw, oc16, past):
    DB = q16.shape[0]
    n_slot = blk.shape[0] // (DB * NSA_GROUPS)
    sps = _pick(n_slot, 4, 1)
    Wc = win_rows.shape[1]
    G4 = NSA_GROUPS

    def kvmap(u, g):
        return lambda b, s, kp, vl, bk: (bk[(b * G4 + g) * n_slot + s * sps + u], 0, 1, 0)

    head = lambda b, s, kp, vl, bk: (b, 0, 0)
    hspec = pl.BlockSpec((1, NSA_HEADS, DH), head)
    grid_spec = pltpu.PrefetchScalarGridSpec(
        num_scalar_prefetch=3,
        grid=(DB, n_slot // sps),
        in_specs=[hspec]
        + [pl.BlockSpec((None, SEL_BLOCK, 2 * G4, DH), kvmap(u, g)) for u in range(sps) for g in range(G4)]
        + [pl.BlockSpec((None, Wc, 2 * G4, DH), lambda b, s, kp, vl, bk: (b, 0, 0, 0))]
        + [hspec] * 4
        + [pl.BlockSpec((NSA_HEADS, DH), lambda b, s, kp, vl, bk: (0, 0))]
        + [hspec] * 4,
        out_specs=hspec,
        scratch_shapes=[pltpu.VMEM((NSA_HEADS, 1), F32), pltpu.VMEM((NSA_HEADS, 1), F32), pltpu.VMEM((NSA_HEADS, DH), F32)],
    )
    return pl.pallas_call(
        functools.partial(_nsasels_body, n_slot=n_slot, sps=sps, Wc=Wc, past=past),
        grid_spec=grid_spec,
        out_shape=jax.ShapeDtypeStruct((DB, NSA_HEADS, DH), BF16),
        compiler_params=_params(("parallel", "arbitrary")),
        name="nsa_sel_sample",
    )(kpos, val, blk, q16, *([half_pages] * (sps * G4)), win_rows, knew, vnew, kwnew, vwnew, slope16, gc, gs, gw, oc16)


def _hgrn_tile(z, qraw, v, og, lb, gn, st):
    TC = z.shape[0]
    C = HG_SUB
    nsub = TC // C
    q = _silu(qraw)
    logf = jnp.log(lb + (1.0 - lb) * _sigmoid(z))
    kk = (1.0 - lb) * _sigmoid(-z)
    tril = (lax.broadcasted_iota(jnp.int32, (TC, TC), 0) >= lax.broadcasted_iota(jnp.int32, (TC, TC), 1)).astype(BF16)
    a, b2, c2 = _split3(logf)
    G = _dot(tril, a) + _dot(tril, b2) + _dot(tril, c2)
    gl = G[TC - 1:TC, :]
    vb = v.astype(BF16)
    tok = lax.broadcasted_iota(jnp.int32, (TC, 1), 0)

    o = _dot_nt((q * jnp.exp(G)).astype(BF16), st.astype(BF16))
    kd = kk * jnp.exp(gl - G)
    upd = lax.dot_general(vb, kd.astype(BF16), (((0,), (0,)), ((), ())), preferred_element_type=F32)
    st_new = st * jnp.exp(gl) + upd

    offs = [jnp.zeros((C, HG_DV), F32)]
    for b in range(1, nsub):
        ref = G[b * C - 1:b * C, :]
        qb = q[b * C:(b + 1) * C, :] * jnp.exp(G[b * C:(b + 1) * C, :] - ref)
        kb = kk * jnp.exp(jnp.where(tok < b * C, ref - G, NEG))
        att = _dot_nt(qb.astype(BF16), kb.astype(BF16))
        offs.append(_dot(att.astype(BF16), vb))
    o = o + jnp.concatenate(offs, axis=0)

    def pick(x, s):
        return jnp.concatenate([jnp.broadcast_to(x[b * C + s:b * C + s + 1, :], (C, x.shape[1])) for b in range(nsub)], axis=0)

    sub = tok % C
    pieces = []
    for s in range(C):
        e = jnp.exp(jnp.where(sub >= s, G - pick(G, s), NEG))
        pieces.append((q * e * pick(kk, s)).astype(BF16))
    rs = _dot(jnp.concatenate(pieces, axis=0), jnp.ones((HG_DK, 128), BF16))
    for s in range(C):
        o = o + rs[s * TC:(s + 1) * TC, :] * pick(v, s)

    y = o * lax.rsqrt(jnp.mean(o * o, axis=-1, keepdims=True) + EPS) * gn * _silu(og)
    return y, st_new


def _hgrn_body(q_ref, f_ref, i_ref, og_ref, lb_ref, gn_ref, s0_ref, o_ref, sout_ref, st_ref, *, HB, nT):
    i = pl.program_id(1)

    @pl.when(i == 0)
    def _init():
        for h in range(HB):
            st_ref[h] = s0_ref[h].T

    for h in range(HB):
        cs = slice(h * HG_DK, (h + 1) * HG_DK)
        y, st_new = _hgrn_tile(f_ref[:, cs], q_ref[:, cs], i_ref[:, cs], og_ref[:, cs], lb_ref[:, cs], gn_ref[...],
                               st_ref[h])
        st_ref[h] = st_new
        o_ref[:, cs] = y.astype(o_ref.dtype)

    @pl.when(i == nT - 1)
    def _fin():
        for h in range(HB):
            sout_ref[h] = st_ref[h].T


def hgrn_prompt(hg, lb, gn, s0, TC=128, HB=4):
    T = hg.shape[0]
    H = HG_HEADS
    nT = T // TC
    nb = H // HB
    W = HB * HG_DK
    return pl.pallas_call(
        functools.partial(_hgrn_body, HB=HB, nT=nT),
        grid=(nb, nT),
        in_specs=[pl.BlockSpec((TC, W), lambda h, i: (i, h)),
                  pl.BlockSpec((TC, W), lambda h, i: (i, nb + h)),
                  pl.BlockSpec((TC, W), lambda h, i: (i, 2 * nb + h)),
                  pl.BlockSpec((TC, W), lambda h, i: (i, 3 * nb + h)),
                  pl.BlockSpec((1, W), lambda h, i: (0, h)),
                  pl.BlockSpec((1, HG_DV), lambda h, i: (0, 0)),
                  pl.BlockSpec((HB, HG_DK, HG_DV), lambda h, i: (h, 0, 0))],
        out_specs=[pl.BlockSpec((TC, W), lambda h, i: (i, h)),
                   pl.BlockSpec((HB, HG_DK, HG_DV), lambda h, i: (h, 0, 0))],
        out_shape=[jax.ShapeDtypeStruct((T, H * HG_DV), BF16), jax.ShapeDtypeStruct((H, HG_DK, HG_DV), F32)],
        scratch_shapes=[pltpu.VMEM((HB, HG_DV, HG_DK), F32)],
        compiler_params=_params(("parallel", "arbitrary")),
        name="hgrn_prompt",
    )(hg, hg, hg, hg, lb, gn, s0)


def _hgrns_body(qc_ref, zc_ref, v_ref, og_ref, lbc_ref, gn_ref, s_ref, o_ref, sout_ref):
    outs = []
    for h in range(HG_HEADS):
        z = zc_ref[0, h]
        lb = lbc_ref[h]
        f = lb + (1.0 - lb) * _sigmoid(z)
        kk = (1.0 - lb) * _sigmoid(-z)
        q = _silu(qc_ref[0, h])
        v = v_ref[0, h:h + 1, :]
        s_new = f * s_ref[0, h] + kk * v
        sout_ref[0, h] = s_new
        o = jnp.sum(q * s_new, axis=0, keepdims=True)
        og = og_ref[0, h:h + 1, :]
        outs.append(o * lax.rsqrt(jnp.mean(o * o, axis=-1, keepdims=True) + EPS) * gn_ref[...] * _silu(og))
    o_ref[0] = jnp.concatenate(outs, axis=0).astype(o_ref.dtype)


def hgrn_sample(qcol, zcol, v, og, lbcol, gn, s0):
    DB = v.shape[0]
    H = HG_HEADS
    col = pl.BlockSpec((1, H, HG_DK, 1), lambda b: (b, 0, 0, 0))
    rowb = pl.BlockSpec((1, H, HG_DV), lambda b: (b, 0, 0))
    st = pl.BlockSpec((1, H, HG_DK, HG_DV), lambda b: (b, 0, 0, 0))
    return pl.pallas_call(
        _hgrns_body,
        grid=(DB,),
        in_specs=[col, col, rowb, rowb, pl.BlockSpec((H, HG_DK, 1), lambda b: (0, 0, 0)),
                  pl.BlockSpec((1, HG_DV), lambda b: (0, 0)), st],
        out_specs=[rowb, st],
        out_shape=[jax.ShapeDtypeStruct((DB, H, HG_DV), BF16), jax.ShapeDtypeStruct((DB, H, HG_DK, HG_DV), F32)],
        compiler_params=_params(("parallel",)),
        name="hgrn_sample",
    )(qcol, zcol, v, og, lbcol, gn, s0)


def _ffn_body(u_ref, uh_ref, wa_ref, wv_ref, cp_ref, wd_ref, h_ref, gt_ref, gf_ref, y_ref, acc_ref, *, nf, tm):
    i = pl.program_id(0)
    f = pl.program_id(1)
    u = u_ref[...]
    a = _dot(u, wa_ref[...])
    v = _dot(u, wv_ref[...])
    ah = _dot(uh_ref[...], wa_ref[...]) * (i > 0).astype(F32)
    cp = cp_ref[...]
    rows = lax.broadcasted_iota(jnp.int32, (tm, 1), 0)
    a1 = jnp.where(rows == 0, ah[7:8, :], pltpu.roll(a, 1, axis=0))
    a2 = jnp.where(rows == 0, ah[6:7, :], jnp.where(rows == 1, ah[7:8, :], pltpu.roll(a, 2, axis=0)))
    conv = cp[3:4, :] + a2 * cp[0:1, :] + a1 * cp[1:2, :] + a * cp[2:3, :]
    hid = (_silu(conv) * v).astype(BF16)
    contrib = _dot(hid, wd_ref[...])

    @pl.when(f == 0)
    def _first():
        acc_ref[...] = contrib

    @pl.when(f > 0)
    def _rest():
        acc_ref[...] += contrib

    @pl.when(f == nf - 1)
    def _fin():
        h2 = h_ref[...] + gt_ref[...] * acc_ref[...]
        y_ref[...] = h2 * lax.rsqrt(jnp.mean(h2 * h2, axis=-1, keepdims=True) + EPS) * gf_ref[...]


def ffn_prompt(u2, h1, wa, wv, cp, wd, gt2, gf, tm=512, tf=512):
    M, D = u2.shape
    Fp = wa.shape[1]
    tm = _pick(M, tm, 8)
    tf = _pick(Fp, tf, 128)
    nf = Fp // tf
    hb = tm // 8
    return pl.pallas_call(
        functools.partial(_ffn_body, nf=nf, tm=tm),
        grid=(M // tm, nf),
        in_specs=[pl.BlockSpec((tm, D), lambda i, f: (i, 0)),
                  pl.BlockSpec((8, D), lambda i, f: (jnp.maximum(i * hb - 1, 0), 0)),
                  pl.BlockSpec((D, tf), lambda i, f: (0, f)),
                  pl.BlockSpec((D, tf), lambda i, f: (0, f)),
                  pl.BlockSpec((8, tf), lambda i, f: (0, f)),
                  pl.BlockSpec((tf, D), lambda i, f: (f, 0)),
                  pl.BlockSpec((tm, D), lambda i, f: (i, 0)),
                  pl.BlockSpec((1, D), lambda i, f: (0, 0)),
                  pl.BlockSpec((1, D), lambda i, f: (0, 0))],
        out_specs=pl.BlockSpec((tm, D), lambda i, f: (i, 0)),
        out_shape=jax.ShapeDtypeStruct((M, D), F32),
        scratch_shapes=[pltpu.VMEM((tm, D), F32)],
        compiler_params=_params(("parallel", "arbitrary")),
        name="ffn_prompt",
    )(u2, u2, wa, wv, cp, wd, h1, gt2, gf)


def _alibi_slopes():
    h = np.arange(1, NSA_HEADS + 1, dtype=np.float32)
    return np.asarray(2.0 ** (-8.0 * h / NSA_HEADS), dtype=np.float32)


def _agg_matrix(n_chunk, n_cmp, n_sel, ns_pad):
    i = np.arange(n_chunk)[:, None] * CMP_STRIDE
    j = np.arange(ns_pad)[None, :] * SEL_BLOCK
    m = (i <= j + SEL_BLOCK - 1) & (i + CMP_BLOCK - 1 >= j)
    m &= (np.arange(n_chunk)[:, None] < n_cmp) & (np.arange(ns_pad)[None, :] < n_sel)
    return jnp.asarray(m.astype(np.float32), BF16)


def _layer_weights(w_in, w_ck1, w_cv1, w_br_a, w_br_b, w_out, w_up, conv_w, conv_b, w_down, w_ada):
    D = w_in.shape[0]
    nq, nkv = NSA_HEADS * DH, NSA_GROUPS * DH
    o = 0
    w = {}
    w["q"] = w_in[:, o:o + nq]; o += nq
    w["kv4"] = w_in[:, o:o + 4 * nkv]; o += 4 * nkv
    w["win"] = w_in[:, o:o + 2 * nkv]; o += 2 * nkv
    gate = w_in[:, o:o + 3 * NSA_HEADS]; o += 3 * NSA_HEADS
    gate = jnp.pad(gate.reshape(D, NSA_GROUPS, 3 * HPG), ((0, 0), (0, 0), (0, 128 - 3 * HPG)))
    w["gate"] = gate.reshape(D, NSA_GROUPS * 128)
    nh = 4 * HG_HEADS * HG_DK
    w["hg"] = w_in[:, o:o + nh]; o += nh
    w["ma"] = w_in[:, o:o + D]; o += D
    w["mb"] = w_in[:, o:o + D]; o += D
    half = CMP_STRIDE * DH
    w["ck_ab"] = jnp.concatenate([w_ck1[:half], w_ck1[half:]], axis=1)
    w["cv_ab"] = jnp.concatenate([w_cv1[:half], w_cv1[half:]], axis=1)
    w["br_a"], w["br_b"], w["out"], w["ada"] = w_br_a, w_br_b, w_out, w_ada
    F = w_down.shape[0]
    Fp = -(-F // 512) * 512
    w["up_a"] = jnp.pad(w_up[:, :F], ((0, 0), (0, Fp - F)))
    w["up_v"] = jnp.pad(w_up[:, F:], ((0, 0), (0, Fp - F)))
    w["down"] = jnp.pad(w_down, ((0, Fp - F), (0, 0)))
    w = {k: v.astype(BF16) for k, v in w.items()}
    cp = jnp.concatenate([conv_w, conv_b[None, :], jnp.zeros((8 - CONV_W - 1, F), F32)], axis=0)
    w["conv"] = jnp.pad(cp, ((0, 0), (0, Fp - F)))
    return w


def _project(u, w):
    scale = DH ** -0.5 * LOG2E
    p = {}
    p["q"] = mm(u, w["q"], epilogue=lambda a: a * scale, out_dtypes=(BF16,), name="proj_q")
    p["kv4"], p["kv4b"] = mm(u, w["kv4"], epilogue=lambda a: (a, a), out_dtypes=(F32, BF16), name="proj_kv")
    p["win"], p["winb"] = mm(u, w["win"], epilogue=lambda a: (a, a), out_dtypes=(F32, BF16), name="proj_win")
    p["gate"] = mm(u, w["gate"], epilogue=_sigmoid, name="proj_gate")
    p["hg"] = mm(u, w["hg"], name="proj_hg")
    p["ma"] = mm(u, w["ma"], epilogue=_sigmoid, out_dtypes=(BF16,), name="proj_ma")
    p["mb"] = mm(u, w["mb"], epilogue=_sigmoid, out_dtypes=(BF16,), name="proj_mb")
    return p


def _compressed_kv(rows3, page_tab, w, c_k, c_v, w_ck2, w_cv2):
    zk, zv = cmpz(rows3, page_tab, w["ck_ab"], w["cv_ab"])
    return cmp_finish(zk, c_k, w_ck2), cmp_finish(zv, c_v, w_cv2)


def _merge_out(oa, oh, p, w, x, gt1):
    gkind = "row" if gt1.shape[0] == 1 else "tile"
    a1 = mm(oa, w["br_a"], extras=[(p["ma"], "tile")], epilogue=lambda a, m: a * m.astype(F32),
            out_dtypes=(BF16,), name="branch_a")
    mix = mm(oh, w["br_b"], extras=[(p["mb"], "tile"), (a1, "tile")],
             epilogue=lambda a, m, prev: a * m.astype(F32) + prev.astype(F32), out_dtypes=(BF16,), name="branch_b")
    return mm(mix, w["out"], extras=[(x, "tile"), (gt1, gkind)], epilogue=lambda a, xr, g: xr + g * a, name="out_proj")


def kernel(x_prompt, x_sample, cache_kv, cache_win, state_hgrn, state_conv, page_table, c_prompt, c_sample, w_ada, b_ada, g_norm1, w_in, nsa_pos_k, nsa_pos_v, w_ck1, w_ck2, w_cv1, w_cv2, hg_lb_logits, hg_norm, w_br_a, w_br_b, w_out, g_norm2, w_up, conv_w, conv_b, w_down, g_final):
    B, T, D = x_prompt.shape
    DB = x_sample.shape[0]
    depth = w_in.shape[0]
    assert B == 1 and x_sample.shape[1] == 1 and depth == 1 and T % 512 == 0 and T >= N_SELECT * SEL_BLOCK
    n_pages = page_table.shape[1]
    past = n_pages * PAGE
    Wc = cache_win.shape[2]
    F = w_down.shape[1]
    slopes = jnp.asarray(_alibi_slopes() * np.float32(LOG2E))
    lower_bounds = jnp.cumsum(jax.nn.softmax(hg_lb_logits.astype(F32), axis=0), axis=0)
    layer = 0
    w = _layer_weights(w_in[layer], w_ck1[layer], w_cv1[layer], w_br_a[layer], w_br_b[layer], w_out[layer],
                       w_up[layer], conv_w[layer], conv_b[layer], w_down[layer], w_ada[layer])
    w_ck2b, w_cv2b = w_ck2[layer].astype(BF16), w_cv2[layer].astype(BF16)
    lb = lower_bounds[layer][None, :]
    gn = hg_norm[layer][None, :]
    g1, g2, gf = g_norm1[layer][None, :], g_norm2[layer][None, :], g_final[None, :]

    n_c = -(-(B + DB) // 8) * 8
    c_all = jnp.pad(jnp.concatenate([c_prompt, c_sample], axis=0), ((0, n_c - B - DB), (0, 0)))
    mod = mm(c_all, w["ada"], extras=[(b_ada[layer][None, :], "row")], epilogue=lambda a, b: a + b, act=_silu, name="adaln")
    sh1, sc1, gt1, sh2, sc2, gt2 = [mod[:, k * D:(k + 1) * D] for k in range(6)]
    ps, ss = slice(0, 1), slice(B, B + DB)

    def pos_term(pos, w1):
        return mm(jnp.pad(pos.reshape(1, -1), ((0, 7), (0, 0))), w1.astype(BF16), name="cmp_pos")
    c_k, c_v = pos_term(nsa_pos_k[layer], w_ck1[layer]), pos_term(nsa_pos_v[layer], w_cv1[layer])

    xp = x_prompt[0]
    u = norm_mod(xp, g1, sc1[ps], sh1[ps])
    p = _project(u, w)
    ident = jnp.arange(T // PAGE, dtype=jnp.int32)[None, :]
    kc, vc = _compressed_kv(p["kv4"].reshape(T // PAGE, PAGE, 4 * NSA_GROUPS * DH), ident, w, c_k, c_v, w_ck2b, w_cv2b)
    n_chunk = T // CMP_STRIDE
    n_sel = T // SEL_BLOCK
    agg = _agg_matrix(n_chunk, n_chunk - 1, n_sel, n_sel)
    oc, mask, _, anyq = nsa_cmp(p["q"][None], kc, vc, agg, slopes, q0=0)
    expand = jnp.asarray(np.kron(np.eye(n_sel, dtype=np.float32), np.ones((1, SEL_BLOCK), np.float32)), BF16)
    ow = nsa_win(p["q"], p["winb"], slopes)
    oa = nsa_sel(p["q"], p["kv4b"], mask[0], anyq[0, :, :, 0, :], expand, p["gate"], oc[0], ow, slopes)
    s0p = jnp.zeros((HG_HEADS, HG_DK, HG_DV), F32)
    oh, s_new_p = hgrn_prompt(p["hg"], lb, gn, s0p)
    h1 = _merge_out(oa, oh, p, w, xp, gt1[ps])
    u2 = norm_mod(h1, g2, sc2[ps], sh2[ps])
    y_p = ffn_prompt(u2, h1, w["up_a"], w["up_v"], w["conv"], w["down"], gt2[ps], gf)
    a_tail = mm(u2[T - 8:], w["up_a"], name="ffn_tail")
    kv_new_p = p["kv4"].reshape(1, 1, T, 4, NSA_GROUPS, DH)
    wk = min(WINDOW, T)
    win_new_p = p["win"][T - wk:].reshape(1, 1, wk, 2, NSA_GROUPS, DH)
    conv_new_p = a_tail[8 - (CONV_W - 1):, :F].reshape(1, 1, CONV_W - 1, F)

    xs = x_sample[:, 0]
    us = norm_mod(xs, g1, sc1[ss], sh1[ss])
    q = _project(us, w)
    n_row = 4 * NSA_GROUPS
    kcs, vcs = _compressed_kv(cache_kv.reshape(-1, PAGE, n_row, DH), page_table, w, c_k, c_v, w_ck2b, w_cv2b)
    n_chunk_s = past // CMP_STRIDE
    n_sel_s = past // SEL_BLOCK + 1
    ns_pad = -(-n_sel_s // 128) * 128
    agg_s = _agg_matrix(n_chunk_s, n_chunk_s - 1, n_sel_s, ns_pad)
    TQS = 128
    q_pad = jnp.pad(q["q"][:, None, :], ((0, 0), (0, TQS - 1), (0, 0)))
    ocs, _, idx, _ = nsa_cmp(q_pad, kcs, vcs, agg_s, slopes, q0=past)
    idx = idx[:, :, :, 0]
    n_past_blk = past // SEL_BLOCK
    jc = jnp.minimum(idx, n_past_blk - 1)
    per_page = PAGE // SEL_BLOCK
    pg = jnp.take_along_axis(page_table[:, None, :], jc // per_page, axis=2)
    blk = (pg * per_page + jc % per_page).astype(jnp.int32).reshape(-1)
    val = (idx < n_past_blk).astype(jnp.int32).reshape(-1)
    kpos = (jc * SEL_BLOCK).astype(jnp.int32).reshape(-1)

    def per_head(rows):
        return jnp.repeat(rows.reshape(DB, NSA_GROUPS, DH), HPG, axis=1)
    nk = NSA_GROUPS * DH
    gate3 = q["gate"].reshape(DB, NSA_GROUPS, 128)[:, :, :3 * HPG].reshape(DB, NSA_HEADS, 3)
    gb = [jnp.broadcast_to(gate3[:, :, k:k + 1], (DB, NSA_HEADS, 128)) for k in range(3)]
    slope16 = jnp.broadcast_to(slopes[:, None], (NSA_HEADS, DH))
    oas = nsa_sel_sample(
        q["q"].reshape(DB, NSA_HEADS, DH), cache_kv.reshape(-1, SEL_BLOCK, n_row, DH), blk, val, kpos,
        cache_win.reshape(DB, Wc, 2 * NSA_GROUPS, DH),
        per_head(q["kv4"][:, 2 * nk:3 * nk]), per_head(q["kv4"][:, 3 * nk:4 * nk]),
        per_head(q["win"][:, :nk]), per_head(q["win"][:, nk:]),
        slope16, gb[0], gb[1], gb[2], ocs[:, 0].reshape(DB, NSA_HEADS, DH), past)
    hq, hz, hv, hog = [q["hg"][:, k * HG_HEADS * HG_DK:(k + 1) * HG_HEADS * HG_DK] for k in range(4)]
    ohs, s_new_s = hgrn_sample(hq.reshape(DB, HG_HEADS, HG_DK, 1), hz.reshape(DB, HG_HEADS, HG_DK, 1),
                               hv.reshape(DB, HG_HEADS, HG_DV), hog.reshape(DB, HG_HEADS, HG_DV),
                               lb.reshape(HG_HEADS, HG_DK, 1), gn, state_hgrn.reshape(state_hgrn.shape[1:]))
    h1s = _merge_out(oas.reshape(DB, NSA_HEADS * DH), ohs.reshape(DB, HG_HEADS * HG_DV), q, w, xs, gt1[ss])
    u2s = norm_mod(h1s, g2, sc2[ss], sh2[ss])
    Fp = w["up_a"].shape[1]
    a_s = mm(u2s, w["up_a"], name="ffn_s_a")
    conv_buf = state_conv.reshape(state_conv.shape[1:])
    buf = jnp.pad(conv_buf, ((0, 0), (0, 0), (0, Fp - F)))

    def conv_gate(v, a, b0, b1, cw0, cw1, cw2, cb):
        conv = cb + b0 * cw0 + b1 * cw1 + a * cw2
        return _silu(conv) * v
    hid = mm(u2s, w["up_v"], extras=[(a_s, "tile"), (buf[:, 0], "tile"), (buf[:, 1], "tile")]
             + [(w["conv"][k:k + 1], "row") for k in range(4)], epilogue=conv_gate, out_dtypes=(BF16,), name="ffn_s_v")

    def resid_norm(a, h, g, gfin):
        h2 = h + g * a
        return h2 * lax.rsqrt(jnp.mean(h2 * h2, axis=-1, keepdims=True) + EPS) * gfin
    y_s = mm(hid, w["down"], extras=[(h1s, "tile"), (gt2[ss], "tile"), (gf, "row")], epilogue=resid_norm, tn=D, name="ffn_s_down")

    kv_new_s = q["kv4"].reshape(1, DB, 1, 4, NSA_GROUPS, DH)
    win_new_s = jnp.concatenate([cache_win[:, :, 1:], q["win"].reshape(1, DB, 1, 2, NSA_GROUPS, DH)], axis=2)
    conv_new_s = jnp.stack([conv_buf[:, 1], a_s[:, :F]], axis=1)[None]
    return (y_p[None], y_s[:, None, :], kv_new_p, win_new_p, s_new_p[None, None], conv_new_p,
            kv_new_s, win_new_s, s_new_s[None], conv_new_s)
```

```python
import functools
import math

import numpy as np
import jax
import jax.numpy as jnp
from jax import lax
from jax.experimental import pallas as pl
from jax.experimental.pallas import tpu as pltpu

F32 = jnp.float32
BF16 = jnp.bfloat16

NSA_HEADS = 16
NSA_GROUPS = 4
HPG = NSA_HEADS // NSA_GROUPS
DH = 128
CMP_BLOCK = 32
CMP_STRIDE = 16
SEL_BLOCK = 64
N_SELECT = 16
WINDOW = 512
SEL_BONUS = 1.0e6
HG_HEADS = 8
HG_DK = 128
HG_DV = 128
CONV_W = 3
EPS = 1e-6
PAGE = 128

NEG = -1.0e30
LOG2E = math.log2(math.e)
V7X_VMEM_LIMIT = 56 * 1024 * 1024
CHUNKS_PER_PAGE = PAGE // CMP_STRIDE
HG_SUB = 16


def _params(sem):
    return pltpu.CompilerParams(dimension_semantics=sem, vmem_limit_bytes=V7X_VMEM_LIMIT)


def _pick(dim, target, mult):
    if dim <= target:
        return dim
    t = (target // mult) * mult
    while t >= mult:
        if dim % t == 0:
            return t
        t -= mult
    return dim


def _dot(a, b):
    return jnp.dot(a, b, preferred_element_type=F32)


def _dot_nt(a, b):
    return lax.dot_general(a, b, (((1,), (1,)), ((), ())), preferred_element_type=F32)


def _sigmoid(x):
    return 1.0 / (1.0 + jnp.exp(-x))


def _silu(x):
    return x * _sigmoid(x)


def _split3(x):
    a = x.astype(BF16)
    r = x - a.astype(F32)
    b = r.astype(BF16)
    c = (r - b.astype(F32)).astype(BF16)
    return a, b, c


def _mm_body(*refs, n_extra, act, epilogue):
    x_ref, w_ref = refs[0], refs[1]
    extras = refs[2:2 + n_extra]
    outs = refs[2 + n_extra:]
    x = x_ref[...]
    if act is not None:
        x = act(x.astype(F32))
    acc = _dot(x.astype(BF16), w_ref[...].astype(BF16))
    res = epilogue(acc, *[e[...] for e in extras])
    if not isinstance(res, tuple):
        res = (res,)
    for o, r in zip(outs, res):
        o[...] = r.astype(o.dtype)


def mm(x, w, extras=(), epilogue=lambda a: a, out_dtypes=(F32,), act=None, tm=1024, tn=512, name="mm"):
    M, K = x.shape
    N = w.shape[1]
    tm = _pick(M, tm, 8)
    tn = _pick(N, tn, 128)
    in_specs = [pl.BlockSpec((tm, K), lambda i, j: (i, 0)), pl.BlockSpec((K, tn), lambda i, j: (0, j))]
    args = [x, w]
    for arr, kind in extras:
        if kind == "tile":
            in_specs.append(pl.BlockSpec((tm, tn), lambda i, j: (i, j)))
        else:
            in_specs.append(pl.BlockSpec((1, tn), lambda i, j: (0, j)))
        args.append(arr)
    outs = pl.pallas_call(
        functools.partial(_mm_body, n_extra=len(extras), act=act, epilogue=epilogue),
        grid=(M // tm, N // tn),
        in_specs=in_specs,
        out_specs=[pl.BlockSpec((tm, tn), lambda i, j: (i, j)) for _ in out_dtypes],
        out_shape=[jax.ShapeDtypeStruct((M, N), d) for d in out_dtypes],
        compiler_params=_params(("parallel", "parallel")),
        name=name,
    )(*args)
    return outs if len(outs) > 1 else outs[0]


def _norm_body(x_ref, g_ref, sc_ref, sh_ref, o_ref):
    x = x_ref[...].astype(F32)
    y = x * lax.rsqrt(jnp.mean(x * x, axis=-1, keepdims=True) + EPS) * g_ref[...]
    o_ref[...] = (y * (1.0 + sc_ref[...]) + sh_ref[...]).astype(o_ref.dtype)


def norm_mod(x, g, sc, sh, out_dtype=BF16):
    M, D = x.shape
    tm = _pick(M, 256, 8)
    per_row = sc.shape[0] == M and M > 1
    mod_spec = pl.BlockSpec((tm, D), lambda i: (i, 0)) if per_row else pl.BlockSpec((1, D), lambda i: (0, 0))
    return pl.pallas_call(
        _norm_body,
        grid=(M // tm,),
        in_specs=[pl.BlockSpec((tm, D), lambda i: (i, 0)), pl.BlockSpec((1, D), lambda i: (0, 0)), mod_spec, mod_spec],
        out_specs=pl.BlockSpec((tm, D), lambda i: (i, 0)),
        out_shape=jax.ShapeDtypeStruct((M, D), out_dtype),
        compiler_params=_params(("parallel",)),
        name="norm_mod",
    )(x, g, sc, sh)


def _cmpz_body(pt_ref, *refs, P, flat):
    pages = refs[:P]
    wk_ref, wv_ref, zk_ref, zv_ref = refs[P:P + 4]
    nrow = 2 * NSA_GROUPS
    if flat:
        nc = CHUNKS_PER_PAGE
        ri = lax.broadcasted_iota(jnp.int32, (nc * nrow, nc * nrow), 0)
        ci = lax.broadcasted_iota(jnp.int32, (nc * nrow, nc * nrow), 1)
        swap = (ci == (ri % nc) * nrow + ri // nc).astype(BF16)
        rp = []
        for p in range(CMP_STRIDE):
            a = jnp.concatenate([jnp.concatenate([pages[k][CMP_STRIDE * c + p] for c in range(nc)], axis=0)
                                 for k in range(P)], axis=1)
            rp.append(_dot(swap, a.astype(BF16)))

        def piece(k, p, kind, g):
            j = kind * NSA_GROUPS + g
            return rp[p][j * nc:(j + 1) * nc, k * DH:(k + 1) * DH]
    else:
        ri = lax.broadcasted_iota(jnp.int32, (PAGE, PAGE), 0)
        ci = lax.broadcasted_iota(jnp.int32, (PAGE, PAGE), 1)
        perm = (ci == (ri % CHUNKS_PER_PAGE) * CMP_STRIDE + ri // CHUNKS_PER_PAGE).astype(BF16)
        xp = [_dot(perm, pages[k][...].astype(BF16)) for k in range(P)]

        def piece(k, p, kind, g):
            col = (kind * NSA_GROUPS + g) * DH
            return xp[k][p * CHUNKS_PER_PAGE:(p + 1) * CHUNKS_PER_PAGE, col:col + DH]
    for kind, (w_ref, z_ref) in enumerate(((wk_ref, zk_ref), (wv_ref, zv_ref))):
        rows = []
        for k in range(P):
            for g in range(NSA_GROUPS):
                rows.append(jnp.concatenate([piece(k, p, kind, g) for p in range(CMP_STRIDE)], axis=1))
        y = jnp.concatenate(rows, axis=0).astype(BF16)
        z = _dot(y, w_ref[...])
        z_ref[...] = z.reshape(1, P, NSA_GROUPS, CHUNKS_PER_PAGE, 2 * DH)


def cmpz(rows, page_tab, wk_ab, wv_ab):
    B, n_pages = page_tab.shape
    P = _pick(n_pages, 16, 1)
    flat = rows.ndim == 4
    if flat:
        def page_spec(k):
            return pl.BlockSpec((None, PAGE, 2 * NSA_GROUPS, DH),
                                lambda b, j, pt: (pt[b * n_pages + j * P + k], 0, 0, 0))
    else:
        def page_spec(k):
            return pl.BlockSpec((None, PAGE, 2 * NSA_GROUPS * DH), lambda b, j, pt: (pt[b * n_pages + j * P + k], 0, 0))

    grid_spec = pltpu.PrefetchScalarGridSpec(
        num_scalar_prefetch=1,
        grid=(B, n_pages // P),
        in_specs=[page_spec(k) for k in range(P)]
        + [pl.BlockSpec((CMP_STRIDE * DH, 2 * DH), lambda b, j, pt: (0, 0))] * 2,
        out_specs=[pl.BlockSpec((1, P, NSA_GROUPS, CHUNKS_PER_PAGE, 2 * DH), lambda b, j, pt: (b, j, 0, 0, 0))] * 2,
    )
    zshape = jax.ShapeDtypeStruct((B, n_pages, NSA_GROUPS, CHUNKS_PER_PAGE, 2 * DH), F32)
    return pl.pallas_call(
        functools.partial(_cmpz_body, P=P, flat=flat),
        grid_spec=grid_spec,
        out_shape=[zshape, zshape],
        compiler_params=_params(("parallel", "parallel")),
        name="nsa_cmpz",
    )(page_tab.reshape(-1), *([rows] * P), wk_ab, wv_ab)


def _cmpfin_body(z_ref, c_ref, w2_ref, o_ref, *, n_chunk):
    z = z_ref[0, :, 0].reshape(n_chunk, 2 * DH)
    nxt = pltpu.roll(z[:, DH:], n_chunk - 1, axis=0)
    pre = z[:, :DH] + nxt + c_ref[0:1, :]
    h = jax.nn.gelu(pre)
    o_ref[0, 0] = _dot(h.astype(BF16), w2_ref[...]).astype(o_ref.dtype)


def cmp_finish(z, c_row8, w2):
    B, n_pages = z.shape[0], z.shape[1]
    n_chunk = n_pages * CHUNKS_PER_PAGE
    return pl.pallas_call(
        functools.partial(_cmpfin_body, n_chunk=n_chunk),
        grid=(B, NSA_GROUPS),
        in_specs=[pl.BlockSpec((1, n_pages, 1, CHUNKS_PER_PAGE, 2 * DH), lambda b, g: (b, 0, g, 0, 0)),
                  pl.BlockSpec((8, DH), lambda b, g: (0, 0)),
                  pl.BlockSpec((DH, DH), lambda b, g: (0, 0))],
        out_specs=pl.BlockSpec((1, 1, n_chunk, DH), lambda b, g: (b, g, 0, 0)),
        out_shape=jax.ShapeDtypeStruct((B, NSA_GROUPS, n_chunk, DH), BF16),
        compiler_params=_params(("parallel", "parallel")),
        name="nsa_cmpfin",
    )(z, c_row8, w2)


def _nsacmp_body(sl_ref, q_ref, kc_ref, vc_ref, agg_ref, oc_ref, mask_ref, idx_ref, any_ref, imp_ref,
                 *, TQ, NC, NS, q0, top):
    g = pl.program_id(1)
    i = pl.program_id(2)
    t0 = q0 + i * TQ
    t = t0 + lax.broadcasted_iota(jnp.int32, (TQ, 1), 0)

    def scores(ncv):
        ce = lax.broadcasted_iota(jnp.int32, (1, ncv), 1) * CMP_STRIDE + (CMP_BLOCK - 1)
        d = (t - ce).astype(F32)
        valid = d >= 0.0
        kc = kc_ref[0, 0, :ncv, :]
        vc = vc_ref[0, 0, :ncv, :]
        psum = jnp.zeros((TQ, ncv), F32)
        for r in range(HPG):
            s = _dot_nt(q_ref[0, :, r * DH:(r + 1) * DH], kc) - sl_ref[g * HPG + r] * d
            m = jnp.max(jnp.where(valid, s, NEG), axis=1, keepdims=True)
            p = jnp.where(valid, jnp.exp2(s - m), 0.0)
            p = p / jnp.maximum(jnp.sum(p, axis=1, keepdims=True), 1e-30)
            oc_ref[0, :, r * DH:(r + 1) * DH] = _dot(p.astype(BF16), vc)
            psum = psum + p
        ph = psum.astype(BF16)
        plo = (psum - ph.astype(F32)).astype(BF16)
        imp_ref[...] = _dot(ph, agg_ref[:ncv, :]) + _dot(plo, agg_ref[:ncv, :])

    nbk = min(4, NC // 128) if NC % 128 == 0 else 1
    share = NC // nbk
    n_vis = jnp.maximum(t0 + TQ - CMP_BLOCK, 0) // CMP_STRIDE + 1
    bucket = jnp.minimum((n_vis + share - 1) // share, nbk)
    for bk in range(1, nbk + 1):
        pl.when(bucket == bk)(functools.partial(scores, bk * share))
    imp = imp_ref[...]
    cur = t // SEL_BLOCK
    j = lax.broadcasted_iota(jnp.int32, (1, NS), 1)
    forced = (j == 0) | (j == cur) | (j == cur - 1)
    imp = jnp.where(j <= cur, imp + jnp.where(forced, SEL_BONUS, 0.0), -SEL_BONUS)
    if TQ >= 128:
        impT = imp.T
        io = lax.broadcasted_iota(jnp.int32, (NS, TQ), 0).astype(F32)
        selT = jnp.zeros((NS, TQ), F32)
        picks = []
        for _ in range(top):
            mx = jnp.max(impT, axis=0, keepdims=True)
            am = jnp.min(jnp.where(impT == mx, io, float(NS)), axis=0, keepdims=True)
            hit = io == am
            selT = jnp.where(hit, 1.0, selT)
            impT = jnp.where(hit, -3.0e38, impT)
            picks.append(am)
        sel = selT.T
        idx_ref[0, 0] = jnp.concatenate(picks, axis=0).astype(jnp.int32)
    else:
        io = lax.broadcasted_iota(jnp.int32, (TQ, NS), 1).astype(F32)
        lane = lax.broadcasted_iota(jnp.int32, (TQ, 128), 1)
        sel = jnp.zeros((TQ, NS), F32)
        ids = jnp.zeros((TQ, 128), F32)
        for k in range(top):
            mx = jnp.max(imp, axis=1, keepdims=True)
            am = jnp.min(jnp.where(imp == mx, io, float(NS)), axis=1, keepdims=True)
            hit = io == am
            sel = jnp.where(hit, 1.0, sel)
            imp = jnp.where(hit, -3.0e38, imp)
            ids = jnp.where(lane == k, am, ids)
        idx_ref[0, 0] = ids.astype(jnp.int32)
    mask_ref[0, 0] = sel.astype(mask_ref.dtype)
    any_ref[0, 0, 0] = jnp.broadcast_to(jnp.max(sel, axis=0, keepdims=True), (8, NS))


def nsa_cmp(q, kc, vc, agg, slopes, q0, TQ=128):
    B, Tq, _ = q.shape
    NC = kc.shape[2]
    NS = agg.shape[1]
    top = N_SELECT
    if TQ >= 128:
        idx_spec = pl.BlockSpec((1, 1, top, TQ), lambda b, g, i, sl: (b, g, 0, i))
        idx_shape = jax.ShapeDtypeStruct((B, NSA_GROUPS, top, Tq), jnp.int32)
    else:
        idx_spec = pl.BlockSpec((1, 1, TQ, 128), lambda b, g, i, sl: (b, g, i, 0))
        idx_shape = jax.ShapeDtypeStruct((B, NSA_GROUPS, Tq, 128), jnp.int32)
    grid_spec = pltpu.PrefetchScalarGridSpec(
        num_scalar_prefetch=1,
        grid=(B, NSA_GROUPS, Tq // TQ),
        in_specs=[pl.BlockSpec((1, TQ, HPG * DH), lambda b, g, i, sl: (b, i, g)),
                  pl.BlockSpec((1, 1, NC, DH), lambda b, g, i, sl: (b, g, 0, 0)),
                  pl.BlockSpec((1, 1, NC, DH), lambda b, g, i, sl: (b, g, 0, 0)),
                  pl.BlockSpec((NC, NS), lambda b, g, i, sl: (0, 0))],
        out_specs=[pl.BlockSpec((1, TQ, HPG * DH), lambda b, g, i, sl: (b, i, g)),
                   pl.BlockSpec((1, 1, TQ, NS), lambda b, g, i, sl: (b, g, i, 0)),
                   idx_spec,
                   pl.BlockSpec((1, 1, 1, 8, NS), lambda b, g, i, sl: (b, g, i, 0, 0))],
        scratch_shapes=[pltpu.VMEM((TQ, NS), F32)],
    )
    return pl.pallas_call(
        functools.partial(_nsacmp_body, TQ=TQ, NC=NC, NS=NS, q0=q0, top=top),
        grid_spec=grid_spec,
        out_shape=[jax.ShapeDtypeStruct((B, Tq, NSA_HEADS * DH), F32),
                   jax.ShapeDtypeStruct((B, NSA_GROUPS, Tq, NS), BF16),
                   idx_shape,
                   jax.ShapeDtypeStruct((B, NSA_GROUPS, Tq // TQ, 8, NS), F32)],
        compiler_params=_params(("parallel", "parallel", "parallel")),
        name="nsa_cmp",
    )(slopes, q, kc, vc, agg)


def _nsawin_body(sl_ref, q_ref, k0_ref, k1_ref, k2_ref, v0_ref, v1_ref, v2_ref, o_ref, *, TQ):
    g = pl.program_id(0)
    qi = pl.program_id(1)
    t0 = qi * TQ
    tq = t0 + lax.broadcasted_iota(jnp.int32, (TQ, 1), 0)
    kp = t0 - 2 * TQ + lax.broadcasted_iota(jnp.int32, (1, 3 * TQ), 1)
    dist = tq - kp
    bias0 = jnp.where((dist >= 0) & (dist < WINDOW) & (kp >= 0), 0.0, NEG)
    kprel = (kp - t0).astype(F32)
    k = jnp.concatenate([k0_ref[...], k1_ref[...], k2_ref[...]], axis=0)
    v = jnp.concatenate([v0_ref[...], v1_ref[...], v2_ref[...]], axis=0)
    for r in range(HPG):
        s = _dot_nt(q_ref[:, r * DH:(r + 1) * DH], k) + (sl_ref[g * HPG + r] * kprel + bias0)
        p = jnp.exp2(s - jnp.max(s, axis=1, keepdims=True))
        o = _dot(p.astype(BF16), v) / jnp.sum(p, axis=1, keepdims=True)
        o_ref[:, r * DH:(r + 1) * DH] = o.astype(o_ref.dtype)


def nsa_win(q, winb, slopes, TQ=256):
    T = q.shape[0]
    assert WINDOW <= 2 * TQ and T % TQ == 0
    G4 = NSA_GROUPS

    def kmap(off, kind):
        return lambda g, i, sl: (jnp.maximum(i - off, 0), kind * G4 + g)

    grid_spec = pltpu.PrefetchScalarGridSpec(
        num_scalar_prefetch=1,
        grid=(NSA_GROUPS, T // TQ),
        in_specs=[pl.BlockSpec((TQ, HPG * DH), lambda g, i, sl: (i, g))]
        + [pl.BlockSpec((TQ, DH), kmap(off, 0)) for off in (2, 1, 0)]
        + [pl.BlockSpec((TQ, DH), kmap(off, 1)) for off in (2, 1, 0)],
        out_specs=pl.BlockSpec((TQ, HPG * DH), lambda g, i, sl: (i, g)),
    )
    return pl.pallas_call(
        functools.partial(_nsawin_body, TQ=TQ),
        grid_spec=grid_spec,
        out_shape=jax.ShapeDtypeStruct((T, NSA_HEADS * DH), F32),
        compiler_params=_params(("parallel", "parallel")),
        name="nsa_win",
    )(slopes, q, *([winb] * 6))


def _nsasel_body(qi_t, ki_t, fl_t, sl_ref, q_ref, k_ref, v_ref, mask_ref, e_ref, gate_ref, oc_ref, ow_ref, o_ref,
                 ms, ls, accs, *, TQ, TK, n_steps):
    g = pl.program_id(0)
    st = g * n_steps + pl.program_id(1)
    qi = qi_t[st]
    ki = ki_t[st]
    fl = fl_t[st]
    t0 = qi * TQ
    k0 = ki * TK

    @pl.when((fl & 4) == 0)
    def _step():
        @pl.when((fl & 1) != 0)
        def _init():
            ms[...] = jnp.full(ms.shape, NEG, F32)
            ls[...] = jnp.zeros(ls.shape, F32)
            accs[...] = jnp.zeros(accs.shape, F32)

        tq = t0 + lax.broadcasted_iota(jnp.int32, (TQ, 1), 0)
        kp = k0 + lax.broadcasted_iota(jnp.int32, (1, TK), 1)
        kprel = (kp - t0).astype(F32)
        sel = _dot(mask_ref[0], e_ref[...])
        bias0 = jnp.where((sel > 0.5) & (tq >= kp), 0.0, NEG)
        k = k_ref[...]
        v = v_ref[...]
        reps = TK // 128
        for r in range(HPG):
            s = _dot_nt(q_ref[:, r * DH:(r + 1) * DH], k) + (sl_ref[g * HPG + r] * kprel + bias0)
            m_old = ms[r]
            m_new = jnp.maximum(m_old, jnp.max(s, axis=1, keepdims=True))
            alpha = jnp.exp2(m_old - m_new)
            p = jnp.exp2(s - jnp.tile(m_new, (1, reps)))
            ls[r] = alpha * ls[r] + jnp.sum(p, axis=1, keepdims=True)
            accs[r] = alpha * accs[r] + _dot(p.astype(BF16), v)
            ms[r] = m_new

        @pl.when((fl & 2) != 0)
        def _finish():
            gt = gate_ref[...]
            for r in range(HPG):
                cs = slice(r * DH, (r + 1) * DH)
                o = (gt[:, 3 * r:3 * r + 1] * oc_ref[:, cs] + gt[:, 3 * r + 1:3 * r + 2] * (accs[r] / ls[r])
                     + gt[:, 3 * r + 2:3 * r + 3] * ow_ref[:, cs])
                o_ref[:, cs] = o.astype(o_ref.dtype)


def nsa_sel(q, kvb, mask, anyq, expand, gates, oc, ow, slopes, TQ=256, TK=512):
    T = q.shape[0]
    NS = mask.shape[2]
    G4 = NSA_GROUPS
    nq, nk = T // TQ, T // TK
    assert T % TK == 0 and T % TQ == 0 and NS == nk * (TK // SEL_BLOCK)
    last = (np.arange(nq) * TQ + TQ - 1) // TK
    causal = np.arange(nk)[None, :] <= last[:, None]
    n_steps = int(causal.sum())
    act = (anyq.reshape(G4, nq, -1, nk, TK // SEL_BLOCK) > 0.5).any(axis=(2, 4)) & jnp.asarray(causal)[None]
    flat = act.reshape(G4, nq * nk)
    cnt = flat.sum(axis=-1).astype(jnp.int32)
    order = jnp.argsort(jnp.logical_not(flat), axis=-1, stable=True)[:, :n_steps].astype(jnp.int32)
    skip = jnp.arange(n_steps, dtype=jnp.int32)[None, :] >= cnt[:, None]
    idx = jnp.where(skip, jnp.take_along_axis(order, (cnt - 1)[:, None], axis=1), order)
    qi_t, ki_t = idx // nk, idx % nk
    fl_t = ((ki_t == 0).astype(jnp.int32) | ((ki_t == jnp.asarray(last, jnp.int32)[qi_t]).astype(jnp.int32) << 1)
            | (skip.astype(jnp.int32) << 2))
    tabs = [x.reshape(-1).astype(jnp.int32) for x in (qi_t, ki_t, fl_t)]
    qblk = pl.BlockSpec((TQ, HPG * DH), lambda g, s, qi, ki, fl, sl: (qi[g * n_steps + s], g))
    grid_spec = pltpu.PrefetchScalarGridSpec(
        num_scalar_prefetch=4,
        grid=(NSA_GROUPS, jnp.max(cnt)),
        in_specs=[
            qblk,
            pl.BlockSpec((TK, DH), lambda g, s, qi, ki, fl, sl: (ki[g * n_steps + s], 2 * G4 + g)),
            pl.BlockSpec((TK, DH), lambda g, s, qi, ki, fl, sl: (ki[g * n_steps + s], 3 * G4 + g)),
            pl.BlockSpec((1, TQ, NS), lambda g, s, qi, ki, fl, sl: (g, qi[g * n_steps + s], 0)),
            pl.BlockSpec((NS, TK), lambda g, s, qi, ki, fl, sl: (0, ki[g * n_steps + s])),
            pl.BlockSpec((TQ, 128), lambda g, s, qi, ki, fl, sl: (qi[g * n_steps + s], g)),
            qblk,
            qblk,
        ],
        out_specs=qblk,
        scratch_shapes=[pltpu.VMEM((HPG, TQ, 128), F32), pltpu.VMEM((HPG, TQ, 128), F32),
                        pltpu.VMEM((HPG, TQ, DH), F32)],
    )
    return pl.pallas_call(
        functools.partial(_nsasel_body, TQ=TQ, TK=TK, n_steps=n_steps),
        grid_spec=grid_spec,
        out_shape=jax.ShapeDtypeStruct((T, NSA_HEADS * DH), BF16),
        compiler_params=_params(("parallel", "arbitrary")),
        name="nsa_sel",
    )(*tabs, slopes, q, kvb, kvb, mask, expand, gates, oc, ow)


def _rowsel(rowg, vals):
    out = vals[0]
    for g in range(1, NSA_GROUPS):
        out = jnp.where(rowg == g, vals[g], out)
    return out


def _nsasels_body(kpos_t, val_t, blk_t, q_ref, *refs, n_slot, sps, Wc, past):
    G4 = NSA_GROUPS
    kv_refs = refs[0:sps * G4]
    (win_ref, knew_ref, vnew_ref, kwnew_ref, vwnew_ref, sl_ref, gc_ref, gs_ref, gw_ref, oc_ref,
     o_ref, m_s, l_s, acc_s) = refs[sps * G4:]
    b = pl.program_id(0)
    s = pl.program_id(1)
    q = q_ref[0]
    rowg = lax.broadcasted_iota(jnp.int32, (NSA_HEADS, 1), 0) // HPG
    slope = sl_ref[...][:, 0:1]

    @pl.when(s == 0)
    def _init():
        m_s[...] = jnp.full(m_s.shape, NEG, F32)
        l_s[...] = jnp.zeros(l_s.shape, F32)
        acc_s[...] = jnp.zeros(acc_s.shape, F32)

    scs, oks = [], []
    m_old = m_s[...]
    m_new = m_old
    for u in range(sps):
        base = (b * NSA_GROUPS) * n_slot + s * sps + u
        sc = _rowsel(rowg, [_dot_nt(q, kv_refs[u * G4 + g][:, g, :].astype(BF16)) for g in range(G4)])
        kp0 = _rowsel(rowg, [kpos_t[base + g * n_slot] for g in range(NSA_GROUPS)])
        ok = _rowsel(rowg, [val_t[base + g * n_slot] for g in range(NSA_GROUPS)]) > 0
        dist = (past - kp0 - lax.broadcasted_iota(jnp.int32, (1, SEL_BLOCK), 1)).astype(F32)
        sc = jnp.where(ok, sc - slope * dist, NEG)
        m_new = jnp.maximum(m_new, jnp.max(sc, axis=1, keepdims=True))
        scs.append(sc)
        oks.append(ok)
    alpha = jnp.exp2(m_old - m_new)
    l_new = alpha * l_s[...]
    acc_new = alpha * acc_s[...]
    for u in range(sps):
        p = jnp.where(oks[u], jnp.exp2(scs[u] - m_new), 0.0)
        pb = p.astype(BF16)
        l_new = l_new + jnp.sum(p, axis=1, keepdims=True)
        acc_new = acc_new + _rowsel(rowg, [_dot(pb, kv_refs[u * G4 + g][:, G4 + g, :].astype(BF16)) for g in range(G4)])
    l_s[...] = l_new
    acc_s[...] = acc_new
    m_s[...] = m_new

    @pl.when(s == n_slot // sps - 1)
    def _finish():
        qf = q.astype(F32)
        sn = jnp.sum(qf * knew_ref[0], axis=1, keepdims=True)
        m1 = m_s[...]
        m2 = jnp.maximum(m1, sn)
        a2 = jnp.exp2(m1 - m2)
        pn = jnp.exp2(sn - m2)
        o_s = (a2 * acc_s[...] + pn * vnew_ref[0]) / (a2 * l_s[...] + pn)
        sw = _rowsel(rowg, [_dot_nt(q, win_ref[:, g, :].astype(BF16)) for g in range(G4)])
        dw = (Wc - lax.broadcasted_iota(jnp.int32, (1, Wc), 1)).astype(F32)
        okw = dw < float(WINDOW)
        sw = jnp.where(okw, sw - slope * dw, NEG)
        swn = jnp.sum(qf * kwnew_ref[0], axis=1, keepdims=True)
        mwin = jnp.maximum(jnp.max(sw, axis=1, keepdims=True), swn)
        pw = jnp.where(okw, jnp.exp2(sw - mwin), 0.0)
        pwn = jnp.exp2(swn - mwin)
        pwb = pw.astype(BF16)
        ow = _rowsel(rowg, [_dot(pwb, win_ref[:, G4 + g, :].astype(BF16)) for g in range(G4)])
        o_w = (ow + pwn * vwnew_ref[0]) / (jnp.sum(pw, axis=1, keepdims=True) + pwn)
        o = gc_ref[0] * oc_ref[0] + gs_ref[0] * o_s + gw_ref[0] * o_w
        o_ref[0] = o.astype(o_ref.dtype)


def nsa_sel_sample(q16, half_pages, blk, val, kpos, win_rows, knew, vnew, kwnew, vwnew, slope16, gc, gs, gw, oc16, past):
    DB = q16.shape[0]
    n_slot = blk.shape[0] // (DB * NSA_GROUPS)
    sps = _pick(n_slot, 4, 1)
    Wc = win_rows.shape[1]
    G4 = NSA_GROUPS

    def kvmap(u, g):
        return lambda b, s, kp, vl, bk: (bk[(b * G4 + g) * n_slot + s * sps + u], 0, 1, 0)

    head = lambda b, s, kp, vl, bk: (b, 0, 0)
    hspec = pl.BlockSpec((1, NSA_HEADS, DH), head)
    grid_spec = pltpu.PrefetchScalarGridSpec(
        num_scalar_prefetch=3,
        grid=(DB, n_slot // sps),
        in_specs=[hspec]
        + [pl.BlockSpec((None, SEL_BLOCK, 2 * G4, DH), kvmap(u, g)) for u in range(sps) for g in range(G4)]
        + [pl.BlockSpec((None, Wc, 2 * G4, DH), lambda b, s, kp, vl, bk: (b, 0, 0, 0))]
        + [hspec] * 4
        + [pl.BlockSpec((NSA_HEADS, DH), lambda b, s, kp, vl, bk: (0, 0))]
        + [hspec] * 4,
        out_specs=hspec,
        scratch_shapes=[pltpu.VMEM((NSA_HEADS, 1), F32), pltpu.VMEM((NSA_HEADS, 1), F32), pltpu.VMEM((NSA_HEADS, DH), F32)],
    )
    return pl.pallas_call(
        functools.partial(_nsasels_body, n_slot=n_slot, sps=sps, Wc=Wc, past=past),
        grid_spec=grid_spec,
        out_shape=jax.ShapeDtypeStruct((DB, NSA_HEADS, DH), BF16),
        compiler_params=_params(("parallel", "arbitrary")),
        name="nsa_sel_sample",
    )(kpos, val, blk, q16, *([half_pages] * (sps * G4)), win_rows, knew, vnew, kwnew, vwnew, slope16, gc, gs, gw, oc16)


def _hgrn_tile(z, qraw, v, og, lb, gn, st):
    TC = z.shape[0]
    C = HG_SUB
    nsub = TC // C
    q = _silu(qraw)
    logf = jnp.log(lb + (1.0 - lb) * _sigmoid(z))
    kk = (1.0 - lb) * _sigmoid(-z)
    tril = (lax.broadcasted_iota(jnp.int32, (TC, TC), 0) >= lax.broadcasted_iota(jnp.int32, (TC, TC), 1)).astype(BF16)
    a, b2, c2 = _split3(logf)
    G = _dot(tril, a) + _dot(tril, b2) + _dot(tril, c2)
    gl = G[TC - 1:TC, :]
    vb = v.astype(BF16)
    tok = lax.broadcasted_iota(jnp.int32, (TC, 1), 0)

    o = _dot_nt((q * jnp.exp(G)).astype(BF16), st.astype(BF16))
    kd = kk * jnp.exp(gl - G)
    upd = lax.dot_general(vb, kd.astype(BF16), (((0,), (0,)), ((), ())), preferred_element_type=F32)
    st_new = st * jnp.exp(gl) + upd

    offs = [jnp.zeros((C, HG_DV), F32)]
    for b in range(1, nsub):
        ref = G[b * C - 1:b * C, :]
        qb = q[b * C:(b + 1) * C, :] * jnp.exp(G[b * C:(b + 1) * C, :] - ref)
        kb = kk * jnp.exp(jnp.where(tok < b * C, ref - G, NEG))
        att = _dot_nt(qb.astype(BF16), kb.astype(BF16))
        offs.append(_dot(att.astype(BF16), vb))
    o = o + jnp.concatenate(offs, axis=0)

    def pick(x, s):
        return jnp.concatenate([jnp.broadcast_to(x[b * C + s:b * C + s + 1, :], (C, x.shape[1])) for b in range(nsub)], axis=0)

    sub = tok % C
    pieces = []
    for s in range(C):
        e = jnp.exp(jnp.where(sub >= s, G - pick(G, s), NEG))
        pieces.append((q * e * pick(kk, s)).astype(BF16))
    rs = _dot(jnp.concatenate(pieces, axis=0), jnp.ones((HG_DK, 128), BF16))
    for s in range(C):
        o = o + rs[s * TC:(s + 1) * TC, :] * pick(v, s)

    y = o * lax.rsqrt(jnp.mean(o * o, axis=-1, keepdims=True) + EPS) * gn * _silu(og)
    return y, st_new


def _hgrn_body(q_ref, f_ref, i_ref, og_ref, lb_ref, gn_ref, s0_ref, o_ref, sout_ref, st_ref, *, HB, nT):
    i = pl.program_id(1)

    @pl.when(i == 0)
    def _init():
        for h in range(HB):
            st_ref[h] = s0_ref[h].T

    for h in range(HB):
        cs = slice(h * HG_DK, (h + 1) * HG_DK)
        y, st_new = _hgrn_tile(f_ref[:, cs], q_ref[:, cs], i_ref[:, cs], og_ref[:, cs], lb_ref[:, cs], gn_ref[...],
                               st_ref[h])
        st_ref[h] = st_new
        o_ref[:, cs] = y.astype(o_ref.dtype)

    @pl.when(i == nT - 1)
    def _fin():
        for h in range(HB):
            sout_ref[h] = st_ref[h].T


def hgrn_prompt(hg, lb, gn, s0, TC=128, HB=4):
    T = hg.shape[0]
    H = HG_HEADS
    nT = T // TC
    nb = H // HB
    W = HB * HG_DK
    return pl.pallas_call(
        functools.partial(_hgrn_body, HB=HB, nT=nT),
        grid=(nb, nT),
        in_specs=[pl.BlockSpec((TC, W), lambda h, i: (i, h)),
                  pl.BlockSpec((TC, W), lambda h, i: (i, nb + h)),
                  pl.BlockSpec((TC, W), lambda h, i: (i, 2 * nb + h)),
                  pl.BlockSpec((TC, W), lambda h, i: (i, 3 * nb + h)),
                  pl.BlockSpec((1, W), lambda h, i: (0, h)),
                  pl.BlockSpec((1, HG_DV), lambda h, i: (0, 0)),
                  pl.BlockSpec((HB, HG_DK, HG_DV), lambda h, i: (h, 0, 0))],
        out_specs=[pl.BlockSpec((TC, W), lambda h, i: (i, h)),
                   pl.BlockSpec((HB, HG_DK, HG_DV), lambda h, i: (h, 0, 0))],
        out_shape=[jax.ShapeDtypeStruct((T, H * HG_DV), BF16), jax.ShapeDtypeStruct((H, HG_DK, HG_DV), F32)],
        scratch_shapes=[pltpu.VMEM((HB, HG_DV, HG_DK), F32)],
        compiler_params=_params(("parallel", "arbitrary")),
        name="hgrn_prompt",
    )(hg, hg, hg, hg, lb, gn, s0)


def _hgrns_body(qc_ref, zc_ref, v_ref, og_ref, lbc_ref, gn_ref, s_ref, o_ref, sout_ref):
    outs = []
    for h in range(HG_HEADS):
        z = zc_ref[0, h]
        lb = lbc_ref[h]
        f = lb + (1.0 - lb) * _sigmoid(z)
        kk = (1.0 - lb) * _sigmoid(-z)
        q = _silu(qc_ref[0, h])
        v = v_ref[0, h:h + 1, :]
        s_new = f * s_ref[0, h] + kk * v
        sout_ref[0, h] = s_new
        o = jnp.sum(q * s_new, axis=0, keepdims=True)
        og = og_ref[0, h:h + 1, :]
        outs.append(o * lax.rsqrt(jnp.mean(o * o, axis=-1, keepdims=True) + EPS) * gn_ref[...] * _silu(og))
    o_ref[0] = jnp.concatenate(outs, axis=0).astype(o_ref.dtype)


def hgrn_sample(qcol, zcol, v, og, lbcol, gn, s0):
    DB = v.shape[0]
    H = HG_HEADS
    col = pl.BlockSpec((1, H, HG_DK, 1), lambda b: (b, 0, 0, 0))
    rowb = pl.BlockSpec((1, H, HG_DV), lambda b: (b, 0, 0))
    st = pl.BlockSpec((1, H, HG_DK, HG_DV), lambda b: (b, 0, 0, 0))
    return pl.pallas_call(
        _hgrns_body,
        grid=(DB,),
        in_specs=[col, col, rowb, rowb, pl.BlockSpec((H, HG_DK, 1), lambda b: (0, 0, 0)),
                  pl.BlockSpec((1, HG_DV), lambda b: (0, 0)), st],
        out_specs=[rowb, st],
        out_shape=[jax.ShapeDtypeStruct((DB, H, HG_DV), BF16), jax.ShapeDtypeStruct((DB, H, HG_DK, HG_DV), F32)],
        compiler_params=_params(("parallel",)),
        name="hgrn_sample",
    )(qcol, zcol, v, og, lbcol, gn, s0)


def _ffn_body(u_ref, uh_ref, wa_ref, wv_ref, cp_ref, wd_ref, h_ref, gt_ref, gf_ref, y_ref, hid_ref, *, nf, tm):
    i = pl.program_id(0)
    f = pl.program_id(1)
    u = u_ref[...]
    a = _dot(u, wa_ref[...])
    v = _dot(u, wv_ref[...])
    ah = _dot(uh_ref[...], wa_ref[...]) * (i > 0).astype(F32)
    cp = cp_ref[...]
    rows = lax.broadcasted_iota(jnp.int32, (tm, 1), 0)
    a1 = jnp.where(rows == 0, ah[7:8, :], pltpu.roll(a, 1, axis=0))
    a2 = jnp.where(rows == 0, ah[6:7, :], jnp.where(rows == 1, ah[7:8, :], pltpu.roll(a, 2, axis=0)))
    conv = cp[3:4, :] + a2 * cp[0:1, :] + a1 * cp[1:2, :] + a * cp[2:3, :]
    hid_ref[f] = (_silu(conv) * v).astype(BF16)

    @pl.when(f == nf - 1)
    def _fin():
        hid = jnp.concatenate([hid_ref[k] for k in range(nf)], axis=1)
        h2 = h_ref[...] + gt_ref[...] * _dot(hid, wd_ref[...])
        y_ref[...] = h2 * lax.rsqrt(jnp.mean(h2 * h2, axis=-1, keepdims=True) + EPS) * gf_ref[...]


def ffn_prompt(u2, h1, wa, wv, cp, wd, gt2, gf, tm=512, tf=512):
    M, D = u2.shape
    Fp = wa.shape[1]
    tm = _pick(M, tm, 8)
    tf = _pick(Fp, tf, 128)
    nf = Fp // tf
    hb = tm // 8
    once = pl.Buffered(1)
    return pl.pallas_call(
        functools.partial(_ffn_body, nf=nf, tm=tm),
        grid=(M // tm, nf),
        in_specs=[pl.BlockSpec((tm, D), lambda i, f: (i, 0)),
                  pl.BlockSpec((8, D), lambda i, f: (jnp.maximum(i * hb - 1, 0), 0)),
                  pl.BlockSpec((D, tf), lambda i, f: (0, f)),
                  pl.BlockSpec((D, tf), lambda i, f: (0, f)),
                  pl.BlockSpec((8, tf), lambda i, f: (0, f)),
                  pl.BlockSpec((Fp, D), lambda i, f: (0, 0), pipeline_mode=once),
                  pl.BlockSpec((tm, D), lambda i, f: (i, 0), pipeline_mode=once),
                  pl.BlockSpec((1, D), lambda i, f: (0, 0)),
                  pl.BlockSpec((1, D), lambda i, f: (0, 0))],
        out_specs=pl.BlockSpec((tm, D), lambda i, f: (i, 0), pipeline_mode=once),
        out_shape=jax.ShapeDtypeStruct((M, D), F32),
        scratch_shapes=[pltpu.VMEM((nf, tm, tf), BF16)],
        compiler_params=pltpu.CompilerParams(dimension_semantics=("parallel", "arbitrary"),
                                             vmem_limit_bytes=V7X_VMEM_LIMIT + 4 * 1024 * 1024),
        name="ffn_prompt",
    )(u2, u2, wa, wv, cp, wd, h1, gt2, gf)


def _alibi_slopes():
    h = np.arange(1, NSA_HEADS + 1, dtype=np.float32)
    return np.asarray(2.0 ** (-8.0 * h / NSA_HEADS), dtype=np.float32)


def _agg_matrix(n_chunk, n_cmp, n_sel, ns_pad):
    i = np.arange(n_chunk)[:, None] * CMP_STRIDE
    j = np.arange(ns_pad)[None, :] * SEL_BLOCK
    m = (i <= j + SEL_BLOCK - 1) & (i + CMP_BLOCK - 1 >= j)
    m &= (np.arange(n_chunk)[:, None] < n_cmp) & (np.arange(ns_pad)[None, :] < n_sel)
    return jnp.asarray(m.astype(np.float32), BF16)


def _layer_weights(w_in, w_ck1, w_cv1, w_br_a, w_br_b, w_out, w_up, conv_w, conv_b, w_down, w_ada):
    D = w_in.shape[0]
    nq, nkv = NSA_HEADS * DH, NSA_GROUPS * DH
    o = 0
    w = {}
    w["q"] = w_in[:, o:o + nq]; o += nq
    w["kv4"] = w_in[:, o:o + 4 * nkv]; o += 4 * nkv
    w["win"] = w_in[:, o:o + 2 * nkv]; o += 2 * nkv
    gate = w_in[:, o:o + 3 * NSA_HEADS]; o += 3 * NSA_HEADS
    gate = jnp.pad(gate.reshape(D, NSA_GROUPS, 3 * HPG), ((0, 0), (0, 0), (0, 128 - 3 * HPG)))
    w["gate"] = gate.reshape(D, NSA_GROUPS * 128)
    nh = 4 * HG_HEADS * HG_DK
    w["hg"] = w_in[:, o:o + nh]; o += nh
    w["ma"] = w_in[:, o:o + D]; o += D
    w["mb"] = w_in[:, o:o + D]; o += D
    half = CMP_STRIDE * DH
    w["ck_ab"] = jnp.concatenate([w_ck1[:half], w_ck1[half:]], axis=1)
    w["cv_ab"] = jnp.concatenate([w_cv1[:half], w_cv1[half:]], axis=1)
    w["br_a"], w["br_b"], w["out"], w["ada"] = w_br_a, w_br_b, w_out, w_ada
    F = w_down.shape[0]
    Fp = -(-F // 512) * 512
    w["up_a"] = jnp.pad(w_up[:, :F], ((0, 0), (0, Fp - F)))
    w["up_v"] = jnp.pad(w_up[:, F:], ((0, 0), (0, Fp - F)))
    w["down"] = jnp.pad(w_down, ((0, Fp - F), (0, 0)))
    w = {k: v.astype(BF16) for k, v in w.items()}
    cp = jnp.concatenate([conv_w, conv_b[None, :], jnp.zeros((8 - CONV_W - 1, F), F32)], axis=0)
    w["conv"] = jnp.pad(cp, ((0, 0), (0, Fp - F)))
    return w


def _project(u, w):
    scale = DH ** -0.5 * LOG2E
    p = {}
    p["q"] = mm(u, w["q"], epilogue=lambda a: a * scale, out_dtypes=(BF16,), name="proj_q")
    p["kv4"], p["kv4b"] = mm(u, w["kv4"], epilogue=lambda a: (a, a), out_dtypes=(F32, BF16), name="proj_kv")
    p["win"], p["winb"] = mm(u, w["win"], epilogue=lambda a: (a, a), out_dtypes=(F32, BF16), name="proj_win")
    p["gate"] = mm(u, w["gate"], epilogue=_sigmoid, name="proj_gate")
    p["hg"] = mm(u, w["hg"], name="proj_hg")
    p["ma"] = mm(u, w["ma"], epilogue=_sigmoid, out_dtypes=(BF16,), name="proj_ma")
    p["mb"] = mm(u, w["mb"], epilogue=_sigmoid, out_dtypes=(BF16,), name="proj_mb")
    return p


def _compressed_kv(rows3, page_tab, w, c_k, c_v, w_ck2, w_cv2):
    zk, zv = cmpz(rows3, page_tab, w["ck_ab"], w["cv_ab"])
    return cmp_finish(zk, c_k, w_ck2), cmp_finish(zv, c_v, w_cv2)


def _merge_out(oa, oh, p, w, x, gt1):
    gkind = "row" if gt1.shape[0] == 1 else "tile"
    a1 = mm(oa, w["br_a"], extras=[(p["ma"], "tile")], epilogue=lambda a, m: a * m.astype(F32),
            out_dtypes=(BF16,), name="branch_a")
    mix = mm(oh, w["br_b"], extras=[(p["mb"], "tile"), (a1, "tile")],
             epilogue=lambda a, m, prev: a * m.astype(F32) + prev.astype(F32), out_dtypes=(BF16,), name="branch_b")
    return mm(mix, w["out"], extras=[(x, "tile"), (gt1, gkind)], epilogue=lambda a, xr, g: xr + g * a, name="out_proj")


def kernel(x_prompt, x_sample, cache_kv, cache_win, state_hgrn, state_conv, page_table, c_prompt, c_sample, w_ada, b_ada, g_norm1, w_in, nsa_pos_k, nsa_pos_v, w_ck1, w_ck2, w_cv1, w_cv2, hg_lb_logits, hg_norm, w_br_a, w_br_b, w_out, g_norm2, w_up, conv_w, conv_b, w_down, g_final):
    B, T, D = x_prompt.shape
    DB = x_sample.shape[0]
    depth = w_in.shape[0]
    assert B == 1 and x_sample.shape[1] == 1 and depth == 1 and T % 512 == 0 and T >= N_SELECT * SEL_BLOCK
    n_pages = page_table.shape[1]
    past = n_pages * PAGE
    Wc = cache_win.shape[2]
    F = w_down.shape[1]
    slopes = jnp.asarray(_alibi_slopes() * np.float32(LOG2E))
    lower_bounds = jnp.cumsum(jax.nn.softmax(hg_lb_logits.astype(F32), axis=0), axis=0)
    layer = 0
    w = _layer_weights(w_in[layer], w_ck1[layer], w_cv1[layer], w_br_a[layer], w_br_b[layer], w_out[layer],
                       w_up[layer], conv_w[layer], conv_b[layer], w_down[layer], w_ada[layer])
    w_ck2b, w_cv2b = w_ck2[layer].astype(BF16), w_cv2[layer].astype(BF16)
    lb = lower_bounds[layer][None, :]
    gn = hg_norm[layer][None, :]
    g1, g2, gf = g_norm1[layer][None, :], g_norm2[layer][None, :], g_final[None, :]

    n_c = -(-(B + DB) // 8) * 8
    c_all = jnp.pad(jnp.concatenate([c_prompt, c_sample], axis=0), ((0, n_c - B - DB), (0, 0)))
    mod = mm(c_all, w["ada"], extras=[(b_ada[layer][None, :], "row")], epilogue=lambda a, b: a + b, act=_silu, name="adaln")
    sh1, sc1, gt1, sh2, sc2, gt2 = [mod[:, k * D:(k + 1) * D] for k in range(6)]
    ps, ss = slice(0, 1), slice(B, B + DB)

    def pos_term(pos, w1):
        return mm(jnp.pad(pos.reshape(1, -1), ((0, 7), (0, 0))), w1.astype(BF16), name="cmp_pos")
    c_k, c_v = pos_term(nsa_pos_k[layer], w_ck1[layer]), pos_term(nsa_pos_v[layer], w_cv1[layer])

    xp = x_prompt[0]
    u = norm_mod(xp, g1, sc1[ps], sh1[ps])
    p = _project(u, w)
    ident = jnp.arange(T // PAGE, dtype=jnp.int32)[None, :]
    kc, vc = _compressed_kv(p["kv4"].reshape(T // PAGE, PAGE, 4 * NSA_GROUPS * DH), ident, w, c_k, c_v, w_ck2b, w_cv2b)
    n_chunk = T // CMP_STRIDE
    n_sel = T // SEL_BLOCK
    agg = _agg_matrix(n_chunk, n_chunk - 1, n_sel, n_sel)
    oc, mask, _, anyq = nsa_cmp(p["q"][None], kc, vc, agg, slopes, q0=0)
    expand = jnp.asarray(np.kron(np.eye(n_sel, dtype=np.float32), np.ones((1, SEL_BLOCK), np.float32)), BF16)
    ow = nsa_win(p["q"], p["winb"], slopes)
    oa = nsa_sel(p["q"], p["kv4b"], mask[0], anyq[0, :, :, 0, :], expand, p["gate"], oc[0], ow, slopes)
    s0p = jnp.zeros((HG_HEADS, HG_DK, HG_DV), F32)
    oh, s_new_p = hgrn_prompt(p["hg"], lb, gn, s0p)
    h1 = _merge_out(oa, oh, p, w, xp, gt1[ps])
    u2 = norm_mod(h1, g2, sc2[ps], sh2[ps])
    y_p = ffn_prompt(u2, h1, w["up_a"], w["up_v"], w["conv"], w["down"], gt2[ps], gf)
    a_tail = mm(u2[T - 8:], w["up_a"], name="ffn_tail")
    kv_new_p = p["kv4"].reshape(1, 1, T, 4, NSA_GROUPS, DH)
    wk = min(WINDOW, T)
    win_new_p = p["win"][T - wk:].reshape(1, 1, wk, 2, NSA_GROUPS, DH)
    conv_new_p = a_tail[8 - (CONV_W - 1):, :F].reshape(1, 1, CONV_W - 1, F)

    xs = x_sample[:, 0]
    us = norm_mod(xs, g1, sc1[ss], sh1[ss])
    q = _project(us, w)
    n_row = 4 * NSA_GROUPS
    kcs, vcs = _compressed_kv(cache_kv.reshape(-1, PAGE, n_row, DH), page_table, w, c_k, c_v, w_ck2b, w_cv2b)
    n_chunk_s = past // CMP_STRIDE
    n_sel_s = past // SEL_BLOCK + 1
    ns_pad = -(-n_sel_s // 128) * 128
    agg_s = _agg_matrix(n_chunk_s, n_chunk_s - 1, n_sel_s, ns_pad)
    TQS = 8
    q_pad = jnp.pad(q["q"][:, None, :], ((0, 0), (0, TQS - 1), (0, 0)))
    ocs, _, idx, _ = nsa_cmp(q_pad, kcs, vcs, agg_s, slopes, q0=past, TQ=TQS)
    idx = idx[:, :, 0, :N_SELECT]
    n_past_blk = past // SEL_BLOCK
    jc = jnp.minimum(idx, n_past_blk - 1)
    per_page = PAGE // SEL_BLOCK
    pg = jnp.take_along_axis(page_table[:, None, :], jc // per_page, axis=2)
    blk = (pg * per_page + jc % per_page).astype(jnp.int32).reshape(-1)
    val = (idx < n_past_blk).astype(jnp.int32).reshape(-1)
    kpos = (jc * SEL_BLOCK).astype(jnp.int32).reshape(-1)

    def per_head(rows):
        return jnp.repeat(rows.reshape(DB, NSA_GROUPS, DH), HPG, axis=1)
    nk = NSA_GROUPS * DH
    gate3 = q["gate"].reshape(DB, NSA_GROUPS, 128)[:, :, :3 * HPG].reshape(DB, NSA_HEADS, 3)
    gb = [jnp.broadcast_to(gate3[:, :, k:k + 1], (DB, NSA_HEADS, 128)) for k in range(3)]
    slope16 = jnp.broadcast_to(slopes[:, None], (NSA_HEADS, DH))
    oas = nsa_sel_sample(
        q["q"].reshape(DB, NSA_HEADS, DH), cache_kv.reshape(-1, SEL_BLOCK, n_row, DH), blk, val, kpos,
        cache_win.reshape(DB, Wc, 2 * NSA_GROUPS, DH),
        per_head(q["kv4"][:, 2 * nk:3 * nk]), per_head(q["kv4"][:, 3 * nk:4 * nk]),
        per_head(q["win"][:, :nk]), per_head(q["win"][:, nk:]),
        slope16, gb[0], gb[1], gb[2], ocs[:, 0].reshape(DB, NSA_HEADS, DH), past)
    hq, hz, hv, hog = [q["hg"][:, k * HG_HEADS * HG_DK:(k + 1) * HG_HEADS * HG_DK] for k in range(4)]
    ohs, s_new_s = hgrn_sample(hq.reshape(DB, HG_HEADS, HG_DK, 1), hz.reshape(DB, HG_HEADS, HG_DK, 1),
                               hv.reshape(DB, HG_HEADS, HG_DV), hog.reshape(DB, HG_HEADS, HG_DV),
                               lb.reshape(HG_HEADS, HG_DK, 1), gn, state_hgrn.reshape(state_hgrn.shape[1:]))
    h1s = _merge_out(oas.reshape(DB, NSA_HEADS * DH), ohs.reshape(DB, HG_HEADS * HG_DV), q, w, xs, gt1[ss])
    u2s = norm_mod(h1s, g2, sc2[ss], sh2[ss])
    Fp = w["up_a"].shape[1]
    a_s = mm(u2s, w["up_a"], name="ffn_s_a")
    conv_buf = state_conv.reshape(state_conv.shape[1:])
    buf = jnp.pad(conv_buf, ((0, 0), (0, 0), (0, Fp - F)))

    def conv_gate(v, a, b0, b1, cw0, cw1, cw2, cb):
        conv = cb + b0 * cw0 + b1 * cw1 + a * cw2
        return _silu(conv) * v
    hid = mm(u2s, w["up_v"], extras=[(a_s, "tile"), (buf[:, 0], "tile"), (buf[:, 1], "tile")]
             + [(w["conv"][k:k + 1], "row") for k in range(4)], epilogue=conv_gate, out_dtypes=(BF16,), name="ffn_s_v")

    def resid_norm(a, h, g, gfin):
        h2 = h + g * a
        return h2 * lax.rsqrt(jnp.mean(h2 * h2, axis=-1, keepdims=True) + EPS) * gfin
    y_s = mm(hid, w["down"], extras=[(h1s, "tile"), (gt2[ss], "tile"), (gf, "row")], epilogue=resid_norm, tn=D, name="ffn_s_down")

    kv_new_s = q["kv4"].reshape(1, DB, 1, 4, NSA_GROUPS, DH)
    win_new_s = jnp.concatenate([cache_win[:, :, 1:], q["win"].reshape(1, DB, 1, 2, NSA_GROUPS, DH)], axis=2)
    conv_new_s = jnp.stack([conv_buf[:, 1], a_s[:, :F]], axis=1)[None]
    return (y_p[None], y_s[:, None, :], kv_new_p, win_new_p, s_new_p[None, None], conv_new_p,
            kv_new_s, win_new_s, s_new_s[None], conv_new_s)
```

```python
import functools
import math

import numpy as np
import jax
import jax.numpy as jnp
from jax import lax
from jax.experimental import pallas as pl
from jax.experimental.pallas import tpu as pltpu

F32 = jnp.float32
BF16 = jnp.bfloat16

NSA_HEADS = 16
NSA_GROUPS = 4
HPG = NSA_HEADS // NSA_GROUPS
DH = 128
CMP_BLOCK = 32
CMP_STRIDE = 16
SEL_BLOCK = 64
N_SELECT = 16
WINDOW = 512
SEL_BONUS = 1.0e6
HG_HEADS = 8
HG_DK = 128
HG_DV = 128
CONV_W = 3
EPS = 1e-6
PAGE = 128

NEG = -1.0e30
LOG2E = math.log2(math.e)
V7X_VMEM_LIMIT = 56 * 1024 * 1024
RESIDENT_W_BYTES = 24 * 1024 * 1024
ROW_TILE_ELEMS = 1 << 20
CHUNKS_PER_PAGE = PAGE // CMP_STRIDE
HG_SUB = 16


def _params(sem):
    return pltpu.CompilerParams(dimension_semantics=sem, vmem_limit_bytes=V7X_VMEM_LIMIT)


def _pick(dim, target, mult):
    if dim <= target:
        return dim
    t = (target // mult) * mult
    while t >= mult:
        if dim % t == 0:
            return t
        t -= mult
    return dim


def _dot(a, b):
    return jnp.dot(a, b, preferred_element_type=F32)


def _dot_nt(a, b):
    return lax.dot_general(a, b, (((1,), (1,)), ((), ())), preferred_element_type=F32)


def _sigmoid(x):
    return 1.0 / (1.0 + jnp.exp(-x))


def _silu(x):
    return x * _sigmoid(x)


def _split3(x):
    a = x.astype(BF16)
    r = x - a.astype(F32)
    b = r.astype(BF16)
    c = (r - b.astype(F32)).astype(BF16)
    return a, b, c


def _mm_body(*refs, n_extra, act, epilogue):
    x_ref, w_ref = refs[0], refs[1]
    extras = refs[2:2 + n_extra]
    outs = refs[2 + n_extra:]
    x = x_ref[...]
    if act is not None:
        x = act(x.astype(F32))
    acc = _dot(x.astype(BF16), w_ref[...].astype(BF16))
    res = epilogue(acc, *[e[...] for e in extras])
    if not isinstance(res, tuple):
        res = (res,)
    for o, r in zip(outs, res):
        o[...] = r.astype(o.dtype)


def mm(x, w, extras=(), epilogue=lambda a: a, out_dtypes=(F32,), act=None, tm=None, tn=None, name="mm"):
    M, K = x.shape
    N = w.shape[1]
    if tn is None:
        tn = N if K * N * w.dtype.itemsize <= RESIDENT_W_BYTES else 512
    tn = _pick(N, tn, 128)
    if tm is None:
        tm = 1024 if tn < N else max(8, min(1024, ROW_TILE_ELEMS // N))
    tm = _pick(M, tm, 8)
    if tn == N:
        w_spec = pl.BlockSpec((K, N), lambda i, j: (0, 0), pipeline_mode=pl.Buffered(1))
    else:
        w_spec = pl.BlockSpec((K, tn), lambda i, j: (0, j))
    in_specs = [pl.BlockSpec((tm, K), lambda i, j: (i, 0)), w_spec]
    args = [x, w]
    for arr, kind in extras:
        if kind == "tile":
            in_specs.append(pl.BlockSpec((tm, tn), lambda i, j: (i, j)))
        else:
            in_specs.append(pl.BlockSpec((1, tn), lambda i, j: (0, j)))
        args.append(arr)
    outs = pl.pallas_call(
        functools.partial(_mm_body, n_extra=len(extras), act=act, epilogue=epilogue),
        grid=(M // tm, N // tn),
        in_specs=in_specs,
        out_specs=[pl.BlockSpec((tm, tn), lambda i, j: (i, j)) for _ in out_dtypes],
        out_shape=[jax.ShapeDtypeStruct((M, N), d) for d in out_dtypes],
        compiler_params=_params(("parallel", "parallel")),
        name=name,
    )(*args)
    return outs if len(outs) > 1 else outs[0]


def _norm_body(x_ref, g_ref, sc_ref, sh_ref, o_ref):
    x = x_ref[...].astype(F32)
    y = x * lax.rsqrt(jnp.mean(x * x, axis=-1, keepdims=True) + EPS) * g_ref[...]
    o_ref[...] = (y * (1.0 + sc_ref[...]) + sh_ref[...]).astype(o_ref.dtype)


def norm_mod(x, g, sc, sh, out_dtype=BF16):
    M, D = x.shape
    tm = _pick(M, 256, 8)
    per_row = sc.shape[0] == M and M > 1
    mod_spec = pl.BlockSpec((tm, D), lambda i: (i, 0)) if per_row else pl.BlockSpec((1, D), lambda i: (0, 0))
    return pl.pallas_call(
        _norm_body,
        grid=(M // tm,),
        in_specs=[pl.BlockSpec((tm, D), lambda i: (i, 0)), pl.BlockSpec((1, D), lambda i: (0, 0)), mod_spec, mod_spec],
        out_specs=pl.BlockSpec((tm, D), lambda i: (i, 0)),
        out_shape=jax.ShapeDtypeStruct((M, D), out_dtype),
        compiler_params=_params(("parallel",)),
        name="norm_mod",
    )(x, g, sc, sh)


def _cmpz_body(pt_ref, *refs, P, flat):
    pages = refs[:P]
    wk_ref, wv_ref, zk_ref, zv_ref = refs[P:P + 4]
    nrow = 2 * NSA_GROUPS
    if flat:
        nc = CHUNKS_PER_PAGE
        ri = lax.broadcasted_iota(jnp.int32, (nc * nrow, nc * nrow), 0)
        ci = lax.broadcasted_iota(jnp.int32, (nc * nrow, nc * nrow), 1)
        swap = (ci == (ri % nc) * nrow + ri // nc).astype(BF16)
        rp = []
        for p in range(CMP_STRIDE):
            a = jnp.concatenate([jnp.concatenate([pages[k][CMP_STRIDE * c + p] for c in range(nc)], axis=0)
                                 for k in range(P)], axis=1)
            rp.append(_dot(swap, a.astype(BF16)))

        def piece(k, p, kind, g):
            j = kind * NSA_GROUPS + g
            return rp[p][j * nc:(j + 1) * nc, k * DH:(k + 1) * DH]
    else:
        ri = lax.broadcasted_iota(jnp.int32, (PAGE, PAGE), 0)
        ci = lax.broadcasted_iota(jnp.int32, (PAGE, PAGE), 1)
        perm = (ci == (ri % CHUNKS_PER_PAGE) * CMP_STRIDE + ri // CHUNKS_PER_PAGE).astype(BF16)
        xp = [_dot(perm, pages[k][...].astype(BF16)) for k in range(P)]

        def piece(k, p, kind, g):
            col = (kind * NSA_GROUPS + g) * DH
            return xp[k][p * CHUNKS_PER_PAGE:(p + 1) * CHUNKS_PER_PAGE, col:col + DH]
    for kind, (w_ref, z_ref) in enumerate(((wk_ref, zk_ref), (wv_ref, zv_ref))):
        rows = []
        for k in range(P):
            for g in range(NSA_GROUPS):
                rows.append(jnp.concatenate([piece(k, p, kind, g) for p in range(CMP_STRIDE)], axis=1))
        y = jnp.concatenate(rows, axis=0).astype(BF16)
        z = _dot(y, w_ref[...])
        z_ref[...] = z.reshape(1, P, NSA_GROUPS, CHUNKS_PER_PAGE, 2 * DH)


def cmpz(rows, page_tab, wk_ab, wv_ab):
    B, n_pages = page_tab.shape
    P = _pick(n_pages, 16, 1)
    flat = rows.ndim == 4
    if flat:
        def page_spec(k):
            return pl.BlockSpec((None, PAGE, 2 * NSA_GROUPS, DH),
                                lambda b, j, pt: (pt[b * n_pages + j * P + k], 0, 0, 0))
    else:
        def page_spec(k):
            return pl.BlockSpec((None, PAGE, 2 * NSA_GROUPS * DH), lambda b, j, pt: (pt[b * n_pages + j * P + k], 0, 0))

    grid_spec = pltpu.PrefetchScalarGridSpec(
        num_scalar_prefetch=1,
        grid=(B, n_pages // P),
        in_specs=[page_spec(k) for k in range(P)]
        + [pl.BlockSpec((CMP_STRIDE * DH, 2 * DH), lambda b, j, pt: (0, 0))] * 2,
        out_specs=[pl.BlockSpec((1, P, NSA_GROUPS, CHUNKS_PER_PAGE, 2 * DH), lambda b, j, pt: (b, j, 0, 0, 0))] * 2,
    )
    zshape = jax.ShapeDtypeStruct((B, n_pages, NSA_GROUPS, CHUNKS_PER_PAGE, 2 * DH), F32)
    return pl.pallas_call(
        functools.partial(_cmpz_body, P=P, flat=flat),
        grid_spec=grid_spec,
        out_shape=[zshape, zshape],
        compiler_params=_params(("parallel", "parallel")),
        name="nsa_cmpz",
    )(page_tab.reshape(-1), *([rows] * P), wk_ab, wv_ab)


def _cmpfin_body(z_ref, c_ref, w2_ref, o_ref, *, n_chunk):
    z = z_ref[0, :, 0].reshape(n_chunk, 2 * DH)
    nxt = pltpu.roll(z[:, DH:], n_chunk - 1, axis=0)
    pre = z[:, :DH] + nxt + c_ref[0:1, :]
    h = jax.nn.gelu(pre)
    o_ref[0, 0] = _dot(h.astype(BF16), w2_ref[...]).astype(o_ref.dtype)


def cmp_finish(z, c_row8, w2):
    B, n_pages = z.shape[0], z.shape[1]
    n_chunk = n_pages * CHUNKS_PER_PAGE
    return pl.pallas_call(
        functools.partial(_cmpfin_body, n_chunk=n_chunk),
        grid=(B, NSA_GROUPS),
        in_specs=[pl.BlockSpec((1, n_pages, 1, CHUNKS_PER_PAGE, 2 * DH), lambda b, g: (b, 0, g, 0, 0)),
                  pl.BlockSpec((8, DH), lambda b, g: (0, 0)),
                  pl.BlockSpec((DH, DH), lambda b, g: (0, 0))],
        out_specs=pl.BlockSpec((1, 1, n_chunk, DH), lambda b, g: (b, g, 0, 0)),
        out_shape=jax.ShapeDtypeStruct((B, NSA_GROUPS, n_chunk, DH), BF16),
        compiler_params=_params(("parallel", "parallel")),
        name="nsa_cmpfin",
    )(z, c_row8, w2)


def _nsacmp_body(sl_ref, q_ref, kc_ref, vc_ref, agg_ref, oc_ref, mask_ref, any_ref, imp_ref,
                 *, TQ, NC, NS, q0, top, n_real):
    g = pl.program_id(1)
    i = pl.program_id(2)
    t0 = q0 + i * TQ
    t = t0 + lax.broadcasted_iota(jnp.int32, (TQ, 1), 0)

    def scores(ncv):
        ce = lax.broadcasted_iota(jnp.int32, (1, ncv), 1) * CMP_STRIDE + (CMP_BLOCK - 1)
        d = (t - ce).astype(F32)
        valid = d >= 0.0
        kc = kc_ref[0, 0, :ncv, :]
        vc = vc_ref[0, 0, :ncv, :]
        psum = jnp.zeros((TQ, ncv), F32)
        for r in range(HPG):
            s = _dot_nt(q_ref[0, :, r * DH:(r + 1) * DH], kc) - sl_ref[g * HPG + r] * d
            m = jnp.max(jnp.where(valid, s, NEG), axis=1, keepdims=True)
            p = jnp.where(valid, jnp.exp2(s - m), 0.0)
            p = p / jnp.maximum(jnp.sum(p, axis=1, keepdims=True), 1e-30)
            oc_ref[0, :, r * DH:(r + 1) * DH] = _dot(p.astype(BF16), vc)
            psum = psum + p
        ph = psum.astype(BF16)
        plo = (psum - ph.astype(F32)).astype(BF16)
        imp_ref[...] = _dot(ph, agg_ref[:ncv, :]) + _dot(plo, agg_ref[:ncv, :])

    nbk = min(4, NC // 128) if NC % 128 == 0 else 1
    share = NC // nbk
    n_vis = jnp.maximum(t0 + TQ - CMP_BLOCK, 0) // CMP_STRIDE + 1
    bucket = jnp.minimum((n_vis + share - 1) // share, nbk)
    for bk in range(1, nbk + 1):
        pl.when(bucket == bk)(functools.partial(scores, bk * share))
    imp = imp_ref[...]
    cur = t // SEL_BLOCK
    j = lax.broadcasted_iota(jnp.int32, (1, NS), 1)
    forced = (j == 0) | (j == cur) | (j == cur - 1)
    imp = jnp.where(j <= cur, imp + jnp.where(forced, SEL_BONUS, 0.0), -SEL_BONUS)
    if n_real is None:
        impT = imp.T
        io = lax.broadcasted_iota(jnp.int32, (NS, TQ), 0).astype(F32)
        selT = jnp.zeros((NS, TQ), F32)
        for _ in range(top):
            mx = jnp.max(impT, axis=0, keepdims=True)
            am = jnp.min(jnp.where(impT == mx, io, float(NS)), axis=0, keepdims=True)
            hit = io == am
            selT = jnp.where(hit, 1.0, selT)
            impT = jnp.where(hit, -3.0e38, impT)
        sel = selT.T
    else:
        colT = jnp.concatenate([imp, jnp.zeros((128 - TQ, NS), F32)], axis=0).T
        ii = lax.broadcasted_iota(jnp.int32, (NS, NS), 0)
        jj = lax.broadcasted_iota(jnp.int32, (NS, NS), 1)
        rowi = lax.broadcasted_iota(jnp.int32, (TQ, 1), 0)
        sel = jnp.zeros((TQ, NS), F32)
        for r in range(n_real):
            a = colT[:, r:r + 1]
            b = imp[r:r + 1, :]
            beats = (a > b) | ((a == b) & (ii < jj))
            rank = jnp.sum(jnp.where(beats, 1.0, 0.0), axis=0, keepdims=True)
            sel = jnp.where(rowi == r, jnp.where(rank < float(top), 1.0, 0.0), sel)
    mask_ref[0, 0] = sel.astype(mask_ref.dtype)
    any_ref[0, 0, 0] = jnp.broadcast_to(jnp.max(sel, axis=0, keepdims=True), (8, NS))


def nsa_cmp(q, kc, vc, agg, slopes, q0, TQ=128, n_real=None):
    B, Tq, _ = q.shape
    NC = kc.shape[2]
    NS = agg.shape[1]
    top = N_SELECT
    grid_spec = pltpu.PrefetchScalarGridSpec(
        num_scalar_prefetch=1,
        grid=(B, NSA_GROUPS, Tq // TQ),
        in_specs=[pl.BlockSpec((1, TQ, HPG * DH), lambda b, g, i, sl: (b, i, g)),
                  pl.BlockSpec((1, 1, NC, DH), lambda b, g, i, sl: (b, g, 0, 0)),
                  pl.BlockSpec((1, 1, NC, DH), lambda b, g, i, sl: (b, g, 0, 0)),
                  pl.BlockSpec((NC, NS), lambda b, g, i, sl: (0, 0))],
        out_specs=[pl.BlockSpec((1, TQ, HPG * DH), lambda b, g, i, sl: (b, i, g)),
                   pl.BlockSpec((1, 1, TQ, NS), lambda b, g, i, sl: (b, g, i, 0)),
                   pl.BlockSpec((1, 1, 1, 8, NS), lambda b, g, i, sl: (b, g, i, 0, 0))],
        scratch_shapes=[pltpu.VMEM((TQ, NS), F32)],
    )
    return pl.pallas_call(
        functools.partial(_nsacmp_body, TQ=TQ, NC=NC, NS=NS, q0=q0, top=top, n_real=n_real),
        grid_spec=grid_spec,
        out_shape=[jax.ShapeDtypeStruct((B, Tq, NSA_HEADS * DH), F32),
                   jax.ShapeDtypeStruct((B, NSA_GROUPS, Tq, NS), BF16),
                   jax.ShapeDtypeStruct((B, NSA_GROUPS, Tq // TQ, 8, NS), F32)],
        compiler_params=_params(("parallel", "parallel", "parallel")),
        name="nsa_cmp",
    )(slopes, q, kc, vc, agg)


def _nsawin_body(sl_ref, q_ref, k0_ref, k1_ref, k2_ref, v0_ref, v1_ref, v2_ref, o_ref, *, TQ):
    g = pl.program_id(0)
    qi = pl.program_id(1)
    t0 = qi * TQ
    tq = t0 + lax.broadcasted_iota(jnp.int32, (TQ, 1), 0)
    kp = t0 - 2 * TQ + lax.broadcasted_iota(jnp.int32, (1, 3 * TQ), 1)
    dist = tq - kp
    bias0 = jnp.where((dist >= 0) & (dist < WINDOW) & (kp >= 0), 0.0, NEG)
    kprel = (kp - t0).astype(F32)
    k = jnp.concatenate([k0_ref[...], k1_ref[...], k2_ref[...]], axis=0)
    v = jnp.concatenate([v0_ref[...], v1_ref[...], v2_ref[...]], axis=0)
    for r in range(HPG):
        s = _dot_nt(q_ref[:, r * DH:(r + 1) * DH], k) + (sl_ref[g * HPG + r] * kprel + bias0)
        p = jnp.exp2(s - jnp.max(s, axis=1, keepdims=True))
        o = _dot(p.astype(BF16), v) / jnp.sum(p, axis=1, keepdims=True)
        o_ref[:, r * DH:(r + 1) * DH] = o.astype(o_ref.dtype)


def nsa_win(q, winb, slopes, TQ=256):
    T = q.shape[0]
    assert WINDOW <= 2 * TQ and T % TQ == 0
    G4 = NSA_GROUPS

    def kmap(off, kind):
        return lambda g, i, sl: (jnp.maximum(i - off, 0), kind * G4 + g)

    grid_spec = pltpu.PrefetchScalarGridSpec(
        num_scalar_prefetch=1,
        grid=(NSA_GROUPS, T // TQ),
        in_specs=[pl.BlockSpec((TQ, HPG * DH), lambda g, i, sl: (i, g))]
        + [pl.BlockSpec((TQ, DH), kmap(off, 0)) for off in (2, 1, 0)]
        + [pl.BlockSpec((TQ, DH), kmap(off, 1)) for off in (2, 1, 0)],
        out_specs=pl.BlockSpec((TQ, HPG * DH), lambda g, i, sl: (i, g)),
    )
    return pl.pallas_call(
        functools.partial(_nsawin_body, TQ=TQ),
        grid_spec=grid_spec,
        out_shape=jax.ShapeDtypeStruct((T, NSA_HEADS * DH), F32),
        compiler_params=_params(("parallel", "parallel")),
        name="nsa_win",
    )(slopes, q, *([winb] * 6))


def _nsasel_body(qi_t, ki_t, fl_t, sl_ref, q_ref, k_ref, v_ref, mask_ref, e_ref, gate_ref, oc_ref, ow_ref, o_ref,
                 ms, ls, accs, *, TQ, TK, n_steps):
    g = pl.program_id(0)
    st = g * n_steps + pl.program_id(1)
    qi = qi_t[st]
    ki = ki_t[st]
    fl = fl_t[st]
    t0 = qi * TQ
    k0 = ki * TK

    @pl.when((fl & 4) == 0)
    def _step():
        @pl.when((fl & 1) != 0)
        def _init():
            ms[...] = jnp.full(ms.shape, NEG, F32)
            ls[...] = jnp.zeros(ls.shape, F32)
            accs[...] = jnp.zeros(accs.shape, F32)

        tq = t0 + lax.broadcasted_iota(jnp.int32, (TQ, 1), 0)
        kp = k0 + lax.broadcasted_iota(jnp.int32, (1, TK), 1)
        kprel = (kp - t0).astype(F32)
        sel = _dot(mask_ref[0], e_ref[...])
        bias0 = jnp.where((sel > 0.5) & (tq >= kp), 0.0, NEG)
        k = k_ref[...]
        v = v_ref[...]
        reps = TK // 128
        for r in range(HPG):
            s = _dot_nt(q_ref[:, r * DH:(r + 1) * DH], k) + (sl_ref[g * HPG + r] * kprel + bias0)
            m_old = ms[r]
            m_new = jnp.maximum(m_old, jnp.max(s, axis=1, keepdims=True))
            alpha = jnp.exp2(m_old - m_new)
            p = jnp.exp2(s - jnp.tile(m_new, (1, reps)))
            ls[r] = alpha * ls[r] + jnp.sum(p, axis=1, keepdims=True)
            accs[r] = alpha * accs[r] + _dot(p.astype(BF16), v)
            ms[r] = m_new

        @pl.when((fl & 2) != 0)
        def _finish():
            gt = gate_ref[...]
            for r in range(HPG):
                cs = slice(r * DH, (r + 1) * DH)
                o = (gt[:, 3 * r:3 * r + 1] * oc_ref[:, cs] + gt[:, 3 * r + 1:3 * r + 2] * (accs[r] / ls[r])
                     + gt[:, 3 * r + 2:3 * r + 3] * ow_ref[:, cs])
                o_ref[:, cs] = o.astype(o_ref.dtype)


def nsa_sel(q, kvb, mask, anyq, expand, gates, oc, ow, slopes, TQ=256, TK=512):
    T = q.shape[0]
    NS = mask.shape[2]
    G4 = NSA_GROUPS
    nq, nk = T // TQ, T // TK
    assert T % TK == 0 and T % TQ == 0 and NS == nk * (TK // SEL_BLOCK)
    last = (np.arange(nq) * TQ + TQ - 1) // TK
    causal = np.arange(nk)[None, :] <= last[:, None]
    n_steps = int(causal.sum())
    act = (anyq.reshape(G4, nq, -1, nk, TK // SEL_BLOCK) > 0.5).any(axis=(2, 4)) & jnp.asarray(causal)[None]
    flat = act.reshape(G4, nq * nk)
    cnt = flat.sum(axis=-1).astype(jnp.int32)
    order = jnp.argsort(jnp.logical_not(flat), axis=-1, stable=True)[:, :n_steps].astype(jnp.int32)
    skip = jnp.arange(n_steps, dtype=jnp.int32)[None, :] >= cnt[:, None]
    idx = jnp.where(skip, jnp.take_along_axis(order, (cnt - 1)[:, None], axis=1), order)
    qi_t, ki_t = idx // nk, idx % nk
    fl_t = ((ki_t == 0).astype(jnp.int32) | ((ki_t == jnp.asarray(last, jnp.int32)[qi_t]).astype(jnp.int32) << 1)
            | (skip.astype(jnp.int32) << 2))
    tabs = [x.reshape(-1).astype(jnp.int32) for x in (qi_t, ki_t, fl_t)]
    qblk = pl.BlockSpec((TQ, HPG * DH), lambda g, s, qi, ki, fl, sl: (qi[g * n_steps + s], g))
    grid_spec = pltpu.PrefetchScalarGridSpec(
        num_scalar_prefetch=4,
        grid=(NSA_GROUPS, jnp.max(cnt)),
        in_specs=[
            qblk,
            pl.BlockSpec((TK, DH), lambda g, s, qi, ki, fl, sl: (ki[g * n_steps + s], 2 * G4 + g)),
            pl.BlockSpec((TK, DH), lambda g, s, qi, ki, fl, sl: (ki[g * n_steps + s], 3 * G4 + g)),
            pl.BlockSpec((1, TQ, NS), lambda g, s, qi, ki, fl, sl: (g, qi[g * n_steps + s], 0)),
            pl.BlockSpec((NS, TK), lambda g, s, qi, ki, fl, sl: (0, ki[g * n_steps + s])),
            pl.BlockSpec((TQ, 128), lambda g, s, qi, ki, fl, sl: (qi[g * n_steps + s], g)),
            qblk,
            qblk,
        ],
        out_specs=qblk,
        scratch_shapes=[pltpu.VMEM((HPG, TQ, 128), F32), pltpu.VMEM((HPG, TQ, 128), F32),
                        pltpu.VMEM((HPG, TQ, DH), F32)],
    )
    return pl.pallas_call(
        functools.partial(_nsasel_body, TQ=TQ, TK=TK, n_steps=n_steps),
        grid_spec=grid_spec,
        out_shape=jax.ShapeDtypeStruct((T, NSA_HEADS * DH), BF16),
        compiler_params=_params(("parallel", "arbitrary")),
        name="nsa_sel",
    )(*tabs, slopes, q, kvb, kvb, mask, expand, gates, oc, ow)


def _rowsel(rowg, vals):
    out = vals[0]
    for g in range(1, NSA_GROUPS):
        out = jnp.where(rowg == g, vals[g], out)
    return out


def _nsasels_body(kpos_t, val_t, blk_t, q_ref, *refs, n_slot, sps, Wc, past):
    G4 = NSA_GROUPS
    kv_refs = refs[0:sps * G4]
    (win_ref, knew_ref, vnew_ref, kwnew_ref, vwnew_ref, sl_ref, gc_ref, gs_ref, gw_ref, oc_ref,
     o_ref, m_s, l_s, acc_s) = refs[sps * G4:]
    b = pl.program_id(0)
    s = pl.program_id(1)
    q = q_ref[0]
    rowg = lax.broadcasted_iota(jnp.int32, (NSA_HEADS, 1), 0) // HPG
    slope = sl_ref[...][:, 0:1]

    @pl.when(s == 0)
    def _init():
        m_s[...] = jnp.full(m_s.shape, NEG, F32)
        l_s[...] = jnp.zeros(l_s.shape, F32)
        acc_s[...] = jnp.zeros(acc_s.shape, F32)

    scs, oks = [], []
    m_old = m_s[...]
    m_new = m_old
    for u in range(sps):
        base = (b * NSA_GROUPS) * n_slot + s * sps + u
        sc = _rowsel(rowg, [_dot_nt(q, kv_refs[u * G4 + g][:, g, :].astype(BF16)) for g in range(G4)])
        kp0 = _rowsel(rowg, [kpos_t[base + g * n_slot] for g in range(NSA_GROUPS)])
        ok = _rowsel(rowg, [val_t[base + g * n_slot] for g in range(NSA_GROUPS)]) > 0
        dist = (past - kp0 - lax.broadcasted_iota(jnp.int32, (1, SEL_BLOCK), 1)).astype(F32)
        sc = jnp.where(ok, sc - slope * dist, NEG)
        m_new = jnp.maximum(m_new, jnp.max(sc, axis=1, keepdims=True))
        scs.append(sc)
        oks.append(ok)
    alpha = jnp.exp2(m_old - m_new)
    l_new = alpha * l_s[...]
    acc_new = alpha * acc_s[...]
    for u in range(sps):
        p = jnp.where(oks[u], jnp.exp2(scs[u] - m_new), 0.0)
        pb = p.astype(BF16)
        l_new = l_new + jnp.sum(p, axis=1, keepdims=True)
        acc_new = acc_new + _rowsel(rowg, [_dot(pb, kv_refs[u * G4 + g][:, G4 + g, :].astype(BF16)) for g in range(G4)])
    l_s[...] = l_new
    acc_s[...] = acc_new
    m_s[...] = m_new

    @pl.when(s == n_slot // sps - 1)
    def _finish():
        qf = q.astype(F32)
        sn = jnp.sum(qf * knew_ref[0], axis=1, keepdims=True)
        m1 = m_s[...]
        m2 = jnp.maximum(m1, sn)
        a2 = jnp.exp2(m1 - m2)
        pn = jnp.exp2(sn - m2)
        o_s = (a2 * acc_s[...] + pn * vnew_ref[0]) / (a2 * l_s[...] + pn)
        sw = _rowsel(rowg, [_dot_nt(q, win_ref[:, g, :].astype(BF16)) for g in range(G4)])
        dw = (Wc - lax.broadcasted_iota(jnp.int32, (1, Wc), 1)).astype(F32)
        okw = dw < float(WINDOW)
        sw = jnp.where(okw, sw - slope * dw, NEG)
        swn = jnp.sum(qf * kwnew_ref[0], axis=1, keepdims=True)
        mwin = jnp.maximum(jnp.max(sw, axis=1, keepdims=True), swn)
        pw = jnp.where(okw, jnp.exp2(sw - mwin), 0.0)
        pwn = jnp.exp2(swn - mwin)
        pwb = pw.astype(BF16)
        ow = _rowsel(rowg, [_dot(pwb, win_ref[:, G4 + g, :].astype(BF16)) for g in range(G4)])
        o_w = (ow + pwn * vwnew_ref[0]) / (jnp.sum(pw, axis=1, keepdims=True) + pwn)
        o = gc_ref[0] * oc_ref[0] + gs_ref[0] * o_s + gw_ref[0] * o_w
        o_ref[0] = o.astype(o_ref.dtype)


def nsa_sel_sample(q16, half_pages, blk, val, kpos, win_rows, knew, vnew, kwnew, vwnew, slope16, gc, gs, gw, oc16, past):
    DB = q16.shape[0]
    n_slot = blk.shape[0] // (DB * NSA_GROUPS)
    sps = _pick(n_slot, 4, 1)
    Wc = win_rows.shape[1]
    G4 = NSA_GROUPS

    def kvmap(u, g):
        return lambda b, s, kp, vl, bk: (bk[(b * G4 + g) * n_slot + s * sps + u], 0, 1, 0)

    head = lambda b, s, kp, vl, bk: (b, 0, 0)
    hspec = pl.BlockSpec((1, NSA_HEADS, DH), head)
    grid_spec = pltpu.PrefetchScalarGridSpec(
        num_scalar_prefetch=3,
        grid=(DB, n_slot // sps),
        in_specs=[hspec]
        + [pl.BlockSpec((None, SEL_BLOCK, 2 * G4, DH), kvmap(u, g)) for u in range(sps) for g in range(G4)]
        + [pl.BlockSpec((None, Wc, 2 * G4, DH), lambda b, s, kp, vl, bk: (b, 0, 0, 0))]
        + [hspec] * 4
        + [pl.BlockSpec((NSA_HEADS, DH), lambda b, s, kp, vl, bk: (0, 0))]
        + [hspec] * 4,
        out_specs=hspec,
        scratch_shapes=[pltpu.VMEM((NSA_HEADS, 1), F32), pltpu.VMEM((NSA_HEADS, 1), F32), pltpu.VMEM((NSA_HEADS, DH), F32)],
    )
    return pl.pallas_call(
        functools.partial(_nsasels_body, n_slot=n_slot, sps=sps, Wc=Wc, past=past),
        grid_spec=grid_spec,
        out_shape=jax.ShapeDtypeStruct((DB, NSA_HEADS, DH), BF16),
        compiler_params=_params(("parallel", "arbitrary")),
        name="nsa_sel_sample",
    )(kpos, val, blk, q16, *([half_pages] * (sps * G4)), win_rows, knew, vnew, kwnew, vwnew, slope16, gc, gs, gw, oc16)


def _hgrn_tile(z, qraw, v, og, lb, gn, st):
    TC = z.shape[0]
    C = HG_SUB
    nsub = TC // C
    q = _silu(qraw)
    logf = jnp.log(lb + (1.0 - lb) * _sigmoid(z))
    kk = (1.0 - lb) * _sigmoid(-z)
    tril = (lax.broadcasted_iota(jnp.int32, (TC, TC), 0) >= lax.broadcasted_iota(jnp.int32, (TC, TC), 1)).astype(BF16)
    a, b2, c2 = _split3(logf)
    G = _dot(tril, a) + _dot(tril, b2) + _dot(tril, c2)
    gl = G[TC - 1:TC, :]
    vb = v.astype(BF16)
    tok = lax.broadcasted_iota(jnp.int32, (TC, 1), 0)

    o = _dot_nt((q * jnp.exp(G)).astype(BF16), st.astype(BF16))
    kd = kk * jnp.exp(gl - G)
    upd = lax.dot_general(vb, kd.astype(BF16), (((0,), (0,)), ((), ())), preferred_element_type=F32)
    st_new = st * jnp.exp(gl) + upd

    offs = [jnp.zeros((C, HG_DV), F32)]
    for b in range(1, nsub):
        ref = G[b * C - 1:b * C, :]
        qb = q[b * C:(b + 1) * C, :] * jnp.exp(G[b * C:(b + 1) * C, :] - ref)
        kb = kk * jnp.exp(jnp.where(tok < b * C, ref - G, NEG))
        att = _dot_nt(qb.astype(BF16), kb.astype(BF16))
        offs.append(_dot(att.astype(BF16), vb))
    o = o + jnp.concatenate(offs, axis=0)

    def pick(x, s):
        return jnp.concatenate([jnp.broadcast_to(x[b * C + s:b * C + s + 1, :], (C, x.shape[1])) for b in range(nsub)], axis=0)

    sub = tok % C
    pieces = []
    for s in range(C):
        e = jnp.exp(jnp.where(sub >= s, G - pick(G, s), NEG))
        pieces.append((q * e * pick(kk, s)).astype(BF16))
    rs = _dot(jnp.concatenate(pieces, axis=0), jnp.ones((HG_DK, 128), BF16))
    for s in range(C):
        o = o + rs[s * TC:(s + 1) * TC, :] * pick(v, s)

    y = o * lax.rsqrt(jnp.mean(o * o, axis=-1, keepdims=True) + EPS) * gn * _silu(og)
    return y, st_new


def _hgrn_body(q_ref, f_ref, i_ref, og_ref, lb_ref, gn_ref, s0_ref, o_ref, sout_ref, st_ref, *, HB, nT):
    i = pl.program_id(1)

    @pl.when(i == 0)
    def _init():
        for h in range(HB):
            st_ref[h] = s0_ref[h].T

    for h in range(HB):
        cs = slice(h * HG_DK, (h + 1) * HG_DK)
        y, st_new = _hgrn_tile(f_ref[:, cs], q_ref[:, cs], i_ref[:, cs], og_ref[:, cs], lb_ref[:, cs], gn_ref[...],
                               st_ref[h])
        st_ref[h] = st_new
        o_ref[:, cs] = y.astype(o_ref.dtype)

    @pl.when(i == nT - 1)
    def _fin():
        for h in range(HB):
            sout_ref[h] = st_ref[h].T


def hgrn_prompt(hg, lb, gn, s0, TC=128, HB=4):
    T = hg.shape[0]
    H = HG_HEADS
    nT = T // TC
    nb = H // HB
    W = HB * HG_DK
    return pl.pallas_call(
        functools.partial(_hgrn_body, HB=HB, nT=nT),
        grid=(nb, nT),
        in_specs=[pl.BlockSpec((TC, W), lambda h, i: (i, h)),
                  pl.BlockSpec((TC, W), lambda h, i: (i, nb + h)),
                  pl.BlockSpec((TC, W), lambda h, i: (i, 2 * nb + h)),
                  pl.BlockSpec((TC, W), lambda h, i: (i, 3 * nb + h)),
                  pl.BlockSpec((1, W), lambda h, i: (0, h)),
                  pl.BlockSpec((1, HG_DV), lambda h, i: (0, 0)),
                  pl.BlockSpec((HB, HG_DK, HG_DV), lambda h, i: (h, 0, 0))],
        out_specs=[pl.BlockSpec((TC, W), lambda h, i: (i, h)),
                   pl.BlockSpec((HB, HG_DK, HG_DV), lambda h, i: (h, 0, 0))],
        out_shape=[jax.ShapeDtypeStruct((T, H * HG_DV), BF16), jax.ShapeDtypeStruct((H, HG_DK, HG_DV), F32)],
        scratch_shapes=[pltpu.VMEM((HB, HG_DV, HG_DK), F32)],
        compiler_params=_params(("parallel", "arbitrary")),
        name="hgrn_prompt",
    )(hg, hg, hg, hg, lb, gn, s0)


def _hgrns_body(qc_ref, zc_ref, v_ref, og_ref, lbc_ref, gn_ref, s_ref, o_ref, sout_ref):
    outs = []
    for h in range(HG_HEADS):
        z = zc_ref[0, h]
        lb = lbc_ref[h]
        f = lb + (1.0 - lb) * _sigmoid(z)
        kk = (1.0 - lb) * _sigmoid(-z)
        q = _silu(qc_ref[0, h])
        v = v_ref[0, h:h + 1, :]
        s_new = f * s_ref[0, h] + kk * v
        sout_ref[0, h] = s_new
        o = jnp.sum(q * s_new, axis=0, keepdims=True)
        og = og_ref[0, h:h + 1, :]
        outs.append(o * lax.rsqrt(jnp.mean(o * o, axis=-1, keepdims=True) + EPS) * gn_ref[...] * _silu(og))
    o_ref[0] = jnp.concatenate(outs, axis=0).astype(o_ref.dtype)


def hgrn_sample(qcol, zcol, v, og, lbcol, gn, s0):
    DB = v.shape[0]
    H = HG_HEADS
    col = pl.BlockSpec((1, H, HG_DK, 1), lambda b: (b, 0, 0, 0))
    rowb = pl.BlockSpec((1, H, HG_DV), lambda b: (b, 0, 0))
    st = pl.BlockSpec((1, H, HG_DK, HG_DV), lambda b: (b, 0, 0, 0))
    return pl.pallas_call(
        _hgrns_body,
        grid=(DB,),
        in_specs=[col, col, rowb, rowb, pl.BlockSpec((H, HG_DK, 1), lambda b: (0, 0, 0)),
                  pl.BlockSpec((1, HG_DV), lambda b: (0, 0)), st],
        out_specs=[rowb, st],
        out_shape=[jax.ShapeDtypeStruct((DB, H, HG_DV), BF16), jax.ShapeDtypeStruct((DB, H, HG_DK, HG_DV), F32)],
        compiler_params=_params(("parallel",)),
        name="hgrn_sample",
    )(qcol, zcol, v, og, lbcol, gn, s0)


def _ffn_body(u_ref, uh_ref, wa_ref, wv_ref, cp_ref, wd_ref, h_ref, gt_ref, gf_ref, y_ref, hid_ref, *, nf, tm):
    i = pl.program_id(0)
    f = pl.program_id(1)
    u = u_ref[...]
    a = _dot(u, wa_ref[...])
    v = _dot(u, wv_ref[...])
    ah = _dot(uh_ref[...], wa_ref[...]) * (i > 0).astype(F32)
    cp = cp_ref[...]
    rows = lax.broadcasted_iota(jnp.int32, (tm, 1), 0)
    a1 = jnp.where(rows == 0, ah[7:8, :], pltpu.roll(a, 1, axis=0))
    a2 = jnp.where(rows == 0, ah[6:7, :], jnp.where(rows == 1, ah[7:8, :], pltpu.roll(a, 2, axis=0)))
    conv = cp[3:4, :] + a2 * cp[0:1, :] + a1 * cp[1:2, :] + a * cp[2:3, :]
    hid_ref[f] = (_silu(conv) * v).astype(BF16)

    @pl.when(f == nf - 1)
    def _fin():
        hid = jnp.concatenate([hid_ref[k] for k in range(nf)], axis=1)
        h2 = h_ref[...] + gt_ref[...] * _dot(hid, wd_ref[...])
        y_ref[...] = h2 * lax.rsqrt(jnp.mean(h2 * h2, axis=-1, keepdims=True) + EPS) * gf_ref[...]


def ffn_prompt(u2, h1, wa, wv, cp, wd, gt2, gf, tm=512, tf=512):
    M, D = u2.shape
    Fp = wa.shape[1]
    tm = _pick(M, tm, 8)
    tf = _pick(Fp, tf, 128)
    nf = Fp // tf
    hb = tm // 8
    once = pl.Buffered(1)
    wa = wa.reshape(D, nf, tf).transpose(1, 0, 2)
    wv = wv.reshape(D, nf, tf).transpose(1, 0, 2)
    return pl.pallas_call(
        functools.partial(_ffn_body, nf=nf, tm=tm),
        grid=(M // tm, nf),
        in_specs=[pl.BlockSpec((tm, D), lambda i, f: (i, 0)),
                  pl.BlockSpec((8, D), lambda i, f: (jnp.maximum(i * hb - 1, 0), 0)),
                  pl.BlockSpec((None, D, tf), lambda i, f: (f, 0, 0)),
                  pl.BlockSpec((None, D, tf), lambda i, f: (f, 0, 0)),
                  pl.BlockSpec((8, tf), lambda i, f: (0, f)),
                  pl.BlockSpec((Fp, D), lambda i, f: (0, 0), pipeline_mode=once),
                  pl.BlockSpec((tm, D), lambda i, f: (i, 0), pipeline_mode=once),
                  pl.BlockSpec((1, D), lambda i, f: (0, 0)),
                  pl.BlockSpec((1, D), lambda i, f: (0, 0))],
        out_specs=pl.BlockSpec((tm, D), lambda i, f: (i, 0), pipeline_mode=once),
        out_shape=jax.ShapeDtypeStruct((M, D), F32),
        scratch_shapes=[pltpu.VMEM((nf, tm, tf), BF16)],
        compiler_params=pltpu.CompilerParams(dimension_semantics=("parallel", "arbitrary"),
                                             vmem_limit_bytes=V7X_VMEM_LIMIT + 4 * 1024 * 1024),
        name="ffn_prompt",
    )(u2, u2, wa, wv, cp, wd, h1, gt2, gf)


def _alibi_slopes():
    h = np.arange(1, NSA_HEADS + 1, dtype=np.float32)
    return np.asarray(2.0 ** (-8.0 * h / NSA_HEADS), dtype=np.float32)


def _agg_matrix(n_chunk, n_cmp, n_sel, ns_pad):
    i = np.arange(n_chunk)[:, None] * CMP_STRIDE
    j = np.arange(ns_pad)[None, :] * SEL_BLOCK
    m = (i <= j + SEL_BLOCK - 1) & (i + CMP_BLOCK - 1 >= j)
    m &= (np.arange(n_chunk)[:, None] < n_cmp) & (np.arange(ns_pad)[None, :] < n_sel)
    return jnp.asarray(m.astype(np.float32), BF16)


def _layer_weights(w_in, w_ck1, w_cv1, w_br_a, w_br_b, w_out, w_up, conv_w, conv_b, w_down, w_ada):
    D = w_in.shape[0]
    nq, nkv = NSA_HEADS * DH, NSA_GROUPS * DH
    o = 0
    w = {}
    w["q"] = w_in[:, o:o + nq]; o += nq
    w["kv4"] = w_in[:, o:o + 4 * nkv]; o += 4 * nkv
    w["win"] = w_in[:, o:o + 2 * nkv]; o += 2 * nkv
    gate = w_in[:, o:o + 3 * NSA_HEADS]; o += 3 * NSA_HEADS
    gate = jnp.pad(gate.reshape(D, NSA_GROUPS, 3 * HPG), ((0, 0), (0, 0), (0, 128 - 3 * HPG)))
    w["gate"] = gate.reshape(D, NSA_GROUPS * 128)
    nh = 4 * HG_HEADS * HG_DK
    w["hg"] = w_in[:, o:o + nh]; o += nh
    w["ma"] = w_in[:, o:o + D]; o += D
    w["mb"] = w_in[:, o:o + D]; o += D
    half = CMP_STRIDE * DH
    w["ck_ab"] = jnp.concatenate([w_ck1[:half], w_ck1[half:]], axis=1)
    w["cv_ab"] = jnp.concatenate([w_cv1[:half], w_cv1[half:]], axis=1)
    w["br_a"], w["br_b"], w["out"], w["ada"] = w_br_a, w_br_b, w_out, w_ada
    F = w_down.shape[0]
    Fp = -(-F // 512) * 512
    w["up_a"] = jnp.pad(w_up[:, :F], ((0, 0), (0, Fp - F)))
    w["up_v"] = jnp.pad(w_up[:, F:], ((0, 0), (0, Fp - F)))
    w["down"] = jnp.pad(w_down, ((0, Fp - F), (0, 0)))
    w = {k: v.astype(BF16) for k, v in w.items()}
    cp = jnp.concatenate([conv_w, conv_b[None, :], jnp.zeros((8 - CONV_W - 1, F), F32)], axis=0)
    w["conv"] = jnp.pad(cp, ((0, 0), (0, Fp - F)))
    return w


def _project(u, w):
    scale = DH ** -0.5 * LOG2E
    p = {}
    p["q"] = mm(u, w["q"], epilogue=lambda a: a * scale, out_dtypes=(BF16,), name="proj_q")
    p["kv4"], p["kv4b"] = mm(u, w["kv4"], epilogue=lambda a: (a, a), out_dtypes=(F32, BF16), name="proj_kv")
    p["win"], p["winb"] = mm(u, w["win"], epilogue=lambda a: (a, a), out_dtypes=(F32, BF16), name="proj_win")
    p["gate"] = mm(u, w["gate"], epilogue=_sigmoid, name="proj_gate")
    p["hg"] = mm(u, w["hg"], name="proj_hg")
    p["ma"] = mm(u, w["ma"], epilogue=_sigmoid, out_dtypes=(BF16,), name="proj_ma")
    p["mb"] = mm(u, w["mb"], epilogue=_sigmoid, out_dtypes=(BF16,), name="proj_mb")
    return p


def _compressed_kv(rows3, page_tab, w, c_k, c_v, w_ck2, w_cv2):
    zk, zv = cmpz(rows3, page_tab, w["ck_ab"], w["cv_ab"])
    return cmp_finish(zk, c_k, w_ck2), cmp_finish(zv, c_v, w_cv2)


def _merge_out(oa, oh, p, w, x, gt1):
    gkind = "row" if gt1.shape[0] == 1 else "tile"
    a1 = mm(oa, w["br_a"], extras=[(p["ma"], "tile")], epilogue=lambda a, m: a * m.astype(F32),
            out_dtypes=(BF16,), name="branch_a")
    mix = mm(oh, w["br_b"], extras=[(p["mb"], "tile"), (a1, "tile")],
             epilogue=lambda a, m, prev: a * m.astype(F32) + prev.astype(F32), out_dtypes=(BF16,), name="branch_b")
    return mm(mix, w["out"], extras=[(x, "tile"), (gt1, gkind)], epilogue=lambda a, xr, g: xr + g * a, name="out_proj")


def kernel(x_prompt, x_sample, cache_kv, cache_win, state_hgrn, state_conv, page_table, c_prompt, c_sample, w_ada, b_ada, g_norm1, w_in, nsa_pos_k, nsa_pos_v, w_ck1, w_ck2, w_cv1, w_cv2, hg_lb_logits, hg_norm, w_br_a, w_br_b, w_out, g_norm2, w_up, conv_w, conv_b, w_down, g_final):
    B, T, D = x_prompt.shape
    DB = x_sample.shape[0]
    depth = w_in.shape[0]
    assert B == 1 and x_sample.shape[1] == 1 and depth == 1 and T % 512 == 0 and T >= N_SELECT * SEL_BLOCK
    n_pages = page_table.shape[1]
    past = n_pages * PAGE
    Wc = cache_win.shape[2]
    F = w_down.shape[1]
    slopes = jnp.asarray(_alibi_slopes() * np.float32(LOG2E))
    lower_bounds = jnp.cumsum(jax.nn.softmax(hg_lb_logits.astype(F32), axis=0), axis=0)
    layer = 0
    w = _layer_weights(w_in[layer], w_ck1[layer], w_cv1[layer], w_br_a[layer], w_br_b[layer], w_out[layer],
                       w_up[layer], conv_w[layer], conv_b[layer], w_down[layer], w_ada[layer])
    w_ck2b, w_cv2b = w_ck2[layer].astype(BF16), w_cv2[layer].astype(BF16)
    lb = lower_bounds[layer][None, :]
    gn = hg_norm[layer][None, :]
    g1, g2, gf = g_norm1[layer][None, :], g_norm2[layer][None, :], g_final[None, :]

    n_c = -(-(B + DB) // 8) * 8
    c_all = jnp.pad(jnp.concatenate([c_prompt, c_sample], axis=0), ((0, n_c - B - DB), (0, 0)))
    mod = mm(c_all, w["ada"], extras=[(b_ada[layer][None, :], "row")], epilogue=lambda a, b: a + b, act=_silu, name="adaln")
    sh1, sc1, gt1, sh2, sc2, gt2 = [mod[:, k * D:(k + 1) * D] for k in range(6)]
    ps, ss = slice(0, 1), slice(B, B + DB)

    def pos_term(pos, w1):
        return mm(jnp.pad(pos.reshape(1, -1), ((0, 7), (0, 0))), w1.astype(BF16), name="cmp_pos")
    c_k, c_v = pos_term(nsa_pos_k[layer], w_ck1[layer]), pos_term(nsa_pos_v[layer], w_cv1[layer])

    xp = x_prompt[0]
    u = norm_mod(xp, g1, sc1[ps], sh1[ps])
    p = _project(u, w)
    ident = jnp.arange(T // PAGE, dtype=jnp.int32)[None, :]
    kc, vc = _compressed_kv(p["kv4"].reshape(T // PAGE, PAGE, 4 * NSA_GROUPS * DH), ident, w, c_k, c_v, w_ck2b, w_cv2b)
    n_chunk = T // CMP_STRIDE
    n_sel = T // SEL_BLOCK
    agg = _agg_matrix(n_chunk, n_chunk - 1, n_sel, n_sel)
    oc, mask, anyq = nsa_cmp(p["q"][None], kc, vc, agg, slopes, q0=0)
    expand = jnp.asarray(np.kron(np.eye(n_sel, dtype=np.float32), np.ones((1, SEL_BLOCK), np.float32)), BF16)
    ow = nsa_win(p["q"], p["winb"], slopes)
    oa = nsa_sel(p["q"], p["kv4b"], mask[0], anyq[0, :, :, 0, :], expand, p["gate"], oc[0], ow, slopes)
    s0p = jnp.zeros((HG_HEADS, HG_DK, HG_DV), F32)
    oh, s_new_p = hgrn_prompt(p["hg"], lb, gn, s0p)
    h1 = _merge_out(oa, oh, p, w, xp, gt1[ps])
    u2 = norm_mod(h1, g2, sc2[ps], sh2[ps])
    y_p = ffn_prompt(u2, h1, w["up_a"], w["up_v"], w["conv"], w["down"], gt2[ps], gf)
    a_tail = mm(u2[T - 8:], w["up_a"], name="ffn_tail")
    kv_new_p = p["kv4"].reshape(1, 1, T, 4, NSA_GROUPS, DH)
    wk = min(WINDOW, T)
    win_new_p = p["win"][T - wk:].reshape(1, 1, wk, 2, NSA_GROUPS, DH)
    conv_new_p = a_tail[8 - (CONV_W - 1):, :F].reshape(1, 1, CONV_W - 1, F)

    xs = x_sample[:, 0]
    us = norm_mod(xs, g1, sc1[ss], sh1[ss])
    q = _project(us, w)
    n_row = 4 * NSA_GROUPS
    kcs, vcs = _compressed_kv(cache_kv.reshape(-1, PAGE, n_row, DH), page_table, w, c_k, c_v, w_ck2b, w_cv2b)
    n_chunk_s = past // CMP_STRIDE
    n_sel_s = past // SEL_BLOCK + 1
    ns_pad = -(-n_sel_s // 128) * 128
    agg_s = _agg_matrix(n_chunk_s, n_chunk_s - 1, n_sel_s, ns_pad)
    TQS = 8
    q_pad = jnp.pad(q["q"][:, None, :], ((0, 0), (0, TQS - 1), (0, 0)))
    ocs, mask_s, _ = nsa_cmp(q_pad, kcs, vcs, agg_s, slopes, q0=past, TQ=TQS, n_real=1)
    idx = jnp.argsort(1.0 - mask_s[:, :, 0, :].astype(F32), axis=-1, stable=True)[..., :N_SELECT].astype(jnp.int32)
    n_past_blk = past // SEL_BLOCK
    jc = jnp.minimum(idx, n_past_blk - 1)
    per_page = PAGE // SEL_BLOCK
    pg = jnp.take_along_axis(page_table[:, None, :], jc // per_page, axis=2)
    blk = (pg * per_page + jc % per_page).astype(jnp.int32).reshape(-1)
    val = (idx < n_past_blk).astype(jnp.int32).reshape(-1)
    kpos = (jc * SEL_BLOCK).astype(jnp.int32).reshape(-1)

    def per_head(rows):
        return jnp.repeat(rows.reshape(DB, NSA_GROUPS, DH), HPG, axis=1)
    nk = NSA_GROUPS * DH
    gate3 = q["gate"].reshape(DB, NSA_GROUPS, 128)[:, :, :3 * HPG].reshape(DB, NSA_HEADS, 3)
    gb = [jnp.broadcast_to(gate3[:, :, k:k + 1], (DB, NSA_HEADS, 128)) for k in range(3)]
    slope16 = jnp.broadcast_to(slopes[:, None], (NSA_HEADS, DH))
    oas = nsa_sel_sample(
        q["q"].reshape(DB, NSA_HEADS, DH), cache_kv.reshape(-1, SEL_BLOCK, n_row, DH), blk, val, kpos,
        cache_win.reshape(DB, Wc, 2 * NSA_GROUPS, DH),
        per_head(q["kv4"][:, 2 * nk:3 * nk]), per_head(q["kv4"][:, 3 * nk:4 * nk]),
        per_head(q["win"][:, :nk]), per_head(q["win"][:, nk:]),
        slope16, gb[0], gb[1], gb[2], ocs[:, 0].reshape(DB, NSA_HEADS, DH), past)
    hq, hz, hv, hog = [q["hg"][:, k * HG_HEADS * HG_DK:(k + 1) * HG_HEADS * HG_DK] for k in range(4)]
    ohs, s_new_s = hgrn_sample(hq.reshape(DB, HG_HEADS, HG_DK, 1), hz.reshape(DB, HG_HEADS, HG_DK, 1),
                               hv.reshape(DB, HG_HEADS, HG_DV), hog.reshape(DB, HG_HEADS, HG_DV),
                               lb.reshape(HG_HEADS, HG_DK, 1), gn, state_hgrn.reshape(state_hgrn.shape[1:]))
    h1s = _merge_out(oas.reshape(DB, NSA_HEADS * DH), ohs.reshape(DB, HG_HEADS * HG_DV), q, w, xs, gt1[ss])
    u2s = norm_mod(h1s, g2, sc2[ss], sh2[ss])
    Fp = w["up_a"].shape[1]
    a_s = mm(u2s, w["up_a"], name="ffn_s_a")
    conv_buf = state_conv.reshape(state_conv.shape[1:])
    buf = jnp.pad(conv_buf, ((0, 0), (0, 0), (0, Fp - F)))

    def conv_gate(v, a, b0, b1, cw0, cw1, cw2, cb):
        conv = cb + b0 * cw0 + b1 * cw1 + a * cw2
        return _silu(conv) * v
    hid = mm(u2s, w["up_v"], extras=[(a_s, "tile"), (buf[:, 0], "tile"), (buf[:, 1], "tile")]
             + [(w["conv"][k:k + 1], "row") for k in range(4)], epilogue=conv_gate, out_dtypes=(BF16,), name="ffn_s_v")

    def resid_norm(a, h, g, gfin):
        h2 = h + g * a
        return h2 * lax.rsqrt(jnp.mean(h2 * h2, axis=-1, keepdims=True) + EPS) * gfin
    y_s = mm(hid, w["down"], extras=[(h1s, "tile"), (gt2[ss], "tile"), (gf, "row")], epilogue=resid_norm, tn=D, name="ffn_s_down")

    kv_new_s = q["kv4"].reshape(1, DB, 1, 4, NSA_GROUPS, DH)
    win_new_s = jnp.concatenate([cache_win[:, :, 1:], q["win"].reshape(1, DB, 1, 2, NSA_GROUPS, DH)], axis=2)
    conv_new_s = jnp.stack([conv_buf[:, 1], a_s[:, :F]], axis=1)[None]
    return (y_p[None], y_s[:, None, :], kv_new_p, win_new_p, s_new_p[None, None], conv_new_p,
            kv_new_s, win_new_s, s_new_s[None], conv_new_s)
```

```python
import functools
import math

import numpy as np
import jax
import jax.numpy as jnp
from jax import lax
from jax.experimental import pallas as pl
from jax.experimental.pallas import tpu as pltpu

F32 = jnp.float32
BF16 = jnp.bfloat16

NSA_HEADS = 16
NSA_GROUPS = 4
HPG = NSA_HEADS // NSA_GROUPS
DH = 128
CMP_BLOCK = 32
CMP_STRIDE = 16
SEL_BLOCK = 64
N_SELECT = 16
WINDOW = 512
SEL_BONUS = 1.0e6
HG_HEADS = 8
HG_DK = 128
HG_DV = 128
CONV_W = 3
EPS = 1e-6
PAGE = 128

NEG = -1.0e30
LOG2E = math.log2(math.e)
V7X_VMEM_LIMIT = 56 * 1024 * 1024
RESIDENT_W_BYTES = 24 * 1024 * 1024
ROW_TILE_ELEMS = 1 << 20
CHUNKS_PER_PAGE = PAGE // CMP_STRIDE
HG_SUB = 16


def _params(sem):
    return pltpu.CompilerParams(dimension_semantics=sem, vmem_limit_bytes=V7X_VMEM_LIMIT)


def _pick(dim, target, mult):
    if dim <= target:
        return dim
    t = (target // mult) * mult
    while t >= mult:
        if dim % t == 0:
            return t
        t -= mult
    return dim


def _dot(a, b):
    return jnp.dot(a, b, preferred_element_type=F32)


def _dot_nt(a, b):
    return lax.dot_general(a, b, (((1,), (1,)), ((), ())), preferred_element_type=F32)


def _sigmoid(x):
    return 1.0 / (1.0 + jnp.exp(-x))


def _silu(x):
    return x * _sigmoid(x)


def _split3(x):
    a = x.astype(BF16)
    r = x - a.astype(F32)
    b = r.astype(BF16)
    c = (r - b.astype(F32)).astype(BF16)
    return a, b, c


def _mm_body(*refs, n_extra, act, epilogue):
    x_ref, w_ref = refs[0], refs[1]
    extras = refs[2:2 + n_extra]
    outs = refs[2 + n_extra:]
    x = x_ref[...]
    if act is not None:
        x = act(x.astype(F32))
    acc = _dot(x.astype(BF16), w_ref[...].astype(BF16))
    res = epilogue(acc, *[e[...] for e in extras])
    if not isinstance(res, tuple):
        res = (res,)
    for o, r in zip(outs, res):
        o[...] = r.astype(o.dtype)


def mm(x, w, extras=(), epilogue=lambda a: a, out_dtypes=(F32,), act=None, tm=None, tn=None, name="mm"):
    M, K = x.shape
    N = w.shape[1]
    if tn is None:
        tn = N if K * N * w.dtype.itemsize <= RESIDENT_W_BYTES else 512
    tn = _pick(N, tn, 128)
    if tm is None:
        tm = 1024 if tn < N else max(8, min(1024, ROW_TILE_ELEMS // N))
    tm = _pick(M, tm, 8)
    if tn == N:
        w_spec = pl.BlockSpec((K, N), lambda i, j: (0, 0), pipeline_mode=pl.Buffered(1))
    else:
        w_spec = pl.BlockSpec((K, tn), lambda i, j: (0, j))
    in_specs = [pl.BlockSpec((tm, K), lambda i, j: (i, 0)), w_spec]
    args = [x, w]
    for arr, kind in extras:
        if kind == "tile":
            in_specs.append(pl.BlockSpec((tm, tn), lambda i, j: (i, j)))
        else:
            in_specs.append(pl.BlockSpec((1, tn), lambda i, j: (0, j)))
        args.append(arr)
    outs = pl.pallas_call(
        functools.partial(_mm_body, n_extra=len(extras), act=act, epilogue=epilogue),
        grid=(M // tm, N // tn),
        in_specs=in_specs,
        out_specs=[pl.BlockSpec((tm, tn), lambda i, j: (i, j)) for _ in out_dtypes],
        out_shape=[jax.ShapeDtypeStruct((M, N), d) for d in out_dtypes],
        compiler_params=_params(("parallel", "parallel")),
        name=name,
    )(*args)
    return outs if len(outs) > 1 else outs[0]


def _norm_body(x_ref, g_ref, sc_ref, sh_ref, o_ref):
    x = x_ref[...].astype(F32)
    y = x * lax.rsqrt(jnp.mean(x * x, axis=-1, keepdims=True) + EPS) * g_ref[...]
    o_ref[...] = (y * (1.0 + sc_ref[...]) + sh_ref[...]).astype(o_ref.dtype)


def norm_mod(x, g, sc, sh, out_dtype=BF16):
    M, D = x.shape
    tm = _pick(M, 256, 8)
    per_row = sc.shape[0] == M and M > 1
    mod_spec = pl.BlockSpec((tm, D), lambda i: (i, 0)) if per_row else pl.BlockSpec((1, D), lambda i: (0, 0))
    return pl.pallas_call(
        _norm_body,
        grid=(M // tm,),
        in_specs=[pl.BlockSpec((tm, D), lambda i: (i, 0)), pl.BlockSpec((1, D), lambda i: (0, 0)), mod_spec, mod_spec],
        out_specs=pl.BlockSpec((tm, D), lambda i: (i, 0)),
        out_shape=jax.ShapeDtypeStruct((M, D), out_dtype),
        compiler_params=_params(("parallel",)),
        name="norm_mod",
    )(x, g, sc, sh)


def _cmpz_body(pt_ref, *refs, P, flat):
    pages = refs[:P]
    wk_ref, wv_ref, zk_ref, zv_ref = refs[P:P + 4]
    nrow = 2 * NSA_GROUPS
    if flat:
        nc = CHUNKS_PER_PAGE
        ri = lax.broadcasted_iota(jnp.int32, (nc * nrow, nc * nrow), 0)
        ci = lax.broadcasted_iota(jnp.int32, (nc * nrow, nc * nrow), 1)
        swap = (ci == (ri % nc) * nrow + ri // nc).astype(BF16)
        rp = []
        for p in range(CMP_STRIDE):
            a = jnp.concatenate([jnp.concatenate([pages[k][CMP_STRIDE * c + p] for c in range(nc)], axis=0)
                                 for k in range(P)], axis=1)
            rp.append(_dot(swap, a.astype(BF16)))

        def piece(k, p, kind, g):
            j = kind * NSA_GROUPS + g
            return rp[p][j * nc:(j + 1) * nc, k * DH:(k + 1) * DH]
    else:
        ri = lax.broadcasted_iota(jnp.int32, (PAGE, PAGE), 0)
        ci = lax.broadcasted_iota(jnp.int32, (PAGE, PAGE), 1)
        perm = (ci == (ri % CHUNKS_PER_PAGE) * CMP_STRIDE + ri // CHUNKS_PER_PAGE).astype(BF16)
        xp = [_dot(perm, pages[k][...].astype(BF16)) for k in range(P)]

        def piece(k, p, kind, g):
            col = (kind * NSA_GROUPS + g) * DH
            return xp[k][p * CHUNKS_PER_PAGE:(p + 1) * CHUNKS_PER_PAGE, col:col + DH]
    for kind, (w_ref, z_ref) in enumerate(((wk_ref, zk_ref), (wv_ref, zv_ref))):
        rows = []
        for k in range(P):
            for g in range(NSA_GROUPS):
                rows.append(jnp.concatenate([piece(k, p, kind, g) for p in range(CMP_STRIDE)], axis=1))
        y = jnp.concatenate(rows, axis=0).astype(BF16)
        z = _dot(y, w_ref[...])
        z_ref[...] = z.reshape(1, P, NSA_GROUPS, CHUNKS_PER_PAGE, 2 * DH)


def cmpz(rows, page_tab, wk_ab, wv_ab):
    B, n_pages = page_tab.shape
    P = _pick(n_pages, 16, 1)
    flat = rows.ndim == 4
    if flat:
        def page_spec(k):
            return pl.BlockSpec((None, PAGE, 2 * NSA_GROUPS, DH),
                                lambda b, j, pt: (pt[b * n_pages + j * P + k], 0, 0, 0))
    else:
        def page_spec(k):
            return pl.BlockSpec((None, PAGE, 2 * NSA_GROUPS * DH), lambda b, j, pt: (pt[b * n_pages + j * P + k], 0, 0))

    grid_spec = pltpu.PrefetchScalarGridSpec(
        num_scalar_prefetch=1,
        grid=(B, n_pages // P),
        in_specs=[page_spec(k) for k in range(P)]
        + [pl.BlockSpec((CMP_STRIDE * DH, 2 * DH), lambda b, j, pt: (0, 0))] * 2,
        out_specs=[pl.BlockSpec((1, P, NSA_GROUPS, CHUNKS_PER_PAGE, 2 * DH), lambda b, j, pt: (b, j, 0, 0, 0))] * 2,
    )
    zshape = jax.ShapeDtypeStruct((B, n_pages, NSA_GROUPS, CHUNKS_PER_PAGE, 2 * DH), F32)
    return pl.pallas_call(
        functools.partial(_cmpz_body, P=P, flat=flat),
        grid_spec=grid_spec,
        out_shape=[zshape, zshape],
        compiler_params=_params(("parallel", "parallel")),
        name="nsa_cmpz",
    )(page_tab.reshape(-1), *([rows] * P), wk_ab, wv_ab)


def _cmpfin_body(z_ref, c_ref, w2_ref, o_ref, *, n_chunk):
    z = z_ref[0, :, 0].reshape(n_chunk, 2 * DH)
    nxt = pltpu.roll(z[:, DH:], n_chunk - 1, axis=0)
    pre = z[:, :DH] + nxt + c_ref[0:1, :]
    h = jax.nn.gelu(pre)
    o_ref[0, 0] = _dot(h.astype(BF16), w2_ref[...]).astype(o_ref.dtype)


def cmp_finish(z, c_row8, w2):
    B, n_pages = z.shape[0], z.shape[1]
    n_chunk = n_pages * CHUNKS_PER_PAGE
    return pl.pallas_call(
        functools.partial(_cmpfin_body, n_chunk=n_chunk),
        grid=(B, NSA_GROUPS),
        in_specs=[pl.BlockSpec((1, n_pages, 1, CHUNKS_PER_PAGE, 2 * DH), lambda b, g: (b, 0, g, 0, 0)),
                  pl.BlockSpec((8, DH), lambda b, g: (0, 0)),
                  pl.BlockSpec((DH, DH), lambda b, g: (0, 0))],
        out_specs=pl.BlockSpec((1, 1, n_chunk, DH), lambda b, g: (b, g, 0, 0)),
        out_shape=jax.ShapeDtypeStruct((B, NSA_GROUPS, n_chunk, DH), BF16),
        compiler_params=_params(("parallel", "parallel")),
        name="nsa_cmpfin",
    )(z, c_row8, w2)


def _nsacmp_body(sl_ref, q_ref, kc_ref, vc_ref, agg_ref, oc_ref, mask_ref, any_ref, imp_ref,
                 *, TQ, NC, NS, q0, top, n_real):
    g = pl.program_id(1)
    i = pl.program_id(2)
    t0 = q0 + i * TQ
    t = t0 + lax.broadcasted_iota(jnp.int32, (TQ, 1), 0)

    def scores(ncv):
        ce = lax.broadcasted_iota(jnp.int32, (1, ncv), 1) * CMP_STRIDE + (CMP_BLOCK - 1)
        d = (t - ce).astype(F32)
        valid = d >= 0.0
        kc = kc_ref[0, 0, :ncv, :]
        vc = vc_ref[0, 0, :ncv, :]
        psum = jnp.zeros((TQ, ncv), F32)
        for r in range(HPG):
            s = _dot_nt(q_ref[0, :, r * DH:(r + 1) * DH], kc) - sl_ref[g * HPG + r] * d
            m = jnp.max(jnp.where(valid, s, NEG), axis=1, keepdims=True)
            p = jnp.where(valid, jnp.exp2(s - m), 0.0)
            p = p / jnp.maximum(jnp.sum(p, axis=1, keepdims=True), 1e-30)
            oc_ref[0, :, r * DH:(r + 1) * DH] = _dot(p.astype(BF16), vc)
            psum = psum + p
        ph = psum.astype(BF16)
        plo = (psum - ph.astype(F32)).astype(BF16)
        imp_ref[...] = _dot(ph, agg_ref[:ncv, :]) + _dot(plo, agg_ref[:ncv, :])

    nbk = min(4, NC // 128) if NC % 128 == 0 else 1
    share = NC // nbk
    n_vis = jnp.maximum(t0 + TQ - CMP_BLOCK, 0) // CMP_STRIDE + 1
    bucket = jnp.minimum((n_vis + share - 1) // share, nbk)
    for bk in range(1, nbk + 1):
        pl.when(bucket == bk)(functools.partial(scores, bk * share))
    imp = imp_ref[...]
    cur = t // SEL_BLOCK
    j = lax.broadcasted_iota(jnp.int32, (1, NS), 1)
    forced = (j == 0) | (j == cur) | (j == cur - 1)
    imp = jnp.where(j <= cur, imp + jnp.where(forced, SEL_BONUS, 0.0), -SEL_BONUS)
    if n_real is None:
        impT = imp.T
        io = lax.broadcasted_iota(jnp.int32, (NS, TQ), 0).astype(F32)
        selT = jnp.zeros((NS, TQ), F32)
        for _ in range(top):
            mx = jnp.max(impT, axis=0, keepdims=True)
            am = jnp.min(jnp.where(impT == mx, io, float(NS)), axis=0, keepdims=True)
            hit = io == am
            selT = jnp.where(hit, 1.0, selT)
            impT = jnp.where(hit, -3.0e38, impT)
        sel = selT.T
    else:
        colT = jnp.concatenate([imp, jnp.zeros((128 - TQ, NS), F32)], axis=0).T
        ii = lax.broadcasted_iota(jnp.int32, (NS, NS), 0)
        jj = lax.broadcasted_iota(jnp.int32, (NS, NS), 1)
        rowi = lax.broadcasted_iota(jnp.int32, (TQ, 1), 0)
        sel = jnp.zeros((TQ, NS), F32)
        for r in range(n_real):
            a = colT[:, r:r + 1]
            b = imp[r:r + 1, :]
            beats = (a > b) | ((a == b) & (ii < jj))
            rank = jnp.sum(jnp.where(beats, 1.0, 0.0), axis=0, keepdims=True)
            sel = jnp.where(rowi == r, jnp.where(rank < float(top), 1.0, 0.0), sel)
    mask_ref[0, 0] = sel.astype(mask_ref.dtype)
    any_ref[0, 0, 0] = jnp.broadcast_to(jnp.max(sel, axis=0, keepdims=True), (8, NS))


def nsa_cmp(q, kc, vc, agg, slopes, q0, TQ=128, n_real=None):
    B, Tq, _ = q.shape
    NC = kc.shape[2]
    NS = agg.shape[1]
    top = N_SELECT
    grid_spec = pltpu.PrefetchScalarGridSpec(
        num_scalar_prefetch=1,
        grid=(B, NSA_GROUPS, Tq // TQ),
        in_specs=[pl.BlockSpec((1, TQ, HPG * DH), lambda b, g, i, sl: (b, i, g)),
                  pl.BlockSpec((1, 1, NC, DH), lambda b, g, i, sl: (b, g, 0, 0)),
                  pl.BlockSpec((1, 1, NC, DH), lambda b, g, i, sl: (b, g, 0, 0)),
                  pl.BlockSpec((NC, NS), lambda b, g, i, sl: (0, 0))],
        out_specs=[pl.BlockSpec((1, TQ, HPG * DH), lambda b, g, i, sl: (b, i, g)),
                   pl.BlockSpec((1, 1, TQ, NS), lambda b, g, i, sl: (b, g, i, 0)),
                   pl.BlockSpec((1, 1, 1, 8, NS), lambda b, g, i, sl: (b, g, i, 0, 0))],
        scratch_shapes=[pltpu.VMEM((TQ, NS), F32)],
    )
    return pl.pallas_call(
        functools.partial(_nsacmp_body, TQ=TQ, NC=NC, NS=NS, q0=q0, top=top, n_real=n_real),
        grid_spec=grid_spec,
        out_shape=[jax.ShapeDtypeStruct((B, Tq, NSA_HEADS * DH), F32),
                   jax.ShapeDtypeStruct((B, NSA_GROUPS, Tq, NS), BF16),
                   jax.ShapeDtypeStruct((B, NSA_GROUPS, Tq // TQ, 8, NS), F32)],
        compiler_params=_params(("parallel", "parallel", "parallel")),
        name="nsa_cmp",
    )(slopes, q, kc, vc, agg)


def _nsawin_body(sl_ref, q_ref, k0_ref, k1_ref, k2_ref, v0_ref, v1_ref, v2_ref, o_ref, *, TQ):
    g = pl.program_id(0)
    qi = pl.program_id(1)
    t0 = qi * TQ
    tq = t0 + lax.broadcasted_iota(jnp.int32, (TQ, 1), 0)
    kp = t0 - 2 * TQ + lax.broadcasted_iota(jnp.int32, (1, 3 * TQ), 1)
    dist = tq - kp
    bias0 = jnp.where((dist >= 0) & (dist < WINDOW) & (kp >= 0), 0.0, NEG)
    kprel = (kp - t0).astype(F32)
    k = jnp.concatenate([k0_ref[...], k1_ref[...], k2_ref[...]], axis=0)
    v = jnp.concatenate([v0_ref[...], v1_ref[...], v2_ref[...]], axis=0)
    for r in range(HPG):
        s = _dot_nt(q_ref[:, r * DH:(r + 1) * DH], k) + (sl_ref[g * HPG + r] * kprel + bias0)
        p = jnp.exp2(s - jnp.max(s, axis=1, keepdims=True))
        o = _dot(p.astype(BF16), v) / jnp.sum(p, axis=1, keepdims=True)
        o_ref[:, r * DH:(r + 1) * DH] = o.astype(o_ref.dtype)


def nsa_win(q, winb, slopes, TQ=256):
    T = q.shape[0]
    assert WINDOW <= 2 * TQ and T % TQ == 0
    G4 = NSA_GROUPS

    def kmap(off, kind):
        return lambda g, i, sl: (jnp.maximum(i - off, 0), kind * G4 + g)

    grid_spec = pltpu.PrefetchScalarGridSpec(
        num_scalar_prefetch=1,
        grid=(NSA_GROUPS, T // TQ),
        in_specs=[pl.BlockSpec((TQ, HPG * DH), lambda g, i, sl: (i, g))]
        + [pl.BlockSpec((TQ, DH), kmap(off, 0)) for off in (2, 1, 0)]
        + [pl.BlockSpec((TQ, DH), kmap(off, 1)) for off in (2, 1, 0)],
        out_specs=pl.BlockSpec((TQ, HPG * DH), lambda g, i, sl: (i, g)),
    )
    return pl.pallas_call(
        functools.partial(_nsawin_body, TQ=TQ),
        grid_spec=grid_spec,
        out_shape=jax.ShapeDtypeStruct((T, NSA_HEADS * DH), F32),
        compiler_params=_params(("parallel", "parallel")),
        name="nsa_win",
    )(slopes, q, *([winb] * 6))


def _nsasel_body(qi_t, ki_t, fl_t, sl_ref, q_ref, k_ref, v_ref, mask_ref, e_ref, gate_ref, oc_ref, ow_ref, o_ref,
                 ms, ls, accs, *, TQ, TK, n_steps):
    g = pl.program_id(0)
    st = g * n_steps + pl.program_id(1)
    qi = qi_t[st]
    ki = ki_t[st]
    fl = fl_t[st]
    t0 = qi * TQ
    k0 = ki * TK

    @pl.when((fl & 4) == 0)
    def _step():
        @pl.when((fl & 1) != 0)
        def _init():
            ms[...] = jnp.full(ms.shape, NEG, F32)
            ls[...] = jnp.zeros(ls.shape, F32)
            accs[...] = jnp.zeros(accs.shape, F32)

        tq = t0 + lax.broadcasted_iota(jnp.int32, (TQ, 1), 0)
        kp = k0 + lax.broadcasted_iota(jnp.int32, (1, TK), 1)
        kprel = (kp - t0).astype(F32)
        sel = _dot(mask_ref[0], e_ref[...])
        bias0 = jnp.where((sel > 0.5) & (tq >= kp), 0.0, NEG)
        k = k_ref[...]
        v = v_ref[...]
        reps = TK // 128
        for r in range(HPG):
            s = _dot_nt(q_ref[:, r * DH:(r + 1) * DH], k) + (sl_ref[g * HPG + r] * kprel + bias0)
            m_old = ms[r]
            m_new = jnp.maximum(m_old, jnp.max(s, axis=1, keepdims=True))
            alpha = jnp.exp2(m_old - m_new)
            p = jnp.exp2(s - jnp.tile(m_new, (1, reps)))
            ls[r] = alpha * ls[r] + jnp.sum(p, axis=1, keepdims=True)
            accs[r] = alpha * accs[r] + _dot(p.astype(BF16), v)
            ms[r] = m_new

        @pl.when((fl & 2) != 0)
        def _finish():
            gt = gate_ref[...]
            for r in range(HPG):
                cs = slice(r * DH, (r + 1) * DH)
                o = (gt[:, 3 * r:3 * r + 1] * oc_ref[:, cs] + gt[:, 3 * r + 1:3 * r + 2] * (accs[r] / ls[r])
                     + gt[:, 3 * r + 2:3 * r + 3] * ow_ref[:, cs])
                o_ref[:, cs] = o.astype(o_ref.dtype)


def nsa_sel(q, kvb, mask, anyq, expand, gates, oc, ow, slopes, TQ=256, TK=512):
    T = q.shape[0]
    NS = mask.shape[2]
    G4 = NSA_GROUPS
    nq, nk = T // TQ, T // TK
    assert T % TK == 0 and T % TQ == 0 and NS == nk * (TK // SEL_BLOCK)
    last = (np.arange(nq) * TQ + TQ - 1) // TK
    causal = np.arange(nk)[None, :] <= last[:, None]
    n_steps = int(causal.sum())
    act = (anyq.reshape(G4, nq, -1, nk, TK // SEL_BLOCK) > 0.5).any(axis=(2, 4)) & jnp.asarray(causal)[None]
    flat = act.reshape(G4, nq * nk)
    cnt = flat.sum(axis=-1).astype(jnp.int32)
    order = jnp.argsort(jnp.logical_not(flat), axis=-1, stable=True)[:, :n_steps].astype(jnp.int32)
    skip = jnp.arange(n_steps, dtype=jnp.int32)[None, :] >= cnt[:, None]
    idx = jnp.where(skip, jnp.take_along_axis(order, (cnt - 1)[:, None], axis=1), order)
    qi_t, ki_t = idx // nk, idx % nk
    fl_t = ((ki_t == 0).astype(jnp.int32) | ((ki_t == jnp.asarray(last, jnp.int32)[qi_t]).astype(jnp.int32) << 1)
            | (skip.astype(jnp.int32) << 2))
    tabs = [x.reshape(-1).astype(jnp.int32) for x in (qi_t, ki_t, fl_t)]
    qblk = pl.BlockSpec((TQ, HPG * DH), lambda g, s, qi, ki, fl, sl: (qi[g * n_steps + s], g))
    grid_spec = pltpu.PrefetchScalarGridSpec(
        num_scalar_prefetch=4,
        grid=(NSA_GROUPS, jnp.max(cnt)),
        in_specs=[
            qblk,
            pl.BlockSpec((TK, DH), lambda g, s, qi, ki, fl, sl: (ki[g * n_steps + s], 2 * G4 + g)),
            pl.BlockSpec((TK, DH), lambda g, s, qi, ki, fl, sl: (ki[g * n_steps + s], 3 * G4 + g)),
            pl.BlockSpec((1, TQ, NS), lambda g, s, qi, ki, fl, sl: (g, qi[g * n_steps + s], 0)),
            pl.BlockSpec((NS, TK), lambda g, s, qi, ki, fl, sl: (0, ki[g * n_steps + s])),
            pl.BlockSpec((TQ, 128), lambda g, s, qi, ki, fl, sl: (qi[g * n_steps + s], g)),
            qblk,
            qblk,
        ],
        out_specs=qblk,
        scratch_shapes=[pltpu.VMEM((HPG, TQ, 128), F32), pltpu.VMEM((HPG, TQ, 128), F32),
                        pltpu.VMEM((HPG, TQ, DH), F32)],
    )
    return pl.pallas_call(
        functools.partial(_nsasel_body, TQ=TQ, TK=TK, n_steps=n_steps),
        grid_spec=grid_spec,
        out_shape=jax.ShapeDtypeStruct((T, NSA_HEADS * DH), BF16),
        compiler_params=_params(("parallel", "arbitrary")),
        name="nsa_sel",
    )(*tabs, slopes, q, kvb, kvb, mask, expand, gates, oc, ow)


def _rowsel(rowg, vals):
    out = vals[0]
    for g in range(1, NSA_GROUPS):
        out = jnp.where(rowg == g, vals[g], out)
    return out


def _nsasels_body(kpos_t, val_t, blk_t, q_ref, *refs, n_slot, sps, Wc, past):
    G4 = NSA_GROUPS
    kv_refs = refs[0:sps * G4]
    (win_ref, knew_ref, vnew_ref, kwnew_ref, vwnew_ref, sl_ref, gc_ref, gs_ref, gw_ref, oc_ref,
     o_ref, m_s, l_s, acc_s) = refs[sps * G4:]
    b = pl.program_id(0)
    s = pl.program_id(1)
    q = q_ref[0]
    rowg = lax.broadcasted_iota(jnp.int32, (NSA_HEADS, 1), 0) // HPG
    slope = sl_ref[...][:, 0:1]

    @pl.when(s == 0)
    def _init():
        m_s[...] = jnp.full(m_s.shape, NEG, F32)
        l_s[...] = jnp.zeros(l_s.shape, F32)
        acc_s[...] = jnp.zeros(acc_s.shape, F32)

    scs, oks = [], []
    m_old = m_s[...]
    m_new = m_old
    for u in range(sps):
        base = (b * NSA_GROUPS) * n_slot + s * sps + u
        sc = _rowsel(rowg, [_dot_nt(q, kv_refs[u * G4 + g][:, g, :].astype(BF16)) for g in range(G4)])
        kp0 = _rowsel(rowg, [kpos_t[base + g * n_slot] for g in range(NSA_GROUPS)])
        ok = _rowsel(rowg, [val_t[base + g * n_slot] for g in range(NSA_GROUPS)]) > 0
        dist = (past - kp0 - lax.broadcasted_iota(jnp.int32, (1, SEL_BLOCK), 1)).astype(F32)
        sc = jnp.where(ok, sc - slope * dist, NEG)
        m_new = jnp.maximum(m_new, jnp.max(sc, axis=1, keepdims=True))
        scs.append(sc)
        oks.append(ok)
    alpha = jnp.exp2(m_old - m_new)
    l_new = alpha * l_s[...]
    acc_new = alpha * acc_s[...]
    for u in range(sps):
        p = jnp.where(oks[u], jnp.exp2(scs[u] - m_new), 0.0)
        pb = p.astype(BF16)
        l_new = l_new + jnp.sum(p, axis=1, keepdims=True)
        acc_new = acc_new + _rowsel(rowg, [_dot(pb, kv_refs[u * G4 + g][:, G4 + g, :].astype(BF16)) for g in range(G4)])
    l_s[...] = l_new
    acc_s[...] = acc_new
    m_s[...] = m_new

    @pl.when(s == n_slot // sps - 1)
    def _finish():
        qf = q.astype(F32)
        sn = jnp.sum(qf * knew_ref[0], axis=1, keepdims=True)
        m1 = m_s[...]
        m2 = jnp.maximum(m1, sn)
        a2 = jnp.exp2(m1 - m2)
        pn = jnp.exp2(sn - m2)
        o_s = (a2 * acc_s[...] + pn * vnew_ref[0]) / (a2 * l_s[...] + pn)
        sw = _rowsel(rowg, [_dot_nt(q, win_ref[:, g, :].astype(BF16)) for g in range(G4)])
        dw = (Wc - lax.broadcasted_iota(jnp.int32, (1, Wc), 1)).astype(F32)
        okw = dw < float(WINDOW)
        sw = jnp.where(okw, sw - slope * dw, NEG)
        swn = jnp.sum(qf * kwnew_ref[0], axis=1, keepdims=True)
        mwin = jnp.maximum(jnp.max(sw, axis=1, keepdims=True), swn)
        pw = jnp.where(okw, jnp.exp2(sw - mwin), 0.0)
        pwn = jnp.exp2(swn - mwin)
        pwb = pw.astype(BF16)
        ow = _rowsel(rowg, [_dot(pwb, win_ref[:, G4 + g, :].astype(BF16)) for g in range(G4)])
        o_w = (ow + pwn * vwnew_ref[0]) / (jnp.sum(pw, axis=1, keepdims=True) + pwn)
        o = gc_ref[0] * oc_ref[0] + gs_ref[0] * o_s + gw_ref[0] * o_w
        o_ref[0] = o.astype(o_ref.dtype)


def nsa_sel_sample(q16, half_pages, blk, val, kpos, win_rows, knew, vnew, kwnew, vwnew, slope16, gc, gs, gw, oc16, past):
    DB = q16.shape[0]
    n_slot = blk.shape[0] // (DB * NSA_GROUPS)
    sps = _pick(n_slot, 4, 1)
    Wc = win_rows.shape[1]
    G4 = NSA_GROUPS

    def kvmap(u, g):
        return lambda b, s, kp, vl, bk: (bk[(b * G4 + g) * n_slot + s * sps + u], 0, 1, 0)

    head = lambda b, s, kp, vl, bk: (b, 0, 0)
    hspec = pl.BlockSpec((1, NSA_HEADS, DH), head)
    grid_spec = pltpu.PrefetchScalarGridSpec(
        num_scalar_prefetch=3,
        grid=(DB, n_slot // sps),
        in_specs=[hspec]
        + [pl.BlockSpec((None, SEL_BLOCK, 2 * G4, DH), kvmap(u, g)) for u in range(sps) for g in range(G4)]
        + [pl.BlockSpec((None, Wc, 2 * G4, DH), lambda b, s, kp, vl, bk: (b, 0, 0, 0))]
        + [hspec] * 4
        + [pl.BlockSpec((NSA_HEADS, DH), lambda b, s, kp, vl, bk: (0, 0))]
        + [hspec] * 4,
        out_specs=hspec,
        scratch_shapes=[pltpu.VMEM((NSA_HEADS, 1), F32), pltpu.VMEM((NSA_HEADS, 1), F32), pltpu.VMEM((NSA_HEADS, DH), F32)],
    )
    return pl.pallas_call(
        functools.partial(_nsasels_body, n_slot=n_slot, sps=sps, Wc=Wc, past=past),
        grid_spec=grid_spec,
        out_shape=jax.ShapeDtypeStruct((DB, NSA_HEADS, DH), BF16),
        compiler_params=_params(("parallel", "arbitrary")),
        name="nsa_sel_sample",
    )(kpos, val, blk, q16, *([half_pages] * (sps * G4)), win_rows, knew, vnew, kwnew, vwnew, slope16, gc, gs, gw, oc16)


def _hgrn_tile(z, qraw, v, og, lb, gn, st):
    TC = z.shape[0]
    C = HG_SUB
    nsub = TC // C
    q = _silu(qraw)
    logf = jnp.log(lb + (1.0 - lb) * _sigmoid(z))
    kk = (1.0 - lb) * _sigmoid(-z)
    tril = (lax.broadcasted_iota(jnp.int32, (TC, TC), 0) >= lax.broadcasted_iota(jnp.int32, (TC, TC), 1)).astype(BF16)
    a, b2, c2 = _split3(logf)
    G = _dot(tril, a) + _dot(tril, b2) + _dot(tril, c2)
    gl = G[TC - 1:TC, :]
    vb = v.astype(BF16)
    tok = lax.broadcasted_iota(jnp.int32, (TC, 1), 0)

    o = _dot_nt((q * jnp.exp(G)).astype(BF16), st.astype(BF16))
    kd = kk * jnp.exp(gl - G)
    upd = lax.dot_general(vb, kd.astype(BF16), (((0,), (0,)), ((), ())), preferred_element_type=F32)
    st_new = st * jnp.exp(gl) + upd

    offs = [jnp.zeros((C, HG_DV), F32)]
    for b in range(1, nsub):
        ref = G[b * C - 1:b * C, :]
        qb = q[b * C:(b + 1) * C, :] * jnp.exp(G[b * C:(b + 1) * C, :] - ref)
        kb = kk * jnp.exp(jnp.where(tok < b * C, ref - G, NEG))
        att = _dot_nt(qb.astype(BF16), kb.astype(BF16))
        offs.append(_dot(att.astype(BF16), vb))
    o = o + jnp.concatenate(offs, axis=0)

    def pick(x, s):
        return jnp.concatenate([jnp.broadcast_to(x[b * C + s:b * C + s + 1, :], (C, x.shape[1])) for b in range(nsub)], axis=0)

    sub = tok % C
    pieces = []
    for s in range(C):
        e = jnp.exp(jnp.where(sub >= s, G - pick(G, s), NEG))
        pieces.append((q * e * pick(kk, s)).astype(BF16))
    rs = _dot(jnp.concatenate(pieces, axis=0), jnp.ones((HG_DK, 128), BF16))
    for s in range(C):
        o = o + rs[s * TC:(s + 1) * TC, :] * pick(v, s)

    y = o * lax.rsqrt(jnp.mean(o * o, axis=-1, keepdims=True) + EPS) * gn * _silu(og)
    return y, st_new


def _hgrn_body(q_ref, f_ref, i_ref, og_ref, lb_ref, gn_ref, s0_ref, o_ref, sout_ref, st_ref, *, HB, nT):
    i = pl.program_id(1)

    @pl.when(i == 0)
    def _init():
        for h in range(HB):
            st_ref[h] = s0_ref[h].T

    for h in range(HB):
        cs = slice(h * HG_DK, (h + 1) * HG_DK)
        y, st_new = _hgrn_tile(f_ref[:, cs], q_ref[:, cs], i_ref[:, cs], og_ref[:, cs], lb_ref[:, cs], gn_ref[...],
                               st_ref[h])
        st_ref[h] = st_new
        o_ref[:, cs] = y.astype(o_ref.dtype)

    @pl.when(i == nT - 1)
    def _fin():
        for h in range(HB):
            sout_ref[h] = st_ref[h].T


def hgrn_prompt(hg, lb, gn, s0, TC=128, HB=4):
    T = hg.shape[0]
    H = HG_HEADS
    nT = T // TC
    nb = H // HB
    W = HB * HG_DK
    return pl.pallas_call(
        functools.partial(_hgrn_body, HB=HB, nT=nT),
        grid=(nb, nT),
        in_specs=[pl.BlockSpec((TC, W), lambda h, i: (i, h)),
                  pl.BlockSpec((TC, W), lambda h, i: (i, nb + h)),
                  pl.BlockSpec((TC, W), lambda h, i: (i, 2 * nb + h)),
                  pl.BlockSpec((TC, W), lambda h, i: (i, 3 * nb + h)),
                  pl.BlockSpec((1, W), lambda h, i: (0, h)),
                  pl.BlockSpec((1, HG_DV), lambda h, i: (0, 0)),
                  pl.BlockSpec((HB, HG_DK, HG_DV), lambda h, i: (h, 0, 0))],
        out_specs=[pl.BlockSpec((TC, W), lambda h, i: (i, h)),
                   pl.BlockSpec((HB, HG_DK, HG_DV), lambda h, i: (h, 0, 0))],
        out_shape=[jax.ShapeDtypeStruct((T, H * HG_DV), BF16), jax.ShapeDtypeStruct((H, HG_DK, HG_DV), F32)],
        scratch_shapes=[pltpu.VMEM((HB, HG_DV, HG_DK), F32)],
        compiler_params=_params(("parallel", "arbitrary")),
        name="hgrn_prompt",
    )(hg, hg, hg, hg, lb, gn, s0)


def _hgrns_body(qc_ref, zc_ref, v_ref, og_ref, lbc_ref, gn_ref, s_ref, o_ref, sout_ref):
    outs = []
    for h in range(HG_HEADS):
        z = zc_ref[0, h]
        lb = lbc_ref[h]
        f = lb + (1.0 - lb) * _sigmoid(z)
        kk = (1.0 - lb) * _sigmoid(-z)
        q = _silu(qc_ref[0, h])
        v = v_ref[0, h:h + 1, :]
        s_new = f * s_ref[0, h] + kk * v
        sout_ref[0, h] = s_new
        o = jnp.sum(q * s_new, axis=0, keepdims=True)
        og = og_ref[0, h:h + 1, :]
        outs.append(o * lax.rsqrt(jnp.mean(o * o, axis=-1, keepdims=True) + EPS) * gn_ref[...] * _silu(og))
    o_ref[0] = jnp.concatenate(outs, axis=0).astype(o_ref.dtype)


def hgrn_sample(qcol, zcol, v, og, lbcol, gn, s0):
    DB = v.shape[0]
    H = HG_HEADS
    col = pl.BlockSpec((1, H, HG_DK, 1), lambda b: (b, 0, 0, 0))
    rowb = pl.BlockSpec((1, H, HG_DV), lambda b: (b, 0, 0))
    st = pl.BlockSpec((1, H, HG_DK, HG_DV), lambda b: (b, 0, 0, 0))
    return pl.pallas_call(
        _hgrns_body,
        grid=(DB,),
        in_specs=[col, col, rowb, rowb, pl.BlockSpec((H, HG_DK, 1), lambda b: (0, 0, 0)),
                  pl.BlockSpec((1, HG_DV), lambda b: (0, 0)), st],
        out_specs=[rowb, st],
        out_shape=[jax.ShapeDtypeStruct((DB, H, HG_DV), BF16), jax.ShapeDtypeStruct((DB, H, HG_DK, HG_DV), F32)],
        compiler_params=_params(("parallel",)),
        name="hgrn_sample",
    )(qcol, zcol, v, og, lbcol, gn, s0)


def _ffn_body(u_ref, uh_ref, *refs, nf, tm, ks):
    wa_refs, wv_refs = refs[:ks], refs[ks:2 * ks]
    cp_ref, wd_ref, h_ref, gt_ref, gf_ref, y_ref, hid_ref = refs[2 * ks:]
    i = pl.program_id(0)
    f = pl.program_id(1)
    dk = u_ref.shape[1] // ks

    def up(x_ref, w_refs):
        acc = _dot(x_ref[:, :dk], w_refs[0][...])
        for c in range(1, ks):
            acc = acc + _dot(x_ref[:, c * dk:(c + 1) * dk], w_refs[c][...])
        return acc

    a = up(u_ref, wa_refs)
    v = up(u_ref, wv_refs)
    ah = up(uh_ref, wa_refs) * (i > 0).astype(F32)
    cp = cp_ref[...]
    rows = lax.broadcasted_iota(jnp.int32, (tm, 1), 0)
    a1 = jnp.where(rows == 0, ah[7:8, :], pltpu.roll(a, 1, axis=0))
    a2 = jnp.where(rows == 0, ah[6:7, :], jnp.where(rows == 1, ah[7:8, :], pltpu.roll(a, 2, axis=0)))
    conv = cp[3:4, :] + a2 * cp[0:1, :] + a1 * cp[1:2, :] + a * cp[2:3, :]
    hid_ref[f] = (_silu(conv) * v).astype(BF16)

    @pl.when(f == nf - 1)
    def _fin():
        hid = jnp.concatenate([hid_ref[k] for k in range(nf)], axis=1)
        h2 = h_ref[...] + gt_ref[...] * _dot(hid, wd_ref[...])
        y_ref[...] = h2 * lax.rsqrt(jnp.mean(h2 * h2, axis=-1, keepdims=True) + EPS) * gf_ref[...]


def ffn_prompt(u2, h1, wa, wv, cp, wd, gt2, gf, tm=512, tf=512):
    M, D = u2.shape
    Fp = wa.shape[1]
    tm = _pick(M, tm, 8)
    tf = _pick(Fp, tf, 128)
    nf = Fp // tf
    hb = tm // 8
    once = pl.Buffered(1)
    wa = wa.reshape(D, nf, tf).transpose(1, 0, 2)
    wv = wv.reshape(D, nf, tf).transpose(1, 0, 2)
    ks = 4
    dk = D // ks
    wspecs = [pl.BlockSpec((None, dk, tf), (lambda i, f, c=c: (f, c, 0))) for c in range(ks)]
    return pl.pallas_call(
        functools.partial(_ffn_body, nf=nf, tm=tm, ks=ks),
        grid=(M // tm, nf),
        in_specs=[pl.BlockSpec((tm, D), lambda i, f: (i, 0)),
                  pl.BlockSpec((8, D), lambda i, f: (jnp.maximum(i * hb - 1, 0), 0))]
        + wspecs + wspecs
        + [pl.BlockSpec((8, tf), lambda i, f: (0, f)),
                  pl.BlockSpec((Fp, D), lambda i, f: (0, 0), pipeline_mode=once),
                  pl.BlockSpec((tm, D), lambda i, f: (i, 0), pipeline_mode=once),
                  pl.BlockSpec((1, D), lambda i, f: (0, 0)),
                  pl.BlockSpec((1, D), lambda i, f: (0, 0))],
        out_specs=pl.BlockSpec((tm, D), lambda i, f: (i, 0), pipeline_mode=once),
        out_shape=jax.ShapeDtypeStruct((M, D), F32),
        scratch_shapes=[pltpu.VMEM((nf, tm, tf), BF16)],
        compiler_params=pltpu.CompilerParams(dimension_semantics=("parallel", "arbitrary"),
                                             vmem_limit_bytes=V7X_VMEM_LIMIT + 4 * 1024 * 1024),
        name="ffn_prompt",
    )(u2, u2, *([wa] * ks), *([wv] * ks), cp, wd, h1, gt2, gf)


def _alibi_slopes():
    h = np.arange(1, NSA_HEADS + 1, dtype=np.float32)
    return np.asarray(2.0 ** (-8.0 * h / NSA_HEADS), dtype=np.float32)


def _agg_matrix(n_chunk, n_cmp, n_sel, ns_pad):
    i = np.arange(n_chunk)[:, None] * CMP_STRIDE
    j = np.arange(ns_pad)[None, :] * SEL_BLOCK
    m = (i <= j + SEL_BLOCK - 1) & (i + CMP_BLOCK - 1 >= j)
    m &= (np.arange(n_chunk)[:, None] < n_cmp) & (np.arange(ns_pad)[None, :] < n_sel)
    return jnp.asarray(m.astype(np.float32), BF16)


def _layer_weights(w_in, w_ck1, w_cv1, w_br_a, w_br_b, w_out, w_up, conv_w, conv_b, w_down, w_ada):
    D = w_in.shape[0]
    nq, nkv = NSA_HEADS * DH, NSA_GROUPS * DH
    o = 0
    w = {}
    w["q"] = w_in[:, o:o + nq]; o += nq
    w["kv4"] = w_in[:, o:o + 4 * nkv]; o += 4 * nkv
    w["win"] = w_in[:, o:o + 2 * nkv]; o += 2 * nkv
    gate = w_in[:, o:o + 3 * NSA_HEADS]; o += 3 * NSA_HEADS
    gate = jnp.pad(gate.reshape(D, NSA_GROUPS, 3 * HPG), ((0, 0), (0, 0), (0, 128 - 3 * HPG)))
    w["gate"] = gate.reshape(D, NSA_GROUPS * 128)
    nh = 4 * HG_HEADS * HG_DK
    w["hg"] = w_in[:, o:o + nh]; o += nh
    w["ma"] = w_in[:, o:o + D]; o += D
    w["mb"] = w_in[:, o:o + D]; o += D
    half = CMP_STRIDE * DH
    w["ck_ab"] = jnp.concatenate([w_ck1[:half], w_ck1[half:]], axis=1)
    w["cv_ab"] = jnp.concatenate([w_cv1[:half], w_cv1[half:]], axis=1)
    w["br_a"], w["br_b"], w["out"], w["ada"] = w_br_a, w_br_b, w_out, w_ada
    F = w_down.shape[0]
    Fp = -(-F // 512) * 512
    w["up_a"] = jnp.pad(w_up[:, :F], ((0, 0), (0, Fp - F)))
    w["up_v"] = jnp.pad(w_up[:, F:], ((0, 0), (0, Fp - F)))
    w["down"] = jnp.pad(w_down, ((0, Fp - F), (0, 0)))
    w = {k: v.astype(BF16) for k, v in w.items()}
    cp = jnp.concatenate([conv_w, conv_b[None, :], jnp.zeros((8 - CONV_W - 1, F), F32)], axis=0)
    w["conv"] = jnp.pad(cp, ((0, 0), (0, Fp - F)))
    return w


def _project(u, w):
    scale = DH ** -0.5 * LOG2E
    p = {}
    p["q"] = mm(u, w["q"], epilogue=lambda a: a * scale, out_dtypes=(BF16,), name="proj_q")
    p["kv4"], p["kv4b"] = mm(u, w["kv4"], epilogue=lambda a: (a, a), out_dtypes=(F32, BF16), name="proj_kv")
    p["win"], p["winb"] = mm(u, w["win"], epilogue=lambda a: (a, a), out_dtypes=(F32, BF16), name="proj_win")
    p["gate"] = mm(u, w["gate"], epilogue=_sigmoid, name="proj_gate")
    p["hg"] = mm(u, w["hg"], name="proj_hg")
    p["ma"] = mm(u, w["ma"], epilogue=_sigmoid, out_dtypes=(BF16,), name="proj_ma")
    p["mb"] = mm(u, w["mb"], epilogue=_sigmoid, out_dtypes=(BF16,), name="proj_mb")
    return p


def _compressed_kv(rows3, page_tab, w, c_k, c_v, w_ck2, w_cv2):
    zk, zv = cmpz(rows3, page_tab, w["ck_ab"], w["cv_ab"])
    return cmp_finish(zk, c_k, w_ck2), cmp_finish(zv, c_v, w_cv2)


def _merge_out(oa, oh, p, w, x, gt1):
    gkind = "row" if gt1.shape[0] == 1 else "tile"
    a1 = mm(oa, w["br_a"], extras=[(p["ma"], "tile")], epilogue=lambda a, m: a * m.astype(F32),
            out_dtypes=(BF16,), name="branch_a")
    mix = mm(oh, w["br_b"], extras=[(p["mb"], "tile"), (a1, "tile")],
             epilogue=lambda a, m, prev: a * m.astype(F32) + prev.astype(F32), out_dtypes=(BF16,), name="branch_b")
    return mm(mix, w["out"], extras=[(x, "tile"), (gt1, gkind)], epilogue=lambda a, xr, g: xr + g * a, name="out_proj")


def kernel(x_prompt, x_sample, cache_kv, cache_win, state_hgrn, state_conv, page_table, c_prompt, c_sample, w_ada, b_ada, g_norm1, w_in, nsa_pos_k, nsa_pos_v, w_ck1, w_ck2, w_cv1, w_cv2, hg_lb_logits, hg_norm, w_br_a, w_br_b, w_out, g_norm2, w_up, conv_w, conv_b, w_down, g_final):
    B, T, D = x_prompt.shape
    DB = x_sample.shape[0]
    depth = w_in.shape[0]
    assert B == 1 and x_sample.shape[1] == 1 and depth == 1 and T % 512 == 0 and T >= N_SELECT * SEL_BLOCK
    n_pages = page_table.shape[1]
    past = n_pages * PAGE
    Wc = cache_win.shape[2]
    F = w_down.shape[1]
    slopes = jnp.asarray(_alibi_slopes() * np.float32(LOG2E))
    lower_bounds = jnp.cumsum(jax.nn.softmax(hg_lb_logits.astype(F32), axis=0), axis=0)
    layer = 0
    w = _layer_weights(w_in[layer], w_ck1[layer], w_cv1[layer], w_br_a[layer], w_br_b[layer], w_out[layer],
                       w_up[layer], conv_w[layer], conv_b[layer], w_down[layer], w_ada[layer])
    w_ck2b, w_cv2b = w_ck2[layer].astype(BF16), w_cv2[layer].astype(BF16)
    lb = lower_bounds[layer][None, :]
    gn = hg_norm[layer][None, :]
    g1, g2, gf = g_norm1[layer][None, :], g_norm2[layer][None, :], g_final[None, :]

    n_c = -(-(B + DB) // 8) * 8
    c_all = jnp.pad(jnp.concatenate([c_prompt, c_sample], axis=0), ((0, n_c - B - DB), (0, 0)))
    mod = mm(c_all, w["ada"], extras=[(b_ada[layer][None, :], "row")], epilogue=lambda a, b: a + b, act=_silu, name="adaln")
    sh1, sc1, gt1, sh2, sc2, gt2 = [mod[:, k * D:(k + 1) * D] for k in range(6)]
    ps, ss = slice(0, 1), slice(B, B + DB)

    def pos_term(pos, w1):
        return mm(jnp.pad(pos.reshape(1, -1), ((0, 7), (0, 0))), w1.astype(BF16), name="cmp_pos")
    c_k, c_v = pos_term(nsa_pos_k[layer], w_ck1[layer]), pos_term(nsa_pos_v[layer], w_cv1[layer])

    xp = x_prompt[0]
    u = norm_mod(xp, g1, sc1[ps], sh1[ps])
    p = _project(u, w)
    ident = jnp.arange(T // PAGE, dtype=jnp.int32)[None, :]
    kc, vc = _compressed_kv(p["kv4"].reshape(T // PAGE, PAGE, 4 * NSA_GROUPS * DH), ident, w, c_k, c_v, w_ck2b, w_cv2b)
    n_chunk = T // CMP_STRIDE
    n_sel = T // SEL_BLOCK
    agg = _agg_matrix(n_chunk, n_chunk - 1, n_sel, n_sel)
    oc, mask, anyq = nsa_cmp(p["q"][None], kc, vc, agg, slopes, q0=0)
    expand = jnp.asarray(np.kron(np.eye(n_sel, dtype=np.float32), np.ones((1, SEL_BLOCK), np.float32)), BF16)
    ow = nsa_win(p["q"], p["winb"], slopes)
    oa = nsa_sel(p["q"], p["kv4b"], mask[0], anyq[0, :, :, 0, :], expand, p["gate"], oc[0], ow, slopes)
    s0p = jnp.zeros((HG_HEADS, HG_DK, HG_DV), F32)
    oh, s_new_p = hgrn_prompt(p["hg"], lb, gn, s0p)
    h1 = _merge_out(oa, oh, p, w, xp, gt1[ps])
    u2 = norm_mod(h1, g2, sc2[ps], sh2[ps])
    y_p = ffn_prompt(u2, h1, w["up_a"], w["up_v"], w["conv"], w["down"], gt2[ps], gf)
    a_tail = mm(u2[T - 8:], w["up_a"], name="ffn_tail")
    kv_new_p = p["kv4"].reshape(1, 1, T, 4, NSA_GROUPS, DH)
    wk = min(WINDOW, T)
    win_new_p = p["win"][T - wk:].reshape(1, 1, wk, 2, NSA_GROUPS, DH)
    conv_new_p = a_tail[8 - (CONV_W - 1):, :F].reshape(1, 1, CONV_W - 1, F)

    xs = x_sample[:, 0]
    us = norm_mod(xs, g1, sc1[ss], sh1[ss])
    q = _project(us, w)
    n_row = 4 * NSA_GROUPS
    kcs, vcs = _compressed_kv(cache_kv.reshape(-1, PAGE, n_row, DH), page_table, w, c_k, c_v, w_ck2b, w_cv2b)
    n_chunk_s = past // CMP_STRIDE
    n_sel_s = past // SEL_BLOCK + 1
    ns_pad = -(-n_sel_s // 128) * 128
    agg_s = _agg_matrix(n_chunk_s, n_chunk_s - 1, n_sel_s, ns_pad)
    TQS = 8
    q_pad = jnp.pad(q["q"][:, None, :], ((0, 0), (0, TQS - 1), (0, 0)))
    ocs, mask_s, _ = nsa_cmp(q_pad, kcs, vcs, agg_s, slopes, q0=past, TQ=TQS, n_real=1)
    chosen = mask_s[:, :, 0, :] > 0.5
    slot = jnp.cumsum(chosen.astype(jnp.int32), axis=-1) - 1
    hit = chosen[..., None] & (slot[..., None] == jnp.arange(N_SELECT, dtype=jnp.int32))
    idx = jnp.sum(jnp.where(hit, jnp.arange(ns_pad, dtype=jnp.int32)[:, None], 0), axis=-2)
    n_past_blk = past // SEL_BLOCK
    jc = jnp.minimum(idx, n_past_blk - 1)
    per_page = PAGE // SEL_BLOCK
    pg = jnp.take_along_axis(page_table[:, None, :], jc // per_page, axis=2)
    blk = (pg * per_page + jc % per_page).astype(jnp.int32).reshape(-1)
    val = (idx < n_past_blk).astype(jnp.int32).reshape(-1)
    kpos = (jc * SEL_BLOCK).astype(jnp.int32).reshape(-1)

    def per_head(rows):
        return jnp.repeat(rows.reshape(DB, NSA_GROUPS, DH), HPG, axis=1)
    nk = NSA_GROUPS * DH
    gate3 = q["gate"].reshape(DB, NSA_GROUPS, 128)[:, :, :3 * HPG].reshape(DB, NSA_HEADS, 3)
    gb = [jnp.broadcast_to(gate3[:, :, k:k + 1], (DB, NSA_HEADS, 128)) for k in range(3)]
    slope16 = jnp.broadcast_to(slopes[:, None], (NSA_HEADS, DH))
    oas = nsa_sel_sample(
        q["q"].reshape(DB, NSA_HEADS, DH), cache_kv.reshape(-1, SEL_BLOCK, n_row, DH), blk, val, kpos,
        cache_win.reshape(DB, Wc, 2 * NSA_GROUPS, DH),
        per_head(q["kv4"][:, 2 * nk:3 * nk]), per_head(q["kv4"][:, 3 * nk:4 * nk]),
        per_head(q["win"][:, :nk]), per_head(q["win"][:, nk:]),
        slope16, gb[0], gb[1], gb[2], ocs[:, 0].reshape(DB, NSA_HEADS, DH), past)
    hq, hz, hv, hog = [q["hg"][:, k * HG_HEADS * HG_DK:(k + 1) * HG_HEADS * HG_DK] for k in range(4)]
    ohs, s_new_s = hgrn_sample(hq.reshape(DB, HG_HEADS, HG_DK, 1), hz.reshape(DB, HG_HEADS, HG_DK, 1),
                               hv.reshape(DB, HG_HEADS, HG_DV), hog.reshape(DB, HG_HEADS, HG_DV),
                               lb.reshape(HG_HEADS, HG_DK, 1), gn, state_hgrn.reshape(state_hgrn.shape[1:]))
    h1s = _merge_out(oas.reshape(DB, NSA_HEADS * DH), ohs.reshape(DB, HG_HEADS * HG_DV), q, w, xs, gt1[ss])
    u2s = norm_mod(h1s, g2, sc2[ss], sh2[ss])
    Fp = w["up_a"].shape[1]
    a_s = mm(u2s, w["up_a"], name="ffn_s_a")
    conv_buf = state_conv.reshape(state_conv.shape[1:])
    buf = jnp.pad(conv_buf, ((0, 0), (0, 0), (0, Fp - F)))

    def conv_gate(v, a, b0, b1, cw0, cw1, cw2, cb):
        conv = cb + b0 * cw0 + b1 * cw1 + a * cw2
        return _silu(conv) * v
    hid = mm(u2s, w["up_v"], extras=[(a_s, "tile"), (buf[:, 0], "tile"), (buf[:, 1], "tile")]
             + [(w["conv"][k:k + 1], "row") for k in range(4)], epilogue=conv_gate, out_dtypes=(BF16,), name="ffn_s_v")

    def resid_norm(a, h, g, gfin):
        h2 = h + g * a
        return h2 * lax.rsqrt(jnp.mean(h2 * h2, axis=-1, keepdims=True) + EPS) * gfin
    y_s = mm(hid, w["down"], extras=[(h1s, "tile"), (gt2[ss], "tile"), (gf, "row")], epilogue=resid_norm, tn=D, name="ffn_s_down")

    kv_new_s = q["kv4"].reshape(1, DB, 1, 4, NSA_GROUPS, DH)
    win_new_s = jnp.concatenate([cache_win[:, :, 1:], q["win"].reshape(1, DB, 1, 2, NSA_GROUPS, DH)], axis=2)
    conv_new_s = jnp.stack([conv_buf[:, 1], a_s[:, :F]], axis=1)[None]
    return (y_p[None], y_s[:, None, :], kv_new_p, win_new_p, s_new_p[None, None], conv_new_p,
            kv_new_s, win_new_s, s_new_s[None], conv_new_s)
```

```python
import functools
import math

import numpy as np
import jax
import jax.numpy as jnp
from jax import lax
from jax.experimental import pallas as pl
from jax.experimental.pallas import tpu as pltpu

F32 = jnp.float32
BF16 = jnp.bfloat16

NSA_HEADS = 16
NSA_GROUPS = 4
HPG = NSA_HEADS // NSA_GROUPS
DH = 128
CMP_BLOCK = 32
CMP_STRIDE = 16
SEL_BLOCK = 64
N_SELECT = 16
WINDOW = 512
SEL_BONUS = 1.0e6
HG_HEADS = 8
HG_DK = 128
HG_DV = 128
CONV_W = 3
EPS = 1e-6
PAGE = 128

NEG = -1.0e30
LOG2E = math.log2(math.e)
V7X_VMEM_LIMIT = 56 * 1024 * 1024
RESIDENT_W_BYTES = 24 * 1024 * 1024
ROW_TILE_ELEMS = 1 << 20
CHUNKS_PER_PAGE = PAGE // CMP_STRIDE
HG_SUB = 16


def _params(sem):
    return pltpu.CompilerParams(dimension_semantics=sem, vmem_limit_bytes=V7X_VMEM_LIMIT)


def _pick(dim, target, mult):
    if dim <= target:
        return dim
    t = (target // mult) * mult
    while t >= mult:
        if dim % t == 0:
            return t
        t -= mult
    return dim


def _dot(a, b):
    return jnp.dot(a, b, preferred_element_type=F32)


def _dot_nt(a, b):
    return lax.dot_general(a, b, (((1,), (1,)), ((), ())), preferred_element_type=F32)


def _sigmoid(x):
    return 1.0 / (1.0 + jnp.exp(-x))


def _silu(x):
    return x * _sigmoid(x)


def _split3(x):
    a = x.astype(BF16)
    r = x - a.astype(F32)
    b = r.astype(BF16)
    c = (r - b.astype(F32)).astype(BF16)
    return a, b, c


def _mm_body(*refs, n_extra, act, epilogue):
    x_ref, w_ref = refs[0], refs[1]
    extras = refs[2:2 + n_extra]
    outs = refs[2 + n_extra:]
    x = x_ref[...]
    if act is not None:
        x = act(x.astype(F32))
    acc = _dot(x.astype(BF16), w_ref[...].astype(BF16))
    res = epilogue(acc, *[e[...] for e in extras])
    if not isinstance(res, tuple):
        res = (res,)
    for o, r in zip(outs, res):
        o[...] = r.astype(o.dtype)


def mm(x, w, extras=(), epilogue=lambda a: a, out_dtypes=(F32,), act=None, tm=None, tn=None, name="mm"):
    M, K = x.shape
    N = w.shape[1]
    if tn is None:
        tn = N if K * N * w.dtype.itemsize <= RESIDENT_W_BYTES else 512
    tn = _pick(N, tn, 128)
    if tm is None:
        tm = 1024 if tn < N else max(8, min(1024, ROW_TILE_ELEMS // N))
    tm = _pick(M, tm, 8)
    if tn == N:
        w_spec = pl.BlockSpec((K, N), lambda i, j: (0, 0), pipeline_mode=pl.Buffered(1))
    else:
        w_spec = pl.BlockSpec((K, tn), lambda i, j: (0, j))
    in_specs = [pl.BlockSpec((tm, K), lambda i, j: (i, 0)), w_spec]
    args = [x, w]
    for arr, kind in extras:
        if kind == "tile":
            in_specs.append(pl.BlockSpec((tm, tn), lambda i, j: (i, j)))
        else:
            in_specs.append(pl.BlockSpec((1, tn), lambda i, j: (0, j)))
        args.append(arr)
    outs = pl.pallas_call(
        functools.partial(_mm_body, n_extra=len(extras), act=act, epilogue=epilogue),
        grid=(M // tm, N // tn),
        in_specs=in_specs,
        out_specs=[pl.BlockSpec((tm, tn), lambda i, j: (i, j)) for _ in out_dtypes],
        out_shape=[jax.ShapeDtypeStruct((M, N), d) for d in out_dtypes],
        compiler_params=_params(("parallel", "parallel")),
        name=name,
    )(*args)
    return outs if len(outs) > 1 else outs[0]


def _norm_body(x_ref, g_ref, sc_ref, sh_ref, o_ref):
    x = x_ref[...].astype(F32)
    y = x * lax.rsqrt(jnp.mean(x * x, axis=-1, keepdims=True) + EPS) * g_ref[...]
    o_ref[...] = (y * (1.0 + sc_ref[...]) + sh_ref[...]).astype(o_ref.dtype)


def norm_mod(x, g, sc, sh, out_dtype=BF16):
    M, D = x.shape
    tm = _pick(M, 256, 8)
    per_row = sc.shape[0] == M and M > 1
    mod_spec = pl.BlockSpec((tm, D), lambda i: (i, 0)) if per_row else pl.BlockSpec((1, D), lambda i: (0, 0))
    return pl.pallas_call(
        _norm_body,
        grid=(M // tm,),
        in_specs=[pl.BlockSpec((tm, D), lambda i: (i, 0)), pl.BlockSpec((1, D), lambda i: (0, 0)), mod_spec, mod_spec],
        out_specs=pl.BlockSpec((tm, D), lambda i: (i, 0)),
        out_shape=jax.ShapeDtypeStruct((M, D), out_dtype),
        compiler_params=_params(("parallel",)),
        name="norm_mod",
    )(x, g, sc, sh)


def _cmpz_body(pt_ref, *refs, P, flat):
    pages = refs[:P]
    wk_ref, wv_ref, zk_ref, zv_ref = refs[P:P + 4]
    nrow = 2 * NSA_GROUPS
    if flat:
        nc = CHUNKS_PER_PAGE
        ri = lax.broadcasted_iota(jnp.int32, (nc * nrow, nc * nrow), 0)
        ci = lax.broadcasted_iota(jnp.int32, (nc * nrow, nc * nrow), 1)
        swap = (ci == (ri % nc) * nrow + ri // nc).astype(BF16)
        rp = []
        for p in range(CMP_STRIDE):
            a = jnp.concatenate([jnp.concatenate([pages[k][CMP_STRIDE * c + p] for c in range(nc)], axis=0)
                                 for k in range(P)], axis=1)
            rp.append(_dot(swap, a.astype(BF16)))

        def piece(k, p, kind, g):
            j = kind * NSA_GROUPS + g
            return rp[p][j * nc:(j + 1) * nc, k * DH:(k + 1) * DH]
    else:
        ri = lax.broadcasted_iota(jnp.int32, (PAGE, PAGE), 0)
        ci = lax.broadcasted_iota(jnp.int32, (PAGE, PAGE), 1)
        perm = (ci == (ri % CHUNKS_PER_PAGE) * CMP_STRIDE + ri // CHUNKS_PER_PAGE).astype(BF16)
        xp = [_dot(perm, pages[k][...].astype(BF16)) for k in range(P)]

        def piece(k, p, kind, g):
            col = (kind * NSA_GROUPS + g) * DH
            return xp[k][p * CHUNKS_PER_PAGE:(p + 1) * CHUNKS_PER_PAGE, col:col + DH]
    for kind, (w_ref, z_ref) in enumerate(((wk_ref, zk_ref), (wv_ref, zv_ref))):
        rows = []
        for k in range(P):
            for g in range(NSA_GROUPS):
                rows.append(jnp.concatenate([piece(k, p, kind, g) for p in range(CMP_STRIDE)], axis=1))
        y = jnp.concatenate(rows, axis=0).astype(BF16)
        z = _dot(y, w_ref[...])
        z_ref[...] = z.reshape(1, P, NSA_GROUPS, CHUNKS_PER_PAGE, 2 * DH)


def cmpz(rows, page_tab, wk_ab, wv_ab):
    B, n_pages = page_tab.shape
    P = _pick(n_pages, 16, 1)
    flat = rows.ndim == 4
    if flat:
        def page_spec(k):
            return pl.BlockSpec((None, PAGE, 2 * NSA_GROUPS, DH),
                                lambda b, j, pt: (pt[b * n_pages + j * P + k], 0, 0, 0))
    else:
        def page_spec(k):
            return pl.BlockSpec((None, PAGE, 2 * NSA_GROUPS * DH), lambda b, j, pt: (pt[b * n_pages + j * P + k], 0, 0))

    grid_spec = pltpu.PrefetchScalarGridSpec(
        num_scalar_prefetch=1,
        grid=(B, n_pages // P),
        in_specs=[page_spec(k) for k in range(P)]
        + [pl.BlockSpec((CMP_STRIDE * DH, 2 * DH), lambda b, j, pt: (0, 0))] * 2,
        out_specs=[pl.BlockSpec((1, P, NSA_GROUPS, CHUNKS_PER_PAGE, 2 * DH), lambda b, j, pt: (b, j, 0, 0, 0))] * 2,
    )
    zshape = jax.ShapeDtypeStruct((B, n_pages, NSA_GROUPS, CHUNKS_PER_PAGE, 2 * DH), F32)
    return pl.pallas_call(
        functools.partial(_cmpz_body, P=P, flat=flat),
        grid_spec=grid_spec,
        out_shape=[zshape, zshape],
        compiler_params=_params(("parallel", "parallel")),
        name="nsa_cmpz",
    )(page_tab.reshape(-1), *([rows] * P), wk_ab, wv_ab)


def _cmpfin_body(z_ref, c_ref, w2_ref, o_ref, *, n_chunk):
    z = z_ref[0, :, 0].reshape(n_chunk, 2 * DH)
    nxt = pltpu.roll(z[:, DH:], n_chunk - 1, axis=0)
    pre = z[:, :DH] + nxt + c_ref[0:1, :]
    h = jax.nn.gelu(pre)
    o_ref[0, 0] = _dot(h.astype(BF16), w2_ref[...]).astype(o_ref.dtype)


def cmp_finish(z, c_row8, w2):
    B, n_pages = z.shape[0], z.shape[1]
    n_chunk = n_pages * CHUNKS_PER_PAGE
    return pl.pallas_call(
        functools.partial(_cmpfin_body, n_chunk=n_chunk),
        grid=(B, NSA_GROUPS),
        in_specs=[pl.BlockSpec((1, n_pages, 1, CHUNKS_PER_PAGE, 2 * DH), lambda b, g: (b, 0, g, 0, 0)),
                  pl.BlockSpec((8, DH), lambda b, g: (0, 0)),
                  pl.BlockSpec((DH, DH), lambda b, g: (0, 0))],
        out_specs=pl.BlockSpec((1, 1, n_chunk, DH), lambda b, g: (b, g, 0, 0)),
        out_shape=jax.ShapeDtypeStruct((B, NSA_GROUPS, n_chunk, DH), BF16),
        compiler_params=_params(("parallel", "parallel")),
        name="nsa_cmpfin",
    )(z, c_row8, w2)


def _nsacmp_body(sl_ref, q_ref, kc_ref, vc_ref, agg_ref, oc_ref, mask_ref, any_ref, imp_ref,
                 *, TQ, NC, NS, q0, top, n_real):
    g = pl.program_id(1)
    i = pl.program_id(2)
    t0 = q0 + i * TQ
    t = t0 + lax.broadcasted_iota(jnp.int32, (TQ, 1), 0)

    def scores(ncv):
        ce = lax.broadcasted_iota(jnp.int32, (1, ncv), 1) * CMP_STRIDE + (CMP_BLOCK - 1)
        d = (t - ce).astype(F32)
        valid = d >= 0.0
        kc = kc_ref[0, 0, :ncv, :]
        vc = vc_ref[0, 0, :ncv, :]
        psum = jnp.zeros((TQ, ncv), F32)
        for r in range(HPG):
            s = _dot_nt(q_ref[0, :, r * DH:(r + 1) * DH], kc) - sl_ref[g * HPG + r] * d
            m = jnp.max(jnp.where(valid, s, NEG), axis=1, keepdims=True)
            p = jnp.where(valid, jnp.exp2(s - m), 0.0)
            p = p / jnp.maximum(jnp.sum(p, axis=1, keepdims=True), 1e-30)
            oc_ref[0, :, r * DH:(r + 1) * DH] = _dot(p.astype(BF16), vc)
            psum = psum + p
        ph = psum.astype(BF16)
        plo = (psum - ph.astype(F32)).astype(BF16)
        imp_ref[...] = _dot(ph, agg_ref[:ncv, :]) + _dot(plo, agg_ref[:ncv, :])

    nbk = min(4, NC // 128) if NC % 128 == 0 else 1
    share = NC // nbk
    n_vis = jnp.maximum(t0 + TQ - CMP_BLOCK, 0) // CMP_STRIDE + 1
    bucket = jnp.minimum((n_vis + share - 1) // share, nbk)
    for bk in range(1, nbk + 1):
        pl.when(bucket == bk)(functools.partial(scores, bk * share))
    imp = imp_ref[...]
    cur = t // SEL_BLOCK
    j = lax.broadcasted_iota(jnp.int32, (1, NS), 1)
    forced = (j == 0) | (j == cur) | (j == cur - 1)
    imp = jnp.where(j <= cur, imp + jnp.where(forced, SEL_BONUS, 0.0), -SEL_BONUS)
    if n_real is None:
        impT = imp.T
        io = lax.broadcasted_iota(jnp.int32, (NS, TQ), 0).astype(F32)
        selT = jnp.zeros((NS, TQ), F32)
        for _ in range(top):
            mx = jnp.max(impT, axis=0, keepdims=True)
            am = jnp.min(jnp.where(impT == mx, io, float(NS)), axis=0, keepdims=True)
            hit = io == am
            selT = jnp.where(hit, 1.0, selT)
            impT = jnp.where(hit, -3.0e38, impT)
        sel = selT.T
    else:
        colT = jnp.concatenate([imp, jnp.zeros((128 - TQ, NS), F32)], axis=0).T
        ii = lax.broadcasted_iota(jnp.int32, (NS, NS), 0)
        jj = lax.broadcasted_iota(jnp.int32, (NS, NS), 1)
        rowi = lax.broadcasted_iota(jnp.int32, (TQ, 1), 0)
        sel = jnp.zeros((TQ, NS), F32)
        for r in range(n_real):
            a = colT[:, r:r + 1]
            b = imp[r:r + 1, :]
            beats = (a > b) | ((a == b) & (ii < jj))
            rank = jnp.sum(jnp.where(beats, 1.0, 0.0), axis=0, keepdims=True)
            sel = jnp.where(rowi == r, jnp.where(rank < float(top), 1.0, 0.0), sel)
    mask_ref[0, 0] = sel.astype(mask_ref.dtype)
    any_ref[0, 0, 0] = jnp.broadcast_to(jnp.max(sel, axis=0, keepdims=True), (8, NS))


def nsa_cmp(q, kc, vc, agg, slopes, q0, TQ=128, n_real=None):
    B, Tq, _ = q.shape
    NC = kc.shape[2]
    NS = agg.shape[1]
    top = N_SELECT
    grid_spec = pltpu.PrefetchScalarGridSpec(
        num_scalar_prefetch=1,
        grid=(B, NSA_GROUPS, Tq // TQ),
        in_specs=[pl.BlockSpec((1, TQ, HPG * DH), lambda b, g, i, sl: (b, i, g)),
                  pl.BlockSpec((1, 1, NC, DH), lambda b, g, i, sl: (b, g, 0, 0)),
                  pl.BlockSpec((1, 1, NC, DH), lambda b, g, i, sl: (b, g, 0, 0)),
                  pl.BlockSpec((NC, NS), lambda b, g, i, sl: (0, 0))],
        out_specs=[pl.BlockSpec((1, TQ, HPG * DH), lambda b, g, i, sl: (b, i, g)),
                   pl.BlockSpec((1, 1, TQ, NS), lambda b, g, i, sl: (b, g, i, 0)),
                   pl.BlockSpec((1, 1, 1, 8, NS), lambda b, g, i, sl: (b, g, i, 0, 0))],
        scratch_shapes=[pltpu.VMEM((TQ, NS), F32)],
    )
    return pl.pallas_call(
        functools.partial(_nsacmp_body, TQ=TQ, NC=NC, NS=NS, q0=q0, top=top, n_real=n_real),
        grid_spec=grid_spec,
        out_shape=[jax.ShapeDtypeStruct((B, Tq, NSA_HEADS * DH), F32),
                   jax.ShapeDtypeStruct((B, NSA_GROUPS, Tq, NS), BF16),
                   jax.ShapeDtypeStruct((B, NSA_GROUPS, Tq // TQ, 8, NS), F32)],
        compiler_params=_params(("parallel", "parallel", "parallel")),
        name="nsa_cmp",
    )(slopes, q, kc, vc, agg)


def _nsawin_body(sl_ref, q_ref, k0_ref, k1_ref, k2_ref, v0_ref, v1_ref, v2_ref, o_ref, *, TQ):
    g = pl.program_id(0)
    qi = pl.program_id(1)
    t0 = qi * TQ
    tq = t0 + lax.broadcasted_iota(jnp.int32, (TQ, 1), 0)
    kp = t0 - 2 * TQ + lax.broadcasted_iota(jnp.int32, (1, 3 * TQ), 1)
    dist = tq - kp
    bias0 = jnp.where((dist >= 0) & (dist < WINDOW) & (kp >= 0), 0.0, NEG)
    kprel = (kp - t0).astype(F32)
    k = jnp.concatenate([k0_ref[...], k1_ref[...], k2_ref[...]], axis=0)
    v = jnp.concatenate([v0_ref[...], v1_ref[...], v2_ref[...]], axis=0)
    for r in range(HPG):
        s = _dot_nt(q_ref[:, r * DH:(r + 1) * DH], k) + (sl_ref[g * HPG + r] * kprel + bias0)
        p = jnp.exp2(s - jnp.max(s, axis=1, keepdims=True))
        o = _dot(p.astype(BF16), v) / jnp.sum(p, axis=1, keepdims=True)
        o_ref[:, r * DH:(r + 1) * DH] = o.astype(o_ref.dtype)


def nsa_win(q, winb, slopes, TQ=256):
    T = q.shape[0]
    assert WINDOW <= 2 * TQ and T % TQ == 0
    G4 = NSA_GROUPS

    def kmap(off, kind):
        return lambda g, i, sl: (jnp.maximum(i - off, 0), kind * G4 + g)

    grid_spec = pltpu.PrefetchScalarGridSpec(
        num_scalar_prefetch=1,
        grid=(NSA_GROUPS, T // TQ),
        in_specs=[pl.BlockSpec((TQ, HPG * DH), lambda g, i, sl: (i, g))]
        + [pl.BlockSpec((TQ, DH), kmap(off, 0)) for off in (2, 1, 0)]
        + [pl.BlockSpec((TQ, DH), kmap(off, 1)) for off in (2, 1, 0)],
        out_specs=pl.BlockSpec((TQ, HPG * DH), lambda g, i, sl: (i, g)),
    )
    return pl.pallas_call(
        functools.partial(_nsawin_body, TQ=TQ),
        grid_spec=grid_spec,
        out_shape=jax.ShapeDtypeStruct((T, NSA_HEADS * DH), F32),
        compiler_params=_params(("parallel", "parallel")),
        name="nsa_win",
    )(slopes, q, *([winb] * 6))


def _nsasel_body(qi_t, ki_t, fl_t, sl_ref, q_ref, k_ref, v_ref, mask_ref, e_ref, gate_ref, oc_ref, ow_ref, o_ref,
                 ms, ls, accs, *, TQ, TK, n_steps):
    g = pl.program_id(0)
    st = g * n_steps + pl.program_id(1)
    qi = qi_t[st]
    ki = ki_t[st]
    fl = fl_t[st]
    t0 = qi * TQ
    k0 = ki * TK

    @pl.when((fl & 4) == 0)
    def _step():
        @pl.when((fl & 1) != 0)
        def _init():
            ms[...] = jnp.full(ms.shape, NEG, F32)
            ls[...] = jnp.zeros(ls.shape, F32)
            accs[...] = jnp.zeros(accs.shape, F32)

        tq = t0 + lax.broadcasted_iota(jnp.int32, (TQ, 1), 0)
        kp = k0 + lax.broadcasted_iota(jnp.int32, (1, TK), 1)
        kprel = (kp - t0).astype(F32)
        sel = _dot(mask_ref[0], e_ref[...])
        bias0 = jnp.where((sel > 0.5) & (tq >= kp), 0.0, NEG)
        k = k_ref[...]
        v = v_ref[...]
        reps = TK // 128
        for r in range(HPG):
            s = _dot_nt(q_ref[:, r * DH:(r + 1) * DH], k) + (sl_ref[g * HPG + r] * kprel + bias0)
            m_old = ms[r]
            m_new = jnp.maximum(m_old, jnp.max(s, axis=1, keepdims=True))
            alpha = jnp.exp2(m_old - m_new)
            p = jnp.exp2(s - jnp.tile(m_new, (1, reps)))
            ls[r] = alpha * ls[r] + jnp.sum(p, axis=1, keepdims=True)
            accs[r] = alpha * accs[r] + _dot(p.astype(BF16), v)
            ms[r] = m_new

        @pl.when((fl & 2) != 0)
        def _finish():
            gt = gate_ref[...]
            for r in range(HPG):
                cs = slice(r * DH, (r + 1) * DH)
                o = (gt[:, 3 * r:3 * r + 1] * oc_ref[:, cs] + gt[:, 3 * r + 1:3 * r + 2] * (accs[r] / ls[r])
                     + gt[:, 3 * r + 2:3 * r + 3] * ow_ref[:, cs])
                o_ref[:, cs] = o.astype(o_ref.dtype)


def nsa_sel(q, kvb, mask, anyq, expand, gates, oc, ow, slopes, TQ=256, TK=512):
    T = q.shape[0]
    NS = mask.shape[2]
    G4 = NSA_GROUPS
    nq, nk = T // TQ, T // TK
    assert T % TK == 0 and T % TQ == 0 and NS == nk * (TK // SEL_BLOCK)
    last = (np.arange(nq) * TQ + TQ - 1) // TK
    causal = np.arange(nk)[None, :] <= last[:, None]
    n_steps = int(causal.sum())
    act = (anyq.reshape(G4, nq, -1, nk, TK // SEL_BLOCK) > 0.5).any(axis=(2, 4)) & jnp.asarray(causal)[None]
    flat = act.reshape(G4, nq * nk)
    cnt = flat.sum(axis=-1).astype(jnp.int32)
    order = jnp.argsort(jnp.logical_not(flat), axis=-1, stable=True)[:, :n_steps].astype(jnp.int32)
    skip = jnp.arange(n_steps, dtype=jnp.int32)[None, :] >= cnt[:, None]
    idx = jnp.where(skip, jnp.take_along_axis(order, (cnt - 1)[:, None], axis=1), order)
    qi_t, ki_t = idx // nk, idx % nk
    fl_t = ((ki_t == 0).astype(jnp.int32) | ((ki_t == jnp.asarray(last, jnp.int32)[qi_t]).astype(jnp.int32) << 1)
            | (skip.astype(jnp.int32) << 2))
    tabs = [x.reshape(-1).astype(jnp.int32) for x in (qi_t, ki_t, fl_t)]
    qblk = pl.BlockSpec((TQ, HPG * DH), lambda g, s, qi, ki, fl, sl: (qi[g * n_steps + s], g))
    grid_spec = pltpu.PrefetchScalarGridSpec(
        num_scalar_prefetch=4,
        grid=(NSA_GROUPS, jnp.max(cnt)),
        in_specs=[
            qblk,
            pl.BlockSpec((TK, DH), lambda g, s, qi, ki, fl, sl: (ki[g * n_steps + s], 2 * G4 + g)),
            pl.BlockSpec((TK, DH), lambda g, s, qi, ki, fl, sl: (ki[g * n_steps + s], 3 * G4 + g)),
            pl.BlockSpec((1, TQ, NS), lambda g, s, qi, ki, fl, sl: (g, qi[g * n_steps + s], 0)),
            pl.BlockSpec((NS, TK), lambda g, s, qi, ki, fl, sl: (0, ki[g * n_steps + s])),
            pl.BlockSpec((TQ, 128), lambda g, s, qi, ki, fl, sl: (qi[g * n_steps + s], g)),
            qblk,
            qblk,
        ],
        out_specs=qblk,
        scratch_shapes=[pltpu.VMEM((HPG, TQ, 128), F32), pltpu.VMEM((HPG, TQ, 128), F32),
                        pltpu.VMEM((HPG, TQ, DH), F32)],
    )
    return pl.pallas_call(
        functools.partial(_nsasel_body, TQ=TQ, TK=TK, n_steps=n_steps),
        grid_spec=grid_spec,
        out_shape=jax.ShapeDtypeStruct((T, NSA_HEADS * DH), BF16),
        compiler_params=_params(("parallel", "arbitrary")),
        name="nsa_sel",
    )(*tabs, slopes, q, kvb, kvb, mask, expand, gates, oc, ow)


def _rowsel(rowg, vals):
    out = vals[0]
    for g in range(1, NSA_GROUPS):
        out = jnp.where(rowg == g, vals[g], out)
    return out


def _nsasels_body(kpos_t, val_t, blk_t, q_ref, *refs, n_slot, sps, Wc, past):
    G4 = NSA_GROUPS
    kv_refs = refs[0:sps * G4]
    (win_ref, knew_ref, vnew_ref, kwnew_ref, vwnew_ref, sl_ref, gc_ref, gs_ref, gw_ref, oc_ref,
     o_ref, m_s, l_s, acc_s) = refs[sps * G4:]
    b = pl.program_id(0)
    s = pl.program_id(1)
    q = q_ref[0]
    rowg = lax.broadcasted_iota(jnp.int32, (NSA_HEADS, 1), 0) // HPG
    slope = sl_ref[...][:, 0:1]

    @pl.when(s == 0)
    def _init():
        m_s[...] = jnp.full(m_s.shape, NEG, F32)
        l_s[...] = jnp.zeros(l_s.shape, F32)
        acc_s[...] = jnp.zeros(acc_s.shape, F32)

    scs, oks = [], []
    m_old = m_s[...]
    m_new = m_old
    for u in range(sps):
        base = (b * NSA_GROUPS) * n_slot + s * sps + u
        sc = _rowsel(rowg, [_dot_nt(q, kv_refs[u * G4 + g][:, g, :].astype(BF16)) for g in range(G4)])
        kp0 = _rowsel(rowg, [kpos_t[base + g * n_slot] for g in range(NSA_GROUPS)])
        ok = _rowsel(rowg, [val_t[base + g * n_slot] for g in range(NSA_GROUPS)]) > 0
        dist = (past - kp0 - lax.broadcasted_iota(jnp.int32, (1, SEL_BLOCK), 1)).astype(F32)
        sc = jnp.where(ok, sc - slope * dist, NEG)
        m_new = jnp.maximum(m_new, jnp.max(sc, axis=1, keepdims=True))
        scs.append(sc)
        oks.append(ok)
    alpha = jnp.exp2(m_old - m_new)
    l_new = alpha * l_s[...]
    acc_new = alpha * acc_s[...]
    for u in range(sps):
        p = jnp.where(oks[u], jnp.exp2(scs[u] - m_new), 0.0)
        pb = p.astype(BF16)
        l_new = l_new + jnp.sum(p, axis=1, keepdims=True)
        acc_new = acc_new + _rowsel(rowg, [_dot(pb, kv_refs[u * G4 + g][:, G4 + g, :].astype(BF16)) for g in range(G4)])
    l_s[...] = l_new
    acc_s[...] = acc_new
    m_s[...] = m_new

    @pl.when(s == n_slot // sps - 1)
    def _finish():
        qf = q.astype(F32)
        sn = jnp.sum(qf * knew_ref[0], axis=1, keepdims=True)
        m1 = m_s[...]
        m2 = jnp.maximum(m1, sn)
        a2 = jnp.exp2(m1 - m2)
        pn = jnp.exp2(sn - m2)
        o_s = (a2 * acc_s[...] + pn * vnew_ref[0]) / (a2 * l_s[...] + pn)
        sw = _rowsel(rowg, [_dot_nt(q, win_ref[:, g, :].astype(BF16)) for g in range(G4)])
        dw = (Wc - lax.broadcasted_iota(jnp.int32, (1, Wc), 1)).astype(F32)
        okw = dw < float(WINDOW)
        sw = jnp.where(okw, sw - slope * dw, NEG)
        swn = jnp.sum(qf * kwnew_ref[0], axis=1, keepdims=True)
        mwin = jnp.maximum(jnp.max(sw, axis=1, keepdims=True), swn)
        pw = jnp.where(okw, jnp.exp2(sw - mwin), 0.0)
        pwn = jnp.exp2(swn - mwin)
        pwb = pw.astype(BF16)
        ow = _rowsel(rowg, [_dot(pwb, win_ref[:, G4 + g, :].astype(BF16)) for g in range(G4)])
        o_w = (ow + pwn * vwnew_ref[0]) / (jnp.sum(pw, axis=1, keepdims=True) + pwn)
        o = gc_ref[0] * oc_ref[0] + gs_ref[0] * o_s + gw_ref[0] * o_w
        o_ref[0] = o.astype(o_ref.dtype)


def nsa_sel_sample(q16, half_pages, blk, val, kpos, win_rows, knew, vnew, kwnew, vwnew, slope16, gc, gs, gw, oc16, past):
    DB = q16.shape[0]
    n_slot = blk.shape[0] // (DB * NSA_GROUPS)
    sps = _pick(n_slot, 4, 1)
    Wc = win_rows.shape[1]
    G4 = NSA_GROUPS

    def kvmap(u, g):
        return lambda b, s, kp, vl, bk: (bk[(b * G4 + g) * n_slot + s * sps + u], 0, 1, 0)

    head = lambda b, s, kp, vl, bk: (b, 0, 0)
    hspec = pl.BlockSpec((1, NSA_HEADS, DH), head)
    grid_spec = pltpu.PrefetchScalarGridSpec(
        num_scalar_prefetch=3,
        grid=(DB, n_slot // sps),
        in_specs=[hspec]
        + [pl.BlockSpec((None, SEL_BLOCK, 2 * G4, DH), kvmap(u, g)) for u in range(sps) for g in range(G4)]
        + [pl.BlockSpec((None, Wc, 2 * G4, DH), lambda b, s, kp, vl, bk: (b, 0, 0, 0))]
        + [hspec] * 4
        + [pl.BlockSpec((NSA_HEADS, DH), lambda b, s, kp, vl, bk: (0, 0))]
        + [hspec] * 4,
        out_specs=hspec,
        scratch_shapes=[pltpu.VMEM((NSA_HEADS, 1), F32), pltpu.VMEM((NSA_HEADS, 1), F32), pltpu.VMEM((NSA_HEADS, DH), F32)],
    )
    return pl.pallas_call(
        functools.partial(_nsasels_body, n_slot=n_slot, sps=sps, Wc=Wc, past=past),
        grid_spec=grid_spec,
        out_shape=jax.ShapeDtypeStruct((DB, NSA_HEADS, DH), BF16),
        compiler_params=_params(("parallel", "arbitrary")),
        name="nsa_sel_sample",
    )(kpos, val, blk, q16, *([half_pages] * (sps * G4)), win_rows, knew, vnew, kwnew, vwnew, slope16, gc, gs, gw, oc16)


def _hgrn_tile(z, qraw, v, og, lb, gn, st):
    TC = z.shape[0]
    C = HG_SUB
    nsub = TC // C
    q = _silu(qraw)
    logf = jnp.log(lb + (1.0 - lb) * _sigmoid(z))
    kk = (1.0 - lb) * _sigmoid(-z)
    tril = (lax.broadcasted_iota(jnp.int32, (TC, TC), 0) >= lax.broadcasted_iota(jnp.int32, (TC, TC), 1)).astype(BF16)
    a, b2, c2 = _split3(logf)
    G = _dot(tril, a) + _dot(tril, b2) + _dot(tril, c2)
    gl = G[TC - 1:TC, :]
    vb = v.astype(BF16)
    tok = lax.broadcasted_iota(jnp.int32, (TC, 1), 0)

    o = _dot_nt((q * jnp.exp(G)).astype(BF16), st.astype(BF16))
    kd = kk * jnp.exp(gl - G)
    upd = lax.dot_general(vb, kd.astype(BF16), (((0,), (0,)), ((), ())), preferred_element_type=F32)
    st_new = st * jnp.exp(gl) + upd

    offs = [jnp.zeros((C, HG_DV), F32)]
    for b in range(1, nsub):
        ref = G[b * C - 1:b * C, :]
        qb = q[b * C:(b + 1) * C, :] * jnp.exp(G[b * C:(b + 1) * C, :] - ref)
        kb = kk * jnp.exp(jnp.where(tok < b * C, ref - G, NEG))
        att = _dot_nt(qb.astype(BF16), kb.astype(BF16))
        offs.append(_dot(att.astype(BF16), vb))
    o = o + jnp.concatenate(offs, axis=0)

    def pick(x, s):
        return jnp.concatenate([jnp.broadcast_to(x[b * C + s:b * C + s + 1, :], (C, x.shape[1])) for b in range(nsub)], axis=0)

    sub = tok % C
    pieces = []
    for s in range(C):
        e = jnp.exp(jnp.where(sub >= s, G - pick(G, s), NEG))
        pieces.append((q * e * pick(kk, s)).astype(BF16))
    rs = _dot(jnp.concatenate(pieces, axis=0), jnp.ones((HG_DK, 128), BF16))
    for s in range(C):
        o = o + rs[s * TC:(s + 1) * TC, :] * pick(v, s)

    y = o * lax.rsqrt(jnp.mean(o * o, axis=-1, keepdims=True) + EPS) * gn * _silu(og)
    return y, st_new


def _hgrn_body(q_ref, f_ref, i_ref, og_ref, lb_ref, gn_ref, s0_ref, o_ref, sout_ref, st_ref, *, HB, nT):
    i = pl.program_id(1)

    @pl.when(i == 0)
    def _init():
        for h in range(HB):
            st_ref[h] = s0_ref[h].T

    for h in range(HB):
        cs = slice(h * HG_DK, (h + 1) * HG_DK)
        y, st_new = _hgrn_tile(f_ref[:, cs], q_ref[:, cs], i_ref[:, cs], og_ref[:, cs], lb_ref[:, cs], gn_ref[...],
                               st_ref[h])
        st_ref[h] = st_new
        o_ref[:, cs] = y.astype(o_ref.dtype)

    @pl.when(i == nT - 1)
    def _fin():
        for h in range(HB):
            sout_ref[h] = st_ref[h].T


def hgrn_prompt(hg, lb, gn, s0, TC=64, HB=8):
    T = hg.shape[0]
    H = HG_HEADS
    nT = T // TC
    nb = H // HB
    W = HB * HG_DK
    return pl.pallas_call(
        functools.partial(_hgrn_body, HB=HB, nT=nT),
        grid=(nb, nT),
        in_specs=[pl.BlockSpec((TC, W), lambda h, i: (i, h)),
                  pl.BlockSpec((TC, W), lambda h, i: (i, nb + h)),
                  pl.BlockSpec((TC, W), lambda h, i: (i, 2 * nb + h)),
                  pl.BlockSpec((TC, W), lambda h, i: (i, 3 * nb + h)),
                  pl.BlockSpec((1, W), lambda h, i: (0, h)),
                  pl.BlockSpec((1, HG_DV), lambda h, i: (0, 0)),
                  pl.BlockSpec((HB, HG_DK, HG_DV), lambda h, i: (h, 0, 0))],
        out_specs=[pl.BlockSpec((TC, W), lambda h, i: (i, h)),
                   pl.BlockSpec((HB, HG_DK, HG_DV), lambda h, i: (h, 0, 0))],
        out_shape=[jax.ShapeDtypeStruct((T, H * HG_DV), BF16), jax.ShapeDtypeStruct((H, HG_DK, HG_DV), F32)],
        scratch_shapes=[pltpu.VMEM((HB, HG_DV, HG_DK), F32)],
        compiler_params=_params(("parallel", "arbitrary")),
        name="hgrn_prompt",
    )(hg, hg, hg, hg, lb, gn, s0)


def _hgrns_body(qc_ref, zc_ref, v_ref, og_ref, lbc_ref, gn_ref, s_ref, o_ref, sout_ref):
    outs = []
    for h in range(HG_HEADS):
        z = zc_ref[0, h]
        lb = lbc_ref[h]
        f = lb + (1.0 - lb) * _sigmoid(z)
        kk = (1.0 - lb) * _sigmoid(-z)
        q = _silu(qc_ref[0, h])
        v = v_ref[0, h:h + 1, :]
        s_new = f * s_ref[0, h] + kk * v
        sout_ref[0, h] = s_new
        o = jnp.sum(q * s_new, axis=0, keepdims=True)
        og = og_ref[0, h:h + 1, :]
        outs.append(o * lax.rsqrt(jnp.mean(o * o, axis=-1, keepdims=True) + EPS) * gn_ref[...] * _silu(og))
    o_ref[0] = jnp.concatenate(outs, axis=0).astype(o_ref.dtype)


def hgrn_sample(qcol, zcol, v, og, lbcol, gn, s0):
    DB = v.shape[0]
    H = HG_HEADS
    col = pl.BlockSpec((1, H, HG_DK, 1), lambda b: (b, 0, 0, 0))
    rowb = pl.BlockSpec((1, H, HG_DV), lambda b: (b, 0, 0))
    st = pl.BlockSpec((1, H, HG_DK, HG_DV), lambda b: (b, 0, 0, 0))
    return pl.pallas_call(
        _hgrns_body,
        grid=(DB,),
        in_specs=[col, col, rowb, rowb, pl.BlockSpec((H, HG_DK, 1), lambda b: (0, 0, 0)),
                  pl.BlockSpec((1, HG_DV), lambda b: (0, 0)), st],
        out_specs=[rowb, st],
        out_shape=[jax.ShapeDtypeStruct((DB, H, HG_DV), BF16), jax.ShapeDtypeStruct((DB, H, HG_DK, HG_DV), F32)],
        compiler_params=_params(("parallel",)),
        name="hgrn_sample",
    )(qcol, zcol, v, og, lbcol, gn, s0)


def _ffn_body(u_ref, uh_ref, *refs, nf, tm, ks):
    wa_refs, wv_refs = refs[:ks], refs[ks:2 * ks]
    cp_ref, wd_ref, h_ref, gt_ref, gf_ref, y_ref, hid_ref = refs[2 * ks:]
    i = pl.program_id(0)
    f = pl.program_id(1)
    dk = u_ref.shape[1] // ks

    def up(x_ref, w_refs):
        acc = _dot(x_ref[:, :dk], w_refs[0][...])
        for c in range(1, ks):
            acc = acc + _dot(x_ref[:, c * dk:(c + 1) * dk], w_refs[c][...])
        return acc

    a = up(u_ref, wa_refs)
    v = up(u_ref, wv_refs)
    ah = up(uh_ref, wa_refs) * (i > 0).astype(F32)
    cp = cp_ref[...]
    rows = lax.broadcasted_iota(jnp.int32, (tm, 1), 0)
    a1 = jnp.where(rows == 0, ah[7:8, :], pltpu.roll(a, 1, axis=0))
    a2 = jnp.where(rows == 0, ah[6:7, :], jnp.where(rows == 1, ah[7:8, :], pltpu.roll(a, 2, axis=0)))
    conv = cp[3:4, :] + a2 * cp[0:1, :] + a1 * cp[1:2, :] + a * cp[2:3, :]
    hid_ref[f] = (_silu(conv) * v).astype(BF16)

    @pl.when(f == nf - 1)
    def _fin():
        hid = jnp.concatenate([hid_ref[k] for k in range(nf)], axis=1)
        h2 = h_ref[...] + gt_ref[...] * _dot(hid, wd_ref[...])
        y_ref[...] = h2 * lax.rsqrt(jnp.mean(h2 * h2, axis=-1, keepdims=True) + EPS) * gf_ref[...]


def ffn_prompt(u2, h1, wa, wv, cp, wd, gt2, gf, tm=512, tf=512):
    M, D = u2.shape
    Fp = wa.shape[1]
    tm = _pick(M, tm, 8)
    tf = _pick(Fp, tf, 128)
    nf = Fp // tf
    hb = tm // 8
    once = pl.Buffered(1)
    wa = wa.reshape(D, nf, tf).transpose(1, 0, 2)
    wv = wv.reshape(D, nf, tf).transpose(1, 0, 2)
    ks = 4
    dk = D // ks
    wspecs = [pl.BlockSpec((None, dk, tf), (lambda i, f, c=c: (f, c, 0))) for c in range(ks)]
    return pl.pallas_call(
        functools.partial(_ffn_body, nf=nf, tm=tm, ks=ks),
        grid=(M // tm, nf),
        in_specs=[pl.BlockSpec((tm, D), lambda i, f: (i, 0)),
                  pl.BlockSpec((8, D), lambda i, f: (jnp.maximum(i * hb - 1, 0), 0))]
        + wspecs + wspecs
        + [pl.BlockSpec((8, tf), lambda i, f: (0, f)),
                  pl.BlockSpec((Fp, D), lambda i, f: (0, 0), pipeline_mode=once),
                  pl.BlockSpec((tm, D), lambda i, f: (i, 0), pipeline_mode=once),
                  pl.BlockSpec((1, D), lambda i, f: (0, 0)),
                  pl.BlockSpec((1, D), lambda i, f: (0, 0))],
        out_specs=pl.BlockSpec((tm, D), lambda i, f: (i, 0), pipeline_mode=once),
        out_shape=jax.ShapeDtypeStruct((M, D), F32),
        scratch_shapes=[pltpu.VMEM((nf, tm, tf), BF16)],
        compiler_params=pltpu.CompilerParams(dimension_semantics=("parallel", "arbitrary"),
                                             vmem_limit_bytes=V7X_VMEM_LIMIT + 4 * 1024 * 1024),
        name="ffn_prompt",
    )(u2, u2, *([wa] * ks), *([wv] * ks), cp, wd, h1, gt2, gf)


def _alibi_slopes():
    h = np.arange(1, NSA_HEADS + 1, dtype=np.float32)
    return np.asarray(2.0 ** (-8.0 * h / NSA_HEADS), dtype=np.float32)


def _agg_matrix(n_chunk, n_cmp, n_sel, ns_pad):
    i = np.arange(n_chunk)[:, None] * CMP_STRIDE
    j = np.arange(ns_pad)[None, :] * SEL_BLOCK
    m = (i <= j + SEL_BLOCK - 1) & (i + CMP_BLOCK - 1 >= j)
    m &= (np.arange(n_chunk)[:, None] < n_cmp) & (np.arange(ns_pad)[None, :] < n_sel)
    return jnp.asarray(m.astype(np.float32), BF16)


def _layer_weights(w_in, w_ck1, w_cv1, w_br_a, w_br_b, w_out, w_up, conv_w, conv_b, w_down, w_ada):
    D = w_in.shape[0]
    nq, nkv = NSA_HEADS * DH, NSA_GROUPS * DH
    o = 0
    w = {}
    w["q"] = w_in[:, o:o + nq]; o += nq
    w["kv4"] = w_in[:, o:o + 4 * nkv]; o += 4 * nkv
    w["win"] = w_in[:, o:o + 2 * nkv]; o += 2 * nkv
    gate = w_in[:, o:o + 3 * NSA_HEADS]; o += 3 * NSA_HEADS
    gate = jnp.pad(gate.reshape(D, NSA_GROUPS, 3 * HPG), ((0, 0), (0, 0), (0, 128 - 3 * HPG)))
    w["gate"] = gate.reshape(D, NSA_GROUPS * 128)
    nh = 4 * HG_HEADS * HG_DK
    w["hg"] = w_in[:, o:o + nh]; o += nh
    w["ma"] = w_in[:, o:o + D]; o += D
    w["mb"] = w_in[:, o:o + D]; o += D
    half = CMP_STRIDE * DH
    w["ck_ab"] = jnp.concatenate([w_ck1[:half], w_ck1[half:]], axis=1)
    w["cv_ab"] = jnp.concatenate([w_cv1[:half], w_cv1[half:]], axis=1)
    w["br_a"], w["br_b"], w["out"], w["ada"] = w_br_a, w_br_b, w_out, w_ada
    F = w_down.shape[0]
    Fp = -(-F // 512) * 512
    w["up_a"] = jnp.pad(w_up[:, :F], ((0, 0), (0, Fp - F)))
    w["up_v"] = jnp.pad(w_up[:, F:], ((0, 0), (0, Fp - F)))
    w["down"] = jnp.pad(w_down, ((0, Fp - F), (0, 0)))
    w = {k: v.astype(BF16) for k, v in w.items()}
    cp = jnp.concatenate([conv_w, conv_b[None, :], jnp.zeros((8 - CONV_W - 1, F), F32)], axis=0)
    w["conv"] = jnp.pad(cp, ((0, 0), (0, Fp - F)))
    return w


def _project(u, w):
    scale = DH ** -0.5 * LOG2E
    p = {}
    p["q"] = mm(u, w["q"], epilogue=lambda a: a * scale, out_dtypes=(BF16,), name="proj_q")
    p["kv4"], p["kv4b"] = mm(u, w["kv4"], epilogue=lambda a: (a, a), out_dtypes=(F32, BF16), name="proj_kv")
    p["win"], p["winb"] = mm(u, w["win"], epilogue=lambda a: (a, a), out_dtypes=(F32, BF16), name="proj_win")
    p["gate"] = mm(u, w["gate"], epilogue=_sigmoid, name="proj_gate")
    p["hg"] = mm(u, w["hg"], name="proj_hg")
    p["ma"] = mm(u, w["ma"], epilogue=_sigmoid, out_dtypes=(BF16,), name="proj_ma")
    p["mb"] = mm(u, w["mb"], epilogue=_sigmoid, out_dtypes=(BF16,), name="proj_mb")
    return p


def _compressed_kv(rows3, page_tab, w, c_k, c_v, w_ck2, w_cv2):
    zk, zv = cmpz(rows3, page_tab, w["ck_ab"], w["cv_ab"])
    return cmp_finish(zk, c_k, w_ck2), cmp_finish(zv, c_v, w_cv2)


def _merge_out(oa, oh, p, w, x, gt1):
    gkind = "row" if gt1.shape[0] == 1 else "tile"
    a1 = mm(oa, w["br_a"], extras=[(p["ma"], "tile")], epilogue=lambda a, m: a * m.astype(F32),
            out_dtypes=(BF16,), name="branch_a")
    mix = mm(oh, w["br_b"], extras=[(p["mb"], "tile"), (a1, "tile")],
             epilogue=lambda a, m, prev: a * m.astype(F32) + prev.astype(F32), out_dtypes=(BF16,), name="branch_b")
    return mm(mix, w["out"], extras=[(x, "tile"), (gt1, gkind)], epilogue=lambda a, xr, g: xr + g * a, name="out_proj")


def kernel(x_prompt, x_sample, cache_kv, cache_win, state_hgrn, state_conv, page_table, c_prompt, c_sample, w_ada, b_ada, g_norm1, w_in, nsa_pos_k, nsa_pos_v, w_ck1, w_ck2, w_cv1, w_cv2, hg_lb_logits, hg_norm, w_br_a, w_br_b, w_out, g_norm2, w_up, conv_w, conv_b, w_down, g_final):
    B, T, D = x_prompt.shape
    DB = x_sample.shape[0]
    depth = w_in.shape[0]
    assert B == 1 and x_sample.shape[1] == 1 and depth == 1 and T % 512 == 0 and T >= N_SELECT * SEL_BLOCK
    n_pages = page_table.shape[1]
    past = n_pages * PAGE
    Wc = cache_win.shape[2]
    F = w_down.shape[1]
    slopes = jnp.asarray(_alibi_slopes() * np.float32(LOG2E))
    lower_bounds = jnp.cumsum(jax.nn.softmax(hg_lb_logits.astype(F32), axis=0), axis=0)
    layer = 0
    w = _layer_weights(w_in[layer], w_ck1[layer], w_cv1[layer], w_br_a[layer], w_br_b[layer], w_out[layer],
                       w_up[layer], conv_w[layer], conv_b[layer], w_down[layer], w_ada[layer])
    w_ck2b, w_cv2b = w_ck2[layer].astype(BF16), w_cv2[layer].astype(BF16)
    lb = lower_bounds[layer][None, :]
    gn = hg_norm[layer][None, :]
    g1, g2, gf = g_norm1[layer][None, :], g_norm2[layer][None, :], g_final[None, :]

    n_c = -(-(B + DB) // 8) * 8
    c_all = jnp.pad(jnp.concatenate([c_prompt, c_sample], axis=0), ((0, n_c - B - DB), (0, 0)))
    mod = mm(c_all, w["ada"], extras=[(b_ada[layer][None, :], "row")], epilogue=lambda a, b: a + b, act=_silu, name="adaln")
    sh1, sc1, gt1, sh2, sc2, gt2 = [mod[:, k * D:(k + 1) * D] for k in range(6)]
    ps, ss = slice(0, 1), slice(B, B + DB)

    def pos_term(pos, w1):
        return mm(jnp.pad(pos.reshape(1, -1), ((0, 7), (0, 0))), w1.astype(BF16), name="cmp_pos")
    c_k, c_v = pos_term(nsa_pos_k[layer], w_ck1[layer]), pos_term(nsa_pos_v[layer], w_cv1[layer])

    xp = x_prompt[0]
    u = norm_mod(xp, g1, sc1[ps], sh1[ps])
    p = _project(u, w)
    ident = jnp.arange(T // PAGE, dtype=jnp.int32)[None, :]
    kc, vc = _compressed_kv(p["kv4"].reshape(T // PAGE, PAGE, 4 * NSA_GROUPS * DH), ident, w, c_k, c_v, w_ck2b, w_cv2b)
    n_chunk = T // CMP_STRIDE
    n_sel = T // SEL_BLOCK
    agg = _agg_matrix(n_chunk, n_chunk - 1, n_sel, n_sel)
    oc, mask, anyq = nsa_cmp(p["q"][None], kc, vc, agg, slopes, q0=0)
    expand = jnp.asarray(np.kron(np.eye(n_sel, dtype=np.float32), np.ones((1, SEL_BLOCK), np.float32)), BF16)
    ow = nsa_win(p["q"], p["winb"], slopes)
    oa = nsa_sel(p["q"], p["kv4b"], mask[0], anyq[0, :, :, 0, :], expand, p["gate"], oc[0], ow, slopes)
    s0p = jnp.zeros((HG_HEADS, HG_DK, HG_DV), F32)
    oh, s_new_p = hgrn_prompt(p["hg"], lb, gn, s0p)
    h1 = _merge_out(oa, oh, p, w, xp, gt1[ps])
    u2 = norm_mod(h1, g2, sc2[ps], sh2[ps])
    y_p = ffn_prompt(u2, h1, w["up_a"], w["up_v"], w["conv"], w["down"], gt2[ps], gf)
    a_tail = mm(u2[T - 8:], w["up_a"], name="ffn_tail")
    kv_new_p = p["kv4"].reshape(1, 1, T, 4, NSA_GROUPS, DH)
    wk = min(WINDOW, T)
    win_new_p = p["win"][T - wk:].reshape(1, 1, wk, 2, NSA_GROUPS, DH)
    conv_new_p = a_tail[8 - (CONV_W - 1):, :F].reshape(1, 1, CONV_W - 1, F)

    xs = x_sample[:, 0]
    us = norm_mod(xs, g1, sc1[ss], sh1[ss])
    q = _project(us, w)
    n_row = 4 * NSA_GROUPS
    kcs, vcs = _compressed_kv(cache_kv.reshape(-1, PAGE, n_row, DH), page_table, w, c_k, c_v, w_ck2b, w_cv2b)
    n_chunk_s = past // CMP_STRIDE
    n_sel_s = past // SEL_BLOCK + 1
    ns_pad = -(-n_sel_s // 128) * 128
    agg_s = _agg_matrix(n_chunk_s, n_chunk_s - 1, n_sel_s, ns_pad)
    TQS = 8
    q_pad = jnp.pad(q["q"][:, None, :], ((0, 0), (0, TQS - 1), (0, 0)))
    ocs, mask_s, _ = nsa_cmp(q_pad, kcs, vcs, agg_s, slopes, q0=past, TQ=TQS, n_real=1)
    chosen = mask_s[:, :, 0, :] > 0.5
    slot = jnp.cumsum(chosen.astype(jnp.int32), axis=-1) - 1
    hit = chosen[..., None] & (slot[..., None] == jnp.arange(N_SELECT, dtype=jnp.int32))
    idx = jnp.sum(jnp.where(hit, jnp.arange(ns_pad, dtype=jnp.int32)[:, None], 0), axis=-2)
    n_past_blk = past // SEL_BLOCK
    jc = jnp.minimum(idx, n_past_blk - 1)
    per_page = PAGE // SEL_BLOCK
    pg = jnp.take_along_axis(page_table[:, None, :], jc // per_page, axis=2)
    blk = (pg * per_page + jc % per_page).astype(jnp.int32).reshape(-1)
    val = (idx < n_past_blk).astype(jnp.int32).reshape(-1)
    kpos = (jc * SEL_BLOCK).astype(jnp.int32).reshape(-1)

    def per_head(rows):
        return jnp.repeat(rows.reshape(DB, NSA_GROUPS, DH), HPG, axis=1)
    nk = NSA_GROUPS * DH
    gate3 = q["gate"].reshape(DB, NSA_GROUPS, 128)[:, :, :3 * HPG].reshape(DB, NSA_HEADS, 3)
    gb = [jnp.broadcast_to(gate3[:, :, k:k + 1], (DB, NSA_HEADS, 128)) for k in range(3)]
    slope16 = jnp.broadcast_to(slopes[:, None], (NSA_HEADS, DH))
    oas = nsa_sel_sample(
        q["q"].reshape(DB, NSA_HEADS, DH), cache_kv.reshape(-1, SEL_BLOCK, n_row, DH), blk, val, kpos,
        cache_win.reshape(DB, Wc, 2 * NSA_GROUPS, DH),
        per_head(q["kv4"][:, 2 * nk:3 * nk]), per_head(q["kv4"][:, 3 * nk:4 * nk]),
        per_head(q["win"][:, :nk]), per_head(q["win"][:, nk:]),
        slope16, gb[0], gb[1], gb[2], ocs[:, 0].reshape(DB, NSA_HEADS, DH), past)
    hq, hz, hv, hog = [q["hg"][:, k * HG_HEADS * HG_DK:(k + 1) * HG_HEADS * HG_DK] for k in range(4)]
    ohs, s_new_s = hgrn_sample(hq.reshape(DB, HG_HEADS, HG_DK, 1), hz.reshape(DB, HG_HEADS, HG_DK, 1),
                               hv.reshape(DB, HG_HEADS, HG_DV), hog.reshape(DB, HG_HEADS, HG_DV),
                               lb.reshape(HG_HEADS, HG_DK, 1), gn, state_hgrn.reshape(state_hgrn.shape[1:]))
    h1s = _merge_out(oas.reshape(DB, NSA_HEADS * DH), ohs.reshape(DB, HG_HEADS * HG_DV), q, w, xs, gt1[ss])
    u2s = norm_mod(h1s, g2, sc2[ss], sh2[ss])
    Fp = w["up_a"].shape[1]
    a_s = mm(u2s, w["up_a"], name="ffn_s_a")
    conv_buf = state_conv.reshape(state_conv.shape[1:])
    buf = jnp.pad(conv_buf, ((0, 0), (0, 0), (0, Fp - F)))

    def conv_gate(v, a, b0, b1, cw0, cw1, cw2, cb):
        conv = cb + b0 * cw0 + b1 * cw1 + a * cw2
        return _silu(conv) * v
    hid = mm(u2s, w["up_v"], extras=[(a_s, "tile"), (buf[:, 0], "tile"), (buf[:, 1], "tile")]
             + [(w["conv"][k:k + 1], "row") for k in range(4)], epilogue=conv_gate, out_dtypes=(BF16,), name="ffn_s_v")

    def resid_norm(a, h, g, gfin):
        h2 = h + g * a
        return h2 * lax.rsqrt(jnp.mean(h2 * h2, axis=-1, keepdims=True) + EPS) * gfin
    y_s = mm(hid, w["down"], extras=[(h1s, "tile"), (gt2[ss], "tile"), (gf, "row")], epilogue=resid_norm, tn=D, name="ffn_s_down")

    kv_new_s = q["kv4"].reshape(1, DB, 1, 4, NSA_GROUPS, DH)
    win_new_s = jnp.concatenate([cache_win[:, :, 1:], q["win"].reshape(1, DB, 1, 2, NSA_GROUPS, DH)], axis=2)
    conv_new_s = jnp.stack([conv_buf[:, 1], a_s[:, :F]], axis=1)[None]
    return (y_p[None], y_s[:, None, :], kv_new_p, win_new_p, s_new_p[None, None], conv_new_p,
            kv_new_s, win_new_s, s_new_s[None], conv_new_s)
```

```python
import functools
import math

import numpy as np
import jax
import jax.numpy as jnp
from jax import lax
from jax.experimental import pallas as pl
from jax.experimental.pallas import tpu as pltpu

F32 = jnp.float32
BF16 = jnp.bfloat16

NSA_HEADS = 16
NSA_GROUPS = 4
HPG = NSA_HEADS // NSA_GROUPS
DH = 128
CMP_BLOCK = 32
CMP_STRIDE = 16
SEL_BLOCK = 64
N_SELECT = 16
WINDOW = 512
SEL_BONUS = 1.0e6
HG_HEADS = 8
HG_DK = 128
HG_DV = 128
CONV_W = 3
EPS = 1e-6
PAGE = 128

NEG = -1.0e30
LOG2E = math.log2(math.e)
V7X_VMEM_LIMIT = 56 * 1024 * 1024
RESIDENT_W_BYTES = 24 * 1024 * 1024
ROW_TILE_ELEMS = 1 << 20
CHUNKS_PER_PAGE = PAGE // CMP_STRIDE
HG_SUB = 16


def _params(sem):
    return pltpu.CompilerParams(dimension_semantics=sem, vmem_limit_bytes=V7X_VMEM_LIMIT)


def _pick(dim, target, mult):
    if dim <= target:
        return dim
    t = (target // mult) * mult
    while t >= mult:
        if dim % t == 0:
            return t
        t -= mult
    return dim


def _dot(a, b):
    return jnp.dot(a, b, preferred_element_type=F32)


def _dot_nt(a, b):
    return lax.dot_general(a, b, (((1,), (1,)), ((), ())), preferred_element_type=F32)


def _sigmoid(x):
    return 1.0 / (1.0 + jnp.exp(-x))


def _silu(x):
    return x * _sigmoid(x)


def _split3(x):
    a = x.astype(BF16)
    r = x - a.astype(F32)
    b = r.astype(BF16)
    c = (r - b.astype(F32)).astype(BF16)
    return a, b, c


def _mm_body(*refs, n_extra, act, epilogue):
    x_ref, w_ref = refs[0], refs[1]
    extras = refs[2:2 + n_extra]
    outs = refs[2 + n_extra:]
    x = x_ref[...]
    if act is not None:
        x = act(x.astype(F32))
    acc = _dot(x.astype(BF16), w_ref[...].astype(BF16))
    res = epilogue(acc, *[e[...] for e in extras])
    if not isinstance(res, tuple):
        res = (res,)
    for o, r in zip(outs, res):
        o[...] = r.astype(o.dtype)


def mm(x, w, extras=(), epilogue=lambda a: a, out_dtypes=(F32,), act=None, tm=None, tn=None, name="mm"):
    M, K = x.shape
    N = w.shape[1]
    if tn is None:
        tn = N if K * N * w.dtype.itemsize <= RESIDENT_W_BYTES else 512
    tn = _pick(N, tn, 128)
    if tm is None:
        tm = 1024 if tn < N else max(8, min(1024, ROW_TILE_ELEMS // N))
    tm = _pick(M, tm, 8)
    if tn == N:
        w_spec = pl.BlockSpec((K, N), lambda i, j: (0, 0), pipeline_mode=pl.Buffered(1))
    else:
        w_spec = pl.BlockSpec((K, tn), lambda i, j: (0, j))
    in_specs = [pl.BlockSpec((tm, K), lambda i, j: (i, 0)), w_spec]
    args = [x, w]
    for arr, kind in extras:
        if kind == "tile":
            in_specs.append(pl.BlockSpec((tm, tn), lambda i, j: (i, j)))
        else:
            in_specs.append(pl.BlockSpec((1, tn), lambda i, j: (0, j)))
        args.append(arr)
    outs = pl.pallas_call(
        functools.partial(_mm_body, n_extra=len(extras), act=act, epilogue=epilogue),
        grid=(M // tm, N // tn),
        in_specs=in_specs,
        out_specs=[pl.BlockSpec((tm, tn), lambda i, j: (i, j)) for _ in out_dtypes],
        out_shape=[jax.ShapeDtypeStruct((M, N), d) for d in out_dtypes],
        compiler_params=_params(("parallel", "parallel")),
        name=name,
    )(*args)
    return outs if len(outs) > 1 else outs[0]


def _norm_body(x_ref, g_ref, sc_ref, sh_ref, o_ref):
    x = x_ref[...].astype(F32)
    y = x * lax.rsqrt(jnp.mean(x * x, axis=-1, keepdims=True) + EPS) * g_ref[...]
    o_ref[...] = (y * (1.0 + sc_ref[...]) + sh_ref[...]).astype(o_ref.dtype)


def norm_mod(x, g, sc, sh, out_dtype=BF16):
    M, D = x.shape
    tm = _pick(M, 256, 8)
    per_row = sc.shape[0] == M and M > 1
    mod_spec = pl.BlockSpec((tm, D), lambda i: (i, 0)) if per_row else pl.BlockSpec((1, D), lambda i: (0, 0))
    return pl.pallas_call(
        _norm_body,
        grid=(M // tm,),
        in_specs=[pl.BlockSpec((tm, D), lambda i: (i, 0)), pl.BlockSpec((1, D), lambda i: (0, 0)), mod_spec, mod_spec],
        out_specs=pl.BlockSpec((tm, D), lambda i: (i, 0)),
        out_shape=jax.ShapeDtypeStruct((M, D), out_dtype),
        compiler_params=_params(("parallel",)),
        name="norm_mod",
    )(x, g, sc, sh)


def _cmpz_body(pt_ref, *refs, P, flat):
    pages = refs[:P]
    wk_ref, wv_ref, zk_ref, zv_ref = refs[P:P + 4]
    nrow = 2 * NSA_GROUPS
    if flat:
        nc = CHUNKS_PER_PAGE
        ri = lax.broadcasted_iota(jnp.int32, (nc * nrow, nc * nrow), 0)
        ci = lax.broadcasted_iota(jnp.int32, (nc * nrow, nc * nrow), 1)
        swap = (ci == (ri % nc) * nrow + ri // nc).astype(BF16)
        rp = []
        for p in range(CMP_STRIDE):
            a = jnp.concatenate([jnp.concatenate([pages[k][CMP_STRIDE * c + p] for c in range(nc)], axis=0)
                                 for k in range(P)], axis=1)
            rp.append(_dot(swap, a.astype(BF16)))

        def piece(k, p, kind, g):
            j = kind * NSA_GROUPS + g
            return rp[p][j * nc:(j + 1) * nc, k * DH:(k + 1) * DH]
    else:
        ri = lax.broadcasted_iota(jnp.int32, (PAGE, PAGE), 0)
        ci = lax.broadcasted_iota(jnp.int32, (PAGE, PAGE), 1)
        perm = (ci == (ri % CHUNKS_PER_PAGE) * CMP_STRIDE + ri // CHUNKS_PER_PAGE).astype(BF16)
        xp = [_dot(perm, pages[k][...].astype(BF16)) for k in range(P)]

        def piece(k, p, kind, g):
            col = (kind * NSA_GROUPS + g) * DH
            return xp[k][p * CHUNKS_PER_PAGE:(p + 1) * CHUNKS_PER_PAGE, col:col + DH]
    for kind, (w_ref, z_ref) in enumerate(((wk_ref, zk_ref), (wv_ref, zv_ref))):
        rows = []
        for k in range(P):
            for g in range(NSA_GROUPS):
                rows.append(jnp.concatenate([piece(k, p, kind, g) for p in range(CMP_STRIDE)], axis=1))
        y = jnp.concatenate(rows, axis=0).astype(BF16)
        z = _dot(y, w_ref[...])
        z_ref[...] = z.reshape(1, P, NSA_GROUPS, CHUNKS_PER_PAGE, 2 * DH)


def cmpz(rows, page_tab, wk_ab, wv_ab):
    B, n_pages = page_tab.shape
    P = _pick(n_pages, 16, 1)
    flat = rows.ndim == 4
    if flat:
        def page_spec(k):
            return pl.BlockSpec((None, PAGE, 2 * NSA_GROUPS, DH),
                                lambda b, j, pt: (pt[b * n_pages + j * P + k], 0, 0, 0))
    else:
        def page_spec(k):
            return pl.BlockSpec((None, PAGE, 2 * NSA_GROUPS * DH), lambda b, j, pt: (pt[b * n_pages + j * P + k], 0, 0))

    grid_spec = pltpu.PrefetchScalarGridSpec(
        num_scalar_prefetch=1,
        grid=(B, n_pages // P),
        in_specs=[page_spec(k) for k in range(P)]
        + [pl.BlockSpec((CMP_STRIDE * DH, 2 * DH), lambda b, j, pt: (0, 0))] * 2,
        out_specs=[pl.BlockSpec((1, P, NSA_GROUPS, CHUNKS_PER_PAGE, 2 * DH), lambda b, j, pt: (b, j, 0, 0, 0))] * 2,
    )
    zshape = jax.ShapeDtypeStruct((B, n_pages, NSA_GROUPS, CHUNKS_PER_PAGE, 2 * DH), F32)
    return pl.pallas_call(
        functools.partial(_cmpz_body, P=P, flat=flat),
        grid_spec=grid_spec,
        out_shape=[zshape, zshape],
        compiler_params=_params(("parallel", "parallel")),
        name="nsa_cmpz",
    )(page_tab.reshape(-1), *([rows] * P), wk_ab, wv_ab)


def _cmpfin_body(z_ref, c_ref, w2_ref, o_ref, *, n_chunk):
    z = z_ref[0, :, 0].reshape(n_chunk, 2 * DH)
    nxt = pltpu.roll(z[:, DH:], n_chunk - 1, axis=0)
    pre = z[:, :DH] + nxt + c_ref[0:1, :]
    h = jax.nn.gelu(pre)
    o_ref[0, 0] = _dot(h.astype(BF16), w2_ref[...]).astype(o_ref.dtype)


def cmp_finish(z, c_row8, w2):
    B, n_pages = z.shape[0], z.shape[1]
    n_chunk = n_pages * CHUNKS_PER_PAGE
    return pl.pallas_call(
        functools.partial(_cmpfin_body, n_chunk=n_chunk),
        grid=(B, NSA_GROUPS),
        in_specs=[pl.BlockSpec((1, n_pages, 1, CHUNKS_PER_PAGE, 2 * DH), lambda b, g: (b, 0, g, 0, 0)),
                  pl.BlockSpec((8, DH), lambda b, g: (0, 0)),
                  pl.BlockSpec((DH, DH), lambda b, g: (0, 0))],
        out_specs=pl.BlockSpec((1, 1, n_chunk, DH), lambda b, g: (b, g, 0, 0)),
        out_shape=jax.ShapeDtypeStruct((B, NSA_GROUPS, n_chunk, DH), BF16),
        compiler_params=_params(("parallel", "parallel")),
        name="nsa_cmpfin",
    )(z, c_row8, w2)


def _nsacmp_body(sl_ref, q_ref, kc_ref, vc_ref, agg_ref, oc_ref, mask_ref, any_ref, imp_ref,
                 *, TQ, NC, NS, q0, top, n_real):
    g = pl.program_id(1)
    i = pl.program_id(2)
    t0 = q0 + i * TQ
    t = t0 + lax.broadcasted_iota(jnp.int32, (TQ, 1), 0)

    def scores(ncv):
        ce = lax.broadcasted_iota(jnp.int32, (1, ncv), 1) * CMP_STRIDE + (CMP_BLOCK - 1)
        d = (t - ce).astype(F32)
        valid = d >= 0.0
        kc = kc_ref[0, 0, :ncv, :]
        vc = vc_ref[0, 0, :ncv, :]
        psum = jnp.zeros((TQ, ncv), F32)
        for r in range(HPG):
            s = _dot_nt(q_ref[0, :, r * DH:(r + 1) * DH], kc) - sl_ref[g * HPG + r] * d
            m = jnp.max(jnp.where(valid, s, NEG), axis=1, keepdims=True)
            p = jnp.where(valid, jnp.exp2(s - m), 0.0)
            p = p / jnp.maximum(jnp.sum(p, axis=1, keepdims=True), 1e-30)
            oc_ref[0, :, r * DH:(r + 1) * DH] = _dot(p.astype(BF16), vc)
            psum = psum + p
        ph = psum.astype(BF16)
        plo = (psum - ph.astype(F32)).astype(BF16)
        imp_ref[...] = _dot(ph, agg_ref[:ncv, :]) + _dot(plo, agg_ref[:ncv, :])

    nbk = min(4, NC // 128) if NC % 128 == 0 else 1
    share = NC // nbk
    n_vis = jnp.maximum(t0 + TQ - CMP_BLOCK, 0) // CMP_STRIDE + 1
    bucket = jnp.minimum((n_vis + share - 1) // share, nbk)
    for bk in range(1, nbk + 1):
        pl.when(bucket == bk)(functools.partial(scores, bk * share))
    imp = imp_ref[...]
    cur = t // SEL_BLOCK
    j = lax.broadcasted_iota(jnp.int32, (1, NS), 1)
    forced = (j == 0) | (j == cur) | (j == cur - 1)
    imp = jnp.where(j <= cur, imp + jnp.where(forced, SEL_BONUS, 0.0), -SEL_BONUS)
    if n_real is None:
        impT = imp.T
        io = lax.broadcasted_iota(jnp.int32, (NS, TQ), 0).astype(F32)
        selT = jnp.zeros((NS, TQ), F32)
        for _ in range(top):
            mx = jnp.max(impT, axis=0, keepdims=True)
            am = jnp.min(jnp.where(impT == mx, io, float(NS)), axis=0, keepdims=True)
            hit = io == am
            selT = jnp.where(hit, 1.0, selT)
            impT = jnp.where(hit, -3.0e38, impT)
        sel = selT.T
    else:
        colT = jnp.concatenate([imp, jnp.zeros((128 - TQ, NS), F32)], axis=0).T
        ii = lax.broadcasted_iota(jnp.int32, (NS, NS), 0)
        jj = lax.broadcasted_iota(jnp.int32, (NS, NS), 1)
        rowi = lax.broadcasted_iota(jnp.int32, (TQ, 1), 0)
        sel = jnp.zeros((TQ, NS), F32)
        for r in range(n_real):
            a = colT[:, r:r + 1]
            b = imp[r:r + 1, :]
            beats = (a > b) | ((a == b) & (ii < jj))
            rank = jnp.sum(jnp.where(beats, 1.0, 0.0), axis=0, keepdims=True)
            sel = jnp.where(rowi == r, jnp.where(rank < float(top), 1.0, 0.0), sel)
    mask_ref[0, 0] = sel.astype(mask_ref.dtype)
    any_ref[0, 0, 0] = jnp.broadcast_to(jnp.max(sel, axis=0, keepdims=True), (8, NS))


def nsa_cmp(q, kc, vc, agg, slopes, q0, TQ=256, n_real=None):
    B, Tq, _ = q.shape
    NC = kc.shape[2]
    NS = agg.shape[1]
    top = N_SELECT
    grid_spec = pltpu.PrefetchScalarGridSpec(
        num_scalar_prefetch=1,
        grid=(B, NSA_GROUPS, Tq // TQ),
        in_specs=[pl.BlockSpec((1, TQ, HPG * DH), lambda b, g, i, sl: (b, i, g)),
                  pl.BlockSpec((1, 1, NC, DH), lambda b, g, i, sl: (b, g, 0, 0)),
                  pl.BlockSpec((1, 1, NC, DH), lambda b, g, i, sl: (b, g, 0, 0)),
                  pl.BlockSpec((NC, NS), lambda b, g, i, sl: (0, 0))],
        out_specs=[pl.BlockSpec((1, TQ, HPG * DH), lambda b, g, i, sl: (b, i, g)),
                   pl.BlockSpec((1, 1, TQ, NS), lambda b, g, i, sl: (b, g, i, 0)),
                   pl.BlockSpec((1, 1, 1, 8, NS), lambda b, g, i, sl: (b, g, i, 0, 0))],
        scratch_shapes=[pltpu.VMEM((TQ, NS), F32)],
    )
    return pl.pallas_call(
        functools.partial(_nsacmp_body, TQ=TQ, NC=NC, NS=NS, q0=q0, top=top, n_real=n_real),
        grid_spec=grid_spec,
        out_shape=[jax.ShapeDtypeStruct((B, Tq, NSA_HEADS * DH), F32),
                   jax.ShapeDtypeStruct((B, NSA_GROUPS, Tq, NS), BF16),
                   jax.ShapeDtypeStruct((B, NSA_GROUPS, Tq // TQ, 8, NS), F32)],
        compiler_params=_params(("parallel", "parallel", "parallel")),
        name="nsa_cmp",
    )(slopes, q, kc, vc, agg)


def _nsawin_body(sl_ref, q_ref, k0_ref, k1_ref, k2_ref, v0_ref, v1_ref, v2_ref, o_ref, *, TQ):
    g = pl.program_id(0)
    qi = pl.program_id(1)
    t0 = qi * TQ
    tq = t0 + lax.broadcasted_iota(jnp.int32, (TQ, 1), 0)
    kp = t0 - 2 * TQ + lax.broadcasted_iota(jnp.int32, (1, 3 * TQ), 1)
    dist = tq - kp
    bias0 = jnp.where((dist >= 0) & (dist < WINDOW) & (kp >= 0), 0.0, NEG)
    kprel = (kp - t0).astype(F32)
    k = jnp.concatenate([k0_ref[...], k1_ref[...], k2_ref[...]], axis=0)
    v = jnp.concatenate([v0_ref[...], v1_ref[...], v2_ref[...]], axis=0)
    for r in range(HPG):
        s = _dot_nt(q_ref[:, r * DH:(r + 1) * DH], k) + (sl_ref[g * HPG + r] * kprel + bias0)
        p = jnp.exp2(s - jnp.max(s, axis=1, keepdims=True))
        o = _dot(p.astype(BF16), v) / jnp.sum(p, axis=1, keepdims=True)
        o_ref[:, r * DH:(r + 1) * DH] = o.astype(o_ref.dtype)


def nsa_win(q, winb, slopes, TQ=256):
    T = q.shape[0]
    assert WINDOW <= 2 * TQ and T % TQ == 0
    G4 = NSA_GROUPS

    def kmap(off, kind):
        return lambda g, i, sl: (jnp.maximum(i - off, 0), kind * G4 + g)

    grid_spec = pltpu.PrefetchScalarGridSpec(
        num_scalar_prefetch=1,
        grid=(NSA_GROUPS, T // TQ),
        in_specs=[pl.BlockSpec((TQ, HPG * DH), lambda g, i, sl: (i, g))]
        + [pl.BlockSpec((TQ, DH), kmap(off, 0)) for off in (2, 1, 0)]
        + [pl.BlockSpec((TQ, DH), kmap(off, 1)) for off in (2, 1, 0)],
        out_specs=pl.BlockSpec((TQ, HPG * DH), lambda g, i, sl: (i, g)),
    )
    return pl.pallas_call(
        functools.partial(_nsawin_body, TQ=TQ),
        grid_spec=grid_spec,
        out_shape=jax.ShapeDtypeStruct((T, NSA_HEADS * DH), F32),
        compiler_params=_params(("parallel", "parallel")),
        name="nsa_win",
    )(slopes, q, *([winb] * 6))


def _nsasel_body(qi_t, ki_t, fl_t, sl_ref, q_ref, k_ref, v_ref, mask_ref, e_ref, gate_ref, oc_ref, ow_ref, o_ref,
                 ms, ls, accs, *, TQ, TK, n_steps):
    g = pl.program_id(0)
    st = g * n_steps + pl.program_id(1)
    qi = qi_t[st]
    ki = ki_t[st]
    fl = fl_t[st]
    t0 = qi * TQ
    k0 = ki * TK

    @pl.when((fl & 4) == 0)
    def _step():
        @pl.when((fl & 1) != 0)
        def _init():
            ms[...] = jnp.full(ms.shape, NEG, F32)
            ls[...] = jnp.zeros(ls.shape, F32)
            accs[...] = jnp.zeros(accs.shape, F32)

        tq = t0 + lax.broadcasted_iota(jnp.int32, (TQ, 1), 0)
        kp = k0 + lax.broadcasted_iota(jnp.int32, (1, TK), 1)
        kprel = (kp - t0).astype(F32)
        sel = _dot(mask_ref[0], e_ref[...])
        bias0 = jnp.where((sel > 0.5) & (tq >= kp), 0.0, NEG)
        k = k_ref[...]
        v = v_ref[...]
        reps = TK // 128
        for r in range(HPG):
            s = _dot_nt(q_ref[:, r * DH:(r + 1) * DH], k) + (sl_ref[g * HPG + r] * kprel + bias0)
            m_old = ms[r]
            m_new = jnp.maximum(m_old, jnp.max(s, axis=1, keepdims=True))
            alpha = jnp.exp2(m_old - m_new)
            p = jnp.exp2(s - jnp.tile(m_new, (1, reps)))
            ls[r] = alpha * ls[r] + jnp.sum(p, axis=1, keepdims=True)
            accs[r] = alpha * accs[r] + _dot(p.astype(BF16), v)
            ms[r] = m_new

        @pl.when((fl & 2) != 0)
        def _finish():
            gt = gate_ref[...]
            for r in range(HPG):
                cs = slice(r * DH, (r + 1) * DH)
                o = (gt[:, 3 * r:3 * r + 1] * oc_ref[:, cs] + gt[:, 3 * r + 1:3 * r + 2] * (accs[r] / ls[r])
                     + gt[:, 3 * r + 2:3 * r + 3] * ow_ref[:, cs])
                o_ref[:, cs] = o.astype(o_ref.dtype)


def nsa_sel(q, kvb, mask, anyq, expand, gates, oc, ow, slopes, TQ=256, TK=512):
    T = q.shape[0]
    NS = mask.shape[2]
    G4 = NSA_GROUPS
    nq, nk = T // TQ, T // TK
    assert T % TK == 0 and T % TQ == 0 and NS == nk * (TK // SEL_BLOCK)
    last = (np.arange(nq) * TQ + TQ - 1) // TK
    causal = np.arange(nk)[None, :] <= last[:, None]
    n_steps = int(causal.sum())
    act = (anyq.reshape(G4, nq, -1, nk, TK // SEL_BLOCK) > 0.5).any(axis=(2, 4)) & jnp.asarray(causal)[None]
    flat = act.reshape(G4, nq * nk)
    cnt = flat.sum(axis=-1).astype(jnp.int32)
    order = jnp.argsort(jnp.logical_not(flat), axis=-1, stable=True)[:, :n_steps].astype(jnp.int32)
    skip = jnp.arange(n_steps, dtype=jnp.int32)[None, :] >= cnt[:, None]
    idx = jnp.where(skip, jnp.take_along_axis(order, (cnt - 1)[:, None], axis=1), order)
    qi_t, ki_t = idx // nk, idx % nk
    fl_t = ((ki_t == 0).astype(jnp.int32) | ((ki_t == jnp.asarray(last, jnp.int32)[qi_t]).astype(jnp.int32) << 1)
            | (skip.astype(jnp.int32) << 2))
    tabs = [x.reshape(-1).astype(jnp.int32) for x in (qi_t, ki_t, fl_t)]
    qblk = pl.BlockSpec((TQ, HPG * DH), lambda g, s, qi, ki, fl, sl: (qi[g * n_steps + s], g))
    grid_spec = pltpu.PrefetchScalarGridSpec(
        num_scalar_prefetch=4,
        grid=(NSA_GROUPS, jnp.max(cnt)),
        in_specs=[
            qblk,
            pl.BlockSpec((TK, DH), lambda g, s, qi, ki, fl, sl: (ki[g * n_steps + s], 2 * G4 + g)),
            pl.BlockSpec((TK, DH), lambda g, s, qi, ki, fl, sl: (ki[g * n_steps + s], 3 * G4 + g)),
            pl.BlockSpec((1, TQ, NS), lambda g, s, qi, ki, fl, sl: (g, qi[g * n_steps + s], 0)),
            pl.BlockSpec((NS, TK), lambda g, s, qi, ki, fl, sl: (0, ki[g * n_steps + s])),
            pl.BlockSpec((TQ, 128), lambda g, s, qi, ki, fl, sl: (qi[g * n_steps + s], g)),
            qblk,
            qblk,
        ],
        out_specs=qblk,
        scratch_shapes=[pltpu.VMEM((HPG, TQ, 128), F32), pltpu.VMEM((HPG, TQ, 128), F32),
                        pltpu.VMEM((HPG, TQ, DH), F32)],
    )
    return pl.pallas_call(
        functools.partial(_nsasel_body, TQ=TQ, TK=TK, n_steps=n_steps),
        grid_spec=grid_spec,
        out_shape=jax.ShapeDtypeStruct((T, NSA_HEADS * DH), BF16),
        compiler_params=_params(("parallel", "arbitrary")),
        name="nsa_sel",
    )(*tabs, slopes, q, kvb, kvb, mask, expand, gates, oc, ow)


def _rowsel(rowg, vals):
    out = vals[0]
    for g in range(1, NSA_GROUPS):
        out = jnp.where(rowg == g, vals[g], out)
    return out


def _nsasels_body(kpos_t, val_t, blk_t, q_ref, *refs, n_slot, sps, Wc, past):
    G4 = NSA_GROUPS
    kv_refs = refs[0:sps * G4]
    (win_ref, knew_ref, vnew_ref, kwnew_ref, vwnew_ref, sl_ref, gc_ref, gs_ref, gw_ref, oc_ref,
     o_ref, m_s, l_s, acc_s) = refs[sps * G4:]
    b = pl.program_id(0)
    s = pl.program_id(1)
    q = q_ref[0]
    rowg = lax.broadcasted_iota(jnp.int32, (NSA_HEADS, 1), 0) // HPG
    slope = sl_ref[...][:, 0:1]

    @pl.when(s == 0)
    def _init():
        m_s[...] = jnp.full(m_s.shape, NEG, F32)
        l_s[...] = jnp.zeros(l_s.shape, F32)
        acc_s[...] = jnp.zeros(acc_s.shape, F32)

    scs, oks = [], []
    m_old = m_s[...]
    m_new = m_old
    for u in range(sps):
        base = (b * NSA_GROUPS) * n_slot + s * sps + u
        sc = _rowsel(rowg, [_dot_nt(q, kv_refs[u * G4 + g][:, g, :].astype(BF16)) for g in range(G4)])
        kp0 = _rowsel(rowg, [kpos_t[base + g * n_slot] for g in range(NSA_GROUPS)])
        ok = _rowsel(rowg, [val_t[base + g * n_slot] for g in range(NSA_GROUPS)]) > 0
        dist = (past - kp0 - lax.broadcasted_iota(jnp.int32, (1, SEL_BLOCK), 1)).astype(F32)
        sc = jnp.where(ok, sc - slope * dist, NEG)
        m_new = jnp.maximum(m_new, jnp.max(sc, axis=1, keepdims=True))
        scs.append(sc)
        oks.append(ok)
    alpha = jnp.exp2(m_old - m_new)
    l_new = alpha * l_s[...]
    acc_new = alpha * acc_s[...]
    for u in range(sps):
        p = jnp.where(oks[u], jnp.exp2(scs[u] - m_new), 0.0)
        pb = p.astype(BF16)
        l_new = l_new + jnp.sum(p, axis=1, keepdims=True)
        acc_new = acc_new + _rowsel(rowg, [_dot(pb, kv_refs[u * G4 + g][:, G4 + g, :].astype(BF16)) for g in range(G4)])
    l_s[...] = l_new
    acc_s[...] = acc_new
    m_s[...] = m_new

    @pl.when(s == n_slot // sps - 1)
    def _finish():
        qf = q.astype(F32)
        sn = jnp.sum(qf * knew_ref[0], axis=1, keepdims=True)
        m1 = m_s[...]
        m2 = jnp.maximum(m1, sn)
        a2 = jnp.exp2(m1 - m2)
        pn = jnp.exp2(sn - m2)
        o_s = (a2 * acc_s[...] + pn * vnew_ref[0]) / (a2 * l_s[...] + pn)
        sw = _rowsel(rowg, [_dot_nt(q, win_ref[:, g, :].astype(BF16)) for g in range(G4)])
        dw = (Wc - lax.broadcasted_iota(jnp.int32, (1, Wc), 1)).astype(F32)
        okw = dw < float(WINDOW)
        sw = jnp.where(okw, sw - slope * dw, NEG)
        swn = jnp.sum(qf * kwnew_ref[0], axis=1, keepdims=True)
        mwin = jnp.maximum(jnp.max(sw, axis=1, keepdims=True), swn)
        pw = jnp.where(okw, jnp.exp2(sw - mwin), 0.0)
        pwn = jnp.exp2(swn - mwin)
        pwb = pw.astype(BF16)
        ow = _rowsel(rowg, [_dot(pwb, win_ref[:, G4 + g, :].astype(BF16)) for g in range(G4)])
        o_w = (ow + pwn * vwnew_ref[0]) / (jnp.sum(pw, axis=1, keepdims=True) + pwn)
        o = gc_ref[0] * oc_ref[0] + gs_ref[0] * o_s + gw_ref[0] * o_w
        o_ref[0] = o.astype(o_ref.dtype)


def nsa_sel_sample(q16, half_pages, blk, val, kpos, win_rows, knew, vnew, kwnew, vwnew, slope16, gc, gs, gw, oc16, past):
    DB = q16.shape[0]
    n_slot = blk.shape[0] // (DB * NSA_GROUPS)
    sps = _pick(n_slot, 4, 1)
    Wc = win_rows.shape[1]
    G4 = NSA_GROUPS

    def kvmap(u, g):
        return lambda b, s, kp, vl, bk: (bk[(b * G4 + g) * n_slot + s * sps + u], 0, 1, 0)

    head = lambda b, s, kp, vl, bk: (b, 0, 0)
    hspec = pl.BlockSpec((1, NSA_HEADS, DH), head)
    grid_spec = pltpu.PrefetchScalarGridSpec(
        num_scalar_prefetch=3,
        grid=(DB, n_slot // sps),
        in_specs=[hspec]
        + [pl.BlockSpec((None, SEL_BLOCK, 2 * G4, DH), kvmap(u, g)) for u in range(sps) for g in range(G4)]
        + [pl.BlockSpec((None, Wc, 2 * G4, DH), lambda b, s, kp, vl, bk: (b, 0, 0, 0))]
        + [hspec] * 4
        + [pl.BlockSpec((NSA_HEADS, DH), lambda b, s, kp, vl, bk: (0, 0))]
        + [hspec] * 4,
        out_specs=hspec,
        scratch_shapes=[pltpu.VMEM((NSA_HEADS, 1), F32), pltpu.VMEM((NSA_HEADS, 1), F32), pltpu.VMEM((NSA_HEADS, DH), F32)],
    )
    return pl.pallas_call(
        functools.partial(_nsasels_body, n_slot=n_slot, sps=sps, Wc=Wc, past=past),
        grid_spec=grid_spec,
        out_shape=jax.ShapeDtypeStruct((DB, NSA_HEADS, DH), BF16),
        compiler_params=_params(("parallel", "arbitrary")),
        name="nsa_sel_sample",
    )(kpos, val, blk, q16, *([half_pages] * (sps * G4)), win_rows, knew, vnew, kwnew, vwnew, slope16, gc, gs, gw, oc16)


def _hgrn_tile(z, qraw, v, og, lb, gn, st):
    TC = z.shape[0]
    C = HG_SUB
    nsub = TC // C
    q = _silu(qraw)
    logf = jnp.log(lb + (1.0 - lb) * _sigmoid(z))
    kk = (1.0 - lb) * _sigmoid(-z)
    tril = (lax.broadcasted_iota(jnp.int32, (TC, TC), 0) >= lax.broadcasted_iota(jnp.int32, (TC, TC), 1)).astype(BF16)
    a, b2, c2 = _split3(logf)
    G = _dot(tril, a) + _dot(tril, b2) + _dot(tril, c2)
    gl = G[TC - 1:TC, :]
    vb = v.astype(BF16)
    tok = lax.broadcasted_iota(jnp.int32, (TC, 1), 0)

    o = _dot_nt((q * jnp.exp(G)).astype(BF16), st.astype(BF16))
    kd = kk * jnp.exp(gl - G)
    upd = lax.dot_general(vb, kd.astype(BF16), (((0,), (0,)), ((), ())), preferred_element_type=F32)
    st_new = st * jnp.exp(gl) + upd

    offs = [jnp.zeros((C, HG_DV), F32)]
    for b in range(1, nsub):
        ref = G[b * C - 1:b * C, :]
        qb = q[b * C:(b + 1) * C, :] * jnp.exp(G[b * C:(b + 1) * C, :] - ref)
        kb = kk * jnp.exp(jnp.where(tok < b * C, ref - G, NEG))
        att = _dot_nt(qb.astype(BF16), kb.astype(BF16))
        offs.append(_dot(att.astype(BF16), vb))
    o = o + jnp.concatenate(offs, axis=0)

    def pick(x, s):
        return jnp.concatenate([jnp.broadcast_to(x[b * C + s:b * C + s + 1, :], (C, x.shape[1])) for b in range(nsub)], axis=0)

    sub = tok % C
    pieces = []
    for s in range(C):
        e = jnp.exp(jnp.where(sub >= s, G - pick(G, s), NEG))
        pieces.append((q * e * pick(kk, s)).astype(BF16))
    rs = _dot(jnp.concatenate(pieces, axis=0), jnp.ones((HG_DK, 128), BF16))
    for s in range(C):
        o = o + rs[s * TC:(s + 1) * TC, :] * pick(v, s)

    y = o * lax.rsqrt(jnp.mean(o * o, axis=-1, keepdims=True) + EPS) * gn * _silu(og)
    return y, st_new


def _hgrn_body(q_ref, f_ref, i_ref, og_ref, lb_ref, gn_ref, s0_ref, o_ref, sout_ref, st_ref, *, HB, nT):
    i = pl.program_id(1)

    @pl.when(i == 0)
    def _init():
        for h in range(HB):
            st_ref[h] = s0_ref[h].T

    for h in range(HB):
        cs = slice(h * HG_DK, (h + 1) * HG_DK)
        y, st_new = _hgrn_tile(f_ref[:, cs], q_ref[:, cs], i_ref[:, cs], og_ref[:, cs], lb_ref[:, cs], gn_ref[...],
                               st_ref[h])
        st_ref[h] = st_new
        o_ref[:, cs] = y.astype(o_ref.dtype)

    @pl.when(i == nT - 1)
    def _fin():
        for h in range(HB):
            sout_ref[h] = st_ref[h].T


def hgrn_prompt(hg, lb, gn, s0, TC=64, HB=8):
    T = hg.shape[0]
    H = HG_HEADS
    nT = T // TC
    nb = H // HB
    W = HB * HG_DK
    return pl.pallas_call(
        functools.partial(_hgrn_body, HB=HB, nT=nT),
        grid=(nb, nT),
        in_specs=[pl.BlockSpec((TC, W), lambda h, i: (i, h)),
                  pl.BlockSpec((TC, W), lambda h, i: (i, nb + h)),
                  pl.BlockSpec((TC, W), lambda h, i: (i, 2 * nb + h)),
                  pl.BlockSpec((TC, W), lambda h, i: (i, 3 * nb + h)),
                  pl.BlockSpec((1, W), lambda h, i: (0, h)),
                  pl.BlockSpec((1, HG_DV), lambda h, i: (0, 0)),
                  pl.BlockSpec((HB, HG_DK, HG_DV), lambda h, i: (h, 0, 0))],
        out_specs=[pl.BlockSpec((TC, W), lambda h, i: (i, h)),
                   pl.BlockSpec((HB, HG_DK, HG_DV), lambda h, i: (h, 0, 0))],
        out_shape=[jax.ShapeDtypeStruct((T, H * HG_DV), BF16), jax.ShapeDtypeStruct((H, HG_DK, HG_DV), F32)],
        scratch_shapes=[pltpu.VMEM((HB, HG_DV, HG_DK), F32)],
        compiler_params=_params(("parallel", "arbitrary")),
        name="hgrn_prompt",
    )(hg, hg, hg, hg, lb, gn, s0)


def _hgrns_body(qc_ref, zc_ref, v_ref, og_ref, lbc_ref, gn_ref, s_ref, o_ref, sout_ref):
    outs = []
    for h in range(HG_HEADS):
        z = zc_ref[0, h]
        lb = lbc_ref[h]
        f = lb + (1.0 - lb) * _sigmoid(z)
        kk = (1.0 - lb) * _sigmoid(-z)
        q = _silu(qc_ref[0, h])
        v = v_ref[0, h:h + 1, :]
        s_new = f * s_ref[0, h] + kk * v
        sout_ref[0, h] = s_new
        o = jnp.sum(q * s_new, axis=0, keepdims=True)
        og = og_ref[0, h:h + 1, :]
        outs.append(o * lax.rsqrt(jnp.mean(o * o, axis=-1, keepdims=True) + EPS) * gn_ref[...] * _silu(og))
    o_ref[0] = jnp.concatenate(outs, axis=0).astype(o_ref.dtype)


def hgrn_sample(qcol, zcol, v, og, lbcol, gn, s0):
    DB = v.shape[0]
    H = HG_HEADS
    col = pl.BlockSpec((1, H, HG_DK, 1), lambda b: (b, 0, 0, 0))
    rowb = pl.BlockSpec((1, H, HG_DV), lambda b: (b, 0, 0))
    st = pl.BlockSpec((1, H, HG_DK, HG_DV), lambda b: (b, 0, 0, 0))
    return pl.pallas_call(
        _hgrns_body,
        grid=(DB,),
        in_specs=[col, col, rowb, rowb, pl.BlockSpec((H, HG_DK, 1), lambda b: (0, 0, 0)),
                  pl.BlockSpec((1, HG_DV), lambda b: (0, 0)), st],
        out_specs=[rowb, st],
        out_shape=[jax.ShapeDtypeStruct((DB, H, HG_DV), BF16), jax.ShapeDtypeStruct((DB, H, HG_DK, HG_DV), F32)],
        compiler_params=_params(("parallel",)),
        name="hgrn_sample",
    )(qcol, zcol, v, og, lbcol, gn, s0)


def _ffn_body(u_ref, uh_ref, *refs, nf, tm, ks):
    wa_refs, wv_refs = refs[:ks], refs[ks:2 * ks]
    cp_ref, wd_ref, h_ref, gt_ref, gf_ref, y_ref, hid_ref = refs[2 * ks:]
    i = pl.program_id(0)
    f = pl.program_id(1)
    dk = u_ref.shape[1] // ks

    def up(x_ref, w_refs):
        acc = _dot(x_ref[:, :dk], w_refs[0][...])
        for c in range(1, ks):
            acc = acc + _dot(x_ref[:, c * dk:(c + 1) * dk], w_refs[c][...])
        return acc

    a = up(u_ref, wa_refs)
    v = up(u_ref, wv_refs)
    ah = up(uh_ref, wa_refs) * (i > 0).astype(F32)
    cp = cp_ref[...]
    rows = lax.broadcasted_iota(jnp.int32, (tm, 1), 0)
    a1 = jnp.where(rows == 0, ah[7:8, :], pltpu.roll(a, 1, axis=0))
    a2 = jnp.where(rows == 0, ah[6:7, :], jnp.where(rows == 1, ah[7:8, :], pltpu.roll(a, 2, axis=0)))
    conv = cp[3:4, :] + a2 * cp[0:1, :] + a1 * cp[1:2, :] + a * cp[2:3, :]
    hid_ref[f] = (_silu(conv) * v).astype(BF16)

    @pl.when(f == nf - 1)
    def _fin():
        hid = jnp.concatenate([hid_ref[k] for k in range(nf)], axis=1)
        h2 = h_ref[...] + gt_ref[...] * _dot(hid, wd_ref[...])
        y_ref[...] = h2 * lax.rsqrt(jnp.mean(h2 * h2, axis=-1, keepdims=True) + EPS) * gf_ref[...]


def ffn_prompt(u2, h1, wa, wv, cp, wd, gt2, gf, tm=512, tf=512):
    M, D = u2.shape
    Fp = wa.shape[1]
    tm = _pick(M, tm, 8)
    tf = _pick(Fp, tf, 128)
    nf = Fp // tf
    hb = tm // 8
    once = pl.Buffered(1)
    wa = wa.reshape(D, nf, tf).transpose(1, 0, 2)
    wv = wv.reshape(D, nf, tf).transpose(1, 0, 2)
    ks = 4
    dk = D // ks
    wspecs = [pl.BlockSpec((None, dk, tf), (lambda i, f, c=c: (f, c, 0))) for c in range(ks)]
    return pl.pallas_call(
        functools.partial(_ffn_body, nf=nf, tm=tm, ks=ks),
        grid=(M // tm, nf),
        in_specs=[pl.BlockSpec((tm, D), lambda i, f: (i, 0)),
                  pl.BlockSpec((8, D), lambda i, f: (jnp.maximum(i * hb - 1, 0), 0))]
        + wspecs + wspecs
        + [pl.BlockSpec((8, tf), lambda i, f: (0, f)),
                  pl.BlockSpec((Fp, D), lambda i, f: (0, 0), pipeline_mode=once),
                  pl.BlockSpec((tm, D), lambda i, f: (i, 0), pipeline_mode=once),
                  pl.BlockSpec((1, D), lambda i, f: (0, 0)),
                  pl.BlockSpec((1, D), lambda i, f: (0, 0))],
        out_specs=pl.BlockSpec((tm, D), lambda i, f: (i, 0), pipeline_mode=once),
        out_shape=jax.ShapeDtypeStruct((M, D), F32),
        scratch_shapes=[pltpu.VMEM((nf, tm, tf), BF16)],
        compiler_params=pltpu.CompilerParams(dimension_semantics=("parallel", "arbitrary"),
                                             vmem_limit_bytes=V7X_VMEM_LIMIT + 4 * 1024 * 1024),
        name="ffn_prompt",
    )(u2, u2, *([wa] * ks), *([wv] * ks), cp, wd, h1, gt2, gf)


def _alibi_slopes():
    h = np.arange(1, NSA_HEADS + 1, dtype=np.float32)
    return np.asarray(2.0 ** (-8.0 * h / NSA_HEADS), dtype=np.float32)


def _agg_matrix(n_chunk, n_cmp, n_sel, ns_pad):
    i = np.arange(n_chunk)[:, None] * CMP_STRIDE
    j = np.arange(ns_pad)[None, :] * SEL_BLOCK
    m = (i <= j + SEL_BLOCK - 1) & (i + CMP_BLOCK - 1 >= j)
    m &= (np.arange(n_chunk)[:, None] < n_cmp) & (np.arange(ns_pad)[None, :] < n_sel)
    return jnp.asarray(m.astype(np.float32), BF16)


def _layer_weights(w_in, w_ck1, w_cv1, w_br_a, w_br_b, w_out, w_up, conv_w, conv_b, w_down, w_ada):
    D = w_in.shape[0]
    nq, nkv = NSA_HEADS * DH, NSA_GROUPS * DH
    o = 0
    w = {}
    w["q"] = w_in[:, o:o + nq]; o += nq
    w["kv4"] = w_in[:, o:o + 4 * nkv]; o += 4 * nkv
    w["win"] = w_in[:, o:o + 2 * nkv]; o += 2 * nkv
    gate = w_in[:, o:o + 3 * NSA_HEADS]; o += 3 * NSA_HEADS
    gate = jnp.pad(gate.reshape(D, NSA_GROUPS, 3 * HPG), ((0, 0), (0, 0), (0, 128 - 3 * HPG)))
    w["gate"] = gate.reshape(D, NSA_GROUPS * 128)
    nh = 4 * HG_HEADS * HG_DK
    w["hg"] = w_in[:, o:o + nh]; o += nh
    w["ma"] = w_in[:, o:o + D]; o += D
    w["mb"] = w_in[:, o:o + D]; o += D
    half = CMP_STRIDE * DH
    w["ck_ab"] = jnp.concatenate([w_ck1[:half], w_ck1[half:]], axis=1)
    w["cv_ab"] = jnp.concatenate([w_cv1[:half], w_cv1[half:]], axis=1)
    w["br_a"], w["br_b"], w["out"], w["ada"] = w_br_a, w_br_b, w_out, w_ada
    F = w_down.shape[0]
    Fp = -(-F // 512) * 512
    w["up_a"] = jnp.pad(w_up[:, :F], ((0, 0), (0, Fp - F)))
    w["up_v"] = jnp.pad(w_up[:, F:], ((0, 0), (0, Fp - F)))
    w["down"] = jnp.pad(w_down, ((0, Fp - F), (0, 0)))
    w = {k: v.astype(BF16) for k, v in w.items()}
    cp = jnp.concatenate([conv_w, conv_b[None, :], jnp.zeros((8 - CONV_W - 1, F), F32)], axis=0)
    w["conv"] = jnp.pad(cp, ((0, 0), (0, Fp - F)))
    return w


def _project(u, w):
    scale = DH ** -0.5 * LOG2E
    p = {}
    p["q"] = mm(u, w["q"], epilogue=lambda a: a * scale, out_dtypes=(BF16,), name="proj_q")
    p["kv4"], p["kv4b"] = mm(u, w["kv4"], epilogue=lambda a: (a, a), out_dtypes=(F32, BF16), name="proj_kv")
    p["win"], p["winb"] = mm(u, w["win"], epilogue=lambda a: (a, a), out_dtypes=(F32, BF16), name="proj_win")
    p["gate"] = mm(u, w["gate"], epilogue=_sigmoid, name="proj_gate")
    p["hg"] = mm(u, w["hg"], name="proj_hg")
    p["ma"] = mm(u, w["ma"], epilogue=_sigmoid, out_dtypes=(BF16,), name="proj_ma")
    p["mb"] = mm(u, w["mb"], epilogue=_sigmoid, out_dtypes=(BF16,), name="proj_mb")
    return p


def _compressed_kv(rows3, page_tab, w, c_k, c_v, w_ck2, w_cv2):
    zk, zv = cmpz(rows3, page_tab, w["ck_ab"], w["cv_ab"])
    return cmp_finish(zk, c_k, w_ck2), cmp_finish(zv, c_v, w_cv2)


def _merge_out(oa, oh, p, w, x, gt1):
    gkind = "row" if gt1.shape[0] == 1 else "tile"
    a1 = mm(oa, w["br_a"], extras=[(p["ma"], "tile")], epilogue=lambda a, m: a * m.astype(F32),
            out_dtypes=(BF16,), name="branch_a")
    mix = mm(oh, w["br_b"], extras=[(p["mb"], "tile"), (a1, "tile")],
             epilogue=lambda a, m, prev: a * m.astype(F32) + prev.astype(F32), out_dtypes=(BF16,), name="branch_b")
    return mm(mix, w["out"], extras=[(x, "tile"), (gt1, gkind)], epilogue=lambda a, xr, g: xr + g * a, name="out_proj")


def kernel(x_prompt, x_sample, cache_kv, cache_win, state_hgrn, state_conv, page_table, c_prompt, c_sample, w_ada, b_ada, g_norm1, w_in, nsa_pos_k, nsa_pos_v, w_ck1, w_ck2, w_cv1, w_cv2, hg_lb_logits, hg_norm, w_br_a, w_br_b, w_out, g_norm2, w_up, conv_w, conv_b, w_down, g_final):
    B, T, D = x_prompt.shape
    DB = x_sample.shape[0]
    depth = w_in.shape[0]
    assert B == 1 and x_sample.shape[1] == 1 and depth == 1 and T % 512 == 0 and T >= N_SELECT * SEL_BLOCK
    n_pages = page_table.shape[1]
    past = n_pages * PAGE
    Wc = cache_win.shape[2]
    F = w_down.shape[1]
    slopes = jnp.asarray(_alibi_slopes() * np.float32(LOG2E))
    lower_bounds = jnp.cumsum(jax.nn.softmax(hg_lb_logits.astype(F32), axis=0), axis=0)
    layer = 0
    w = _layer_weights(w_in[layer], w_ck1[layer], w_cv1[layer], w_br_a[layer], w_br_b[layer], w_out[layer],
                       w_up[layer], conv_w[layer], conv_b[layer], w_down[layer], w_ada[layer])
    w_ck2b, w_cv2b = w_ck2[layer].astype(BF16), w_cv2[layer].astype(BF16)
    lb = lower_bounds[layer][None, :]
    gn = hg_norm[layer][None, :]
    g1, g2, gf = g_norm1[layer][None, :], g_norm2[layer][None, :], g_final[None, :]

    n_c = -(-(B + DB) // 8) * 8
    c_all = jnp.pad(jnp.concatenate([c_prompt, c_sample], axis=0), ((0, n_c - B - DB), (0, 0)))
    mod = mm(c_all, w["ada"], extras=[(b_ada[layer][None, :], "row")], epilogue=lambda a, b: a + b, act=_silu, name="adaln")
    sh1, sc1, gt1, sh2, sc2, gt2 = [mod[:, k * D:(k + 1) * D] for k in range(6)]
    ps, ss = slice(0, 1), slice(B, B + DB)

    def pos_term(pos, w1):
        return mm(jnp.pad(pos.reshape(1, -1), ((0, 7), (0, 0))), w1.astype(BF16), name="cmp_pos")
    c_k, c_v = pos_term(nsa_pos_k[layer], w_ck1[layer]), pos_term(nsa_pos_v[layer], w_cv1[layer])

    xp = x_prompt[0]
    u = norm_mod(xp, g1, sc1[ps], sh1[ps])
    p = _project(u, w)
    ident = jnp.arange(T // PAGE, dtype=jnp.int32)[None, :]
    kc, vc = _compressed_kv(p["kv4"].reshape(T // PAGE, PAGE, 4 * NSA_GROUPS * DH), ident, w, c_k, c_v, w_ck2b, w_cv2b)
    n_chunk = T // CMP_STRIDE
    n_sel = T // SEL_BLOCK
    agg = _agg_matrix(n_chunk, n_chunk - 1, n_sel, n_sel)
    oc, mask, anyq = nsa_cmp(p["q"][None], kc, vc, agg, slopes, q0=0)
    expand = jnp.asarray(np.kron(np.eye(n_sel, dtype=np.float32), np.ones((1, SEL_BLOCK), np.float32)), BF16)
    ow = nsa_win(p["q"], p["winb"], slopes)
    oa = nsa_sel(p["q"], p["kv4b"], mask[0], anyq[0, :, :, 0, :], expand, p["gate"], oc[0], ow, slopes)
    s0p = jnp.zeros((HG_HEADS, HG_DK, HG_DV), F32)
    oh, s_new_p = hgrn_prompt(p["hg"], lb, gn, s0p)
    h1 = _merge_out(oa, oh, p, w, xp, gt1[ps])
    u2 = norm_mod(h1, g2, sc2[ps], sh2[ps])
    y_p = ffn_prompt(u2, h1, w["up_a"], w["up_v"], w["conv"], w["down"], gt2[ps], gf)
    a_tail = mm(u2[T - 8:], w["up_a"], name="ffn_tail")
    kv_new_p = p["kv4"].reshape(1, 1, T, 4, NSA_GROUPS, DH)
    wk = min(WINDOW, T)
    win_new_p = p["win"][T - wk:].reshape(1, 1, wk, 2, NSA_GROUPS, DH)
    conv_new_p = a_tail[8 - (CONV_W - 1):, :F].reshape(1, 1, CONV_W - 1, F)

    xs = x_sample[:, 0]
    us = norm_mod(xs, g1, sc1[ss], sh1[ss])
    q = _project(us, w)
    n_row = 4 * NSA_GROUPS
    kcs, vcs = _compressed_kv(cache_kv.reshape(-1, PAGE, n_row, DH), page_table, w, c_k, c_v, w_ck2b, w_cv2b)
    n_chunk_s = past // CMP_STRIDE
    n_sel_s = past // SEL_BLOCK + 1
    ns_pad = -(-n_sel_s // 128) * 128
    agg_s = _agg_matrix(n_chunk_s, n_chunk_s - 1, n_sel_s, ns_pad)
    TQS = 8
    q_pad = jnp.pad(q["q"][:, None, :], ((0, 0), (0, TQS - 1), (0, 0)))
    ocs, mask_s, _ = nsa_cmp(q_pad, kcs, vcs, agg_s, slopes, q0=past, TQ=TQS, n_real=1)
    chosen = mask_s[:, :, 0, :] > 0.5
    slot = jnp.cumsum(chosen.astype(jnp.int32), axis=-1) - 1
    hit = chosen[..., None] & (slot[..., None] == jnp.arange(N_SELECT, dtype=jnp.int32))
    idx = jnp.sum(jnp.where(hit, jnp.arange(ns_pad, dtype=jnp.int32)[:, None], 0), axis=-2)
    n_past_blk = past // SEL_BLOCK
    jc = jnp.minimum(idx, n_past_blk - 1)
    per_page = PAGE // SEL_BLOCK
    pg = jnp.take_along_axis(page_table[:, None, :], jc // per_page, axis=2)
    blk = (pg * per_page + jc % per_page).astype(jnp.int32).reshape(-1)
    val = (idx < n_past_blk).astype(jnp.int32).reshape(-1)
    kpos = (jc * SEL_BLOCK).astype(jnp.int32).reshape(-1)

    def per_head(rows):
        return jnp.repeat(rows.reshape(DB, NSA_GROUPS, DH), HPG, axis=1)
    nk = NSA_GROUPS * DH
    gate3 = q["gate"].reshape(DB, NSA_GROUPS, 128)[:, :, :3 * HPG].reshape(DB, NSA_HEADS, 3)
    gb = [jnp.broadcast_to(gate3[:, :, k:k + 1], (DB, NSA_HEADS, 128)) for k in range(3)]
    slope16 = jnp.broadcast_to(slopes[:, None], (NSA_HEADS, DH))
    oas = nsa_sel_sample(
        q["q"].reshape(DB, NSA_HEADS, DH), cache_kv.reshape(-1, SEL_BLOCK, n_row, DH), blk, val, kpos,
        cache_win.reshape(DB, Wc, 2 * NSA_GROUPS, DH),
        per_head(q["kv4"][:, 2 * nk:3 * nk]), per_head(q["kv4"][:, 3 * nk:4 * nk]),
        per_head(q["win"][:, :nk]), per_head(q["win"][:, nk:]),
        slope16, gb[0], gb[1], gb[2], ocs[:, 0].reshape(DB, NSA_HEADS, DH), past)
    hq, hz, hv, hog = [q["hg"][:, k * HG_HEADS * HG_DK:(k + 1) * HG_HEADS * HG_DK] for k in range(4)]
    ohs, s_new_s = hgrn_sample(hq.reshape(DB, HG_HEADS, HG_DK, 1), hz.reshape(DB, HG_HEADS, HG_DK, 1),
                               hv.reshape(DB, HG_HEADS, HG_DV), hog.reshape(DB, HG_HEADS, HG_DV),
                               lb.reshape(HG_HEADS, HG_DK, 1), gn, state_hgrn.reshape(state_hgrn.shape[1:]))
    h1s = _merge_out(oas.reshape(DB, NSA_HEADS * DH), ohs.reshape(DB, HG_HEADS * HG_DV), q, w, xs, gt1[ss])
    u2s = norm_mod(h1s, g2, sc2[ss], sh2[ss])
    Fp = w["up_a"].shape[1]
    a_s = mm(u2s, w["up_a"], name="ffn_s_a")
    conv_buf = state_conv.reshape(state_conv.shape[1:])
    buf = jnp.pad(conv_buf, ((0, 0), (0, 0), (0, Fp - F)))

    def conv_gate(v, a, b0, b1, cw0, cw1, cw2, cb):
        conv = cb + b0 * cw0 + b1 * cw1 + a * cw2
        return _silu(conv) * v
    hid = mm(u2s, w["up_v"], extras=[(a_s, "tile"), (buf[:, 0], "tile"), (buf[:, 1], "tile")]
             + [(w["conv"][k:k + 1], "row") for k in range(4)], epilogue=conv_gate, out_dtypes=(BF16,), name="ffn_s_v")

    def resid_norm(a, h, g, gfin):
        h2 = h + g * a
        return h2 * lax.rsqrt(jnp.mean(h2 * h2, axis=-1, keepdims=True) + EPS) * gfin
    y_s = mm(hid, w["down"], extras=[(h1s, "tile"), (gt2[ss], "tile"), (gf, "row")], epilogue=resid_norm, tn=D, name="ffn_s_down")

    kv_new_s = q["kv4"].reshape(1, DB, 1, 4, NSA_GROUPS, DH)
    win_new_s = jnp.concatenate([cache_win[:, :, 1:], q["win"].reshape(1, DB, 1, 2, NSA_GROUPS, DH)], axis=2)
    conv_new_s = jnp.stack([conv_buf[:, 1], a_s[:, :F]], axis=1)[None]
    return (y_p[None], y_s[:, None, :], kv_new_p, win_new_p, s_new_p[None, None], conv_new_p,
            kv_new_s, win_new_s, s_new_s[None], conv_new_s)
```
